```python
import math
import jax, jax.numpy as jnp
from jax import lax
import numpy as np

D_MODEL = 2048
BATCH = 2
SEQ = 8192
DEPTH = 2

N_EVEN = (DEPTH + 1) // 2
N_ODD = DEPTH // 2
NORM_EPS = 1e-6

ATTN_GROUPS = ((128, 1), (512, 4), (2048, 16))
ATTN_HEADS_PER_GROUP = 8
ATTN_HEAD_DIM = 64
ATTN_BLOCK = 128
ATTN_GROUP_WIDTH = ATTN_HEADS_PER_GROUP * ATTN_HEAD_DIM
ATTN_QKV_WIDTH = 3 * len(ATTN_GROUPS) * ATTN_GROUP_WIDTH
ATTN_OUT_WIDTH = ATTN_GROUP_WIDTH

POOL_WINDOWS = (2, 4, 8, 16)
POOL_GROUP_WIDTH = D_MODEL // 16
POOL_WIDTH = len(POOL_WINDOWS) * POOL_GROUP_WIDTH

EVEN_IN_WIDTH = ATTN_QKV_WIDTH + POOL_WIDTH
EVEN_OUT_WIDTH = ATTN_OUT_WIDTH + POOL_WIDTH

SSM_EXPAND = 2
SSM_D_INNER = SSM_EXPAND * D_MODEL
SSM_HEAD_DIM = 64
SSM_HEADS = SSM_D_INNER // SSM_HEAD_DIM
SSM_GROUPS = 8
SSM_STATE = 128
SSM_CONV = 4
SSM_CHUNK = 128
SSM_CONV_DIM = SSM_D_INNER + 2 * SSM_GROUPS * SSM_STATE
SSM_IN_WIDTH = SSM_D_INNER + SSM_CONV_DIM + SSM_HEADS

FFN_HIDDEN = 4 * D_MODEL

kernel_name = "hybrid_dilated_attn_pool_ssd_adaln"


def rmsnorm(x, g):
    xf = x.astype(jnp.float32)
    y = xf * lax.rsqrt(jnp.mean(xf * xf, axis=-1, keepdims=True) + NORM_EPS)
    return (y * g.astype(jnp.float32)).astype(x.dtype)


def modulate(h, shift, scale):
    return h * (1 + scale[:, None, :]) + shift[:, None, :]


def dilated_window_attention(q, k, v, dilation, n_back):
    b, s, h, e = q.shape
    L = s // dilation
    nb = -(-L // ATTN_BLOCK)
    Lp = nb * ATTN_BLOCK

    def to_sub(t):
        t = t.reshape(b, L, dilation, h, e).transpose(0, 2, 3, 1, 4)
        return jnp.pad(t, ((0, 0), (0, 0), (0, 0), (0, Lp - L), (0, 0)))

    def windows(t):
        t = jnp.pad(to_sub(t), ((0, 0), (0, 0), (0, 0), (ATTN_BLOCK, 0), (0, 0)))
        t = t.reshape(b, dilation, h, nb + 1, ATTN_BLOCK, e)
        return jnp.concatenate([t[:, :, :, :-1], t[:, :, :, 1:]], axis=4)

    qs = to_sub(q).reshape(b, dilation, h, nb, ATTN_BLOCK, e)
    ks, vs = windows(k), windows(v)
    scores = jnp.einsum('bdhnqe,bdhnke->bdhnqk', qs, ks).astype(jnp.float32) * (e ** -0.5)
    qi = jnp.arange(ATTN_BLOCK)[:, None]
    kj = jnp.arange(2 * ATTN_BLOCK)[None, :]
    dist = qi + ATTN_BLOCK - kj
    kpos = jnp.arange(nb)[:, None, None] * ATTN_BLOCK - ATTN_BLOCK + kj[None]
    mask = (dist >= 0) & (dist <= n_back) & (kpos >= 0)
    scores = jnp.where(mask, scores, -jnp.inf)
    m = jnp.max(scores, axis=-1, keepdims=True)
    p = jnp.exp(scores - m)
    den = jnp.sum(p, axis=-1, keepdims=True)
    o = jnp.einsum('bdhnqk,bdhnke->bdhnqe', (p / den).astype(v.dtype), vs)
    lse = (m + jnp.log(den))[..., 0]

    def from_sub(t):
        t = t.reshape(b, dilation, h, Lp, *t.shape[5:])[:, :, :, :L]
        t = jnp.moveaxis(t, 3, 1)
        return t.reshape(b, s, h, *t.shape[4:])

    return from_sub(o), from_sub(lse)


def multiscale_pool(u, pool_w, pool_scale):
    b, s, _ = u.shape
    ug = u.reshape(b, s, len(POOL_WINDOWS), POOL_GROUP_WIDTH).astype(jnp.float32)
    cs = jnp.pad(jnp.cumsum(ug, axis=1), ((0, 0), (1, 0), (0, 0), (0, 0)))
    t = jnp.arange(s)
    diffs = []
    for gi, w in enumerate(POOL_WINDOWS):
        csg = cs[:, :, gi]
        lo = jnp.maximum(t + 1 - w, 0)
        cnt = jnp.minimum(t + 1, w).astype(jnp.float32)
        mean = (csg[:, 1:] - csg[:, lo]) / cnt[None, :, None]
        diffs.append(mean - ug[:, :, gi])
    d = jnp.stack(diffs, axis=2).astype(u.dtype)
    y = jnp.einsum('bsgc,gce->bsge', d, pool_w)
    return y.reshape(b, s, POOL_WIDTH) * pool_scale


def attn_pool_mixer(h, w_in, pool_w, pool_scale, w_out):
    b, s, _ = h.shape
    proj = h @ w_in
    qkv = proj[..., :ATTN_QKV_WIDTH].reshape(b, s, 3, len(ATTN_GROUPS), ATTN_HEADS_PER_GROUP, ATTN_HEAD_DIM)
    u = proj[..., ATTN_QKV_WIDTH:]
    outs, lses = [], []
    for gi, (window, dil) in enumerate(ATTN_GROUPS):
        o, lse = dilated_window_attention(qkv[:, :, 0, gi], qkv[:, :, 1, gi], qkv[:, :, 2, gi], dil, window // dil)
        outs.append(o)
        lses.append(lse)
    wts = jax.nn.softmax(jnp.stack(lses, axis=0), axis=0)
    attn = jnp.sum(wts[..., None] * jnp.stack(outs, axis=0).astype(jnp.float32), axis=0)
    attn = attn.astype(h.dtype).reshape(b, s, ATTN_OUT_WIDTH)
    pool = multiscale_pool(u, pool_w, pool_scale)
    return jnp.concatenate([attn, pool], axis=-1) @ w_out


def causal_depthwise_conv(x, w, bias):
    y = lax.conv_general_dilated(x, w[:, None, :], window_strides=(1,), padding=[(SSM_CONV - 1, 0)],
                                 dimension_numbers=('NWC', 'WIO', 'NWC'), feature_group_count=x.shape[-1])
    return y + bias


def ssd_scan(x, dt, A, bm, cm):
    b, s, h, p = x.shape
    g, n = bm.shape[2], bm.shape[3]
    r = h // g
    q = SSM_CHUNK
    nc = s // q
    xc = x.astype(jnp.float32).reshape(b, nc, q, g, r, p)
    bc = bm.astype(jnp.float32).reshape(b, nc, q, g, n)
    cc = cm.astype(jnp.float32).reshape(b, nc, q, g, n)
    dtc = dt.reshape(b, nc, q, g, r)
    a = jnp.cumsum(dtc * A.reshape(g, r), axis=2)
    xdt = xc * dtc[..., None]
    causal = jnp.tril(jnp.ones((q, q), dtype=bool))[:, :, None, None]
    decay = jnp.exp(jnp.where(causal, a[:, :, :, None] - a[:, :, None], -jnp.inf))
    cb = jnp.einsum('bclgn,bcsgn->bclsg', cc, bc)
    y_diag = jnp.einsum('bclsgr,bcsgrp->bclgrp', cb[..., None] * decay, xdt)
    decay_to_end = jnp.exp(a[:, :, -1:] - a)
    states = jnp.einsum('bcsgn,bcsgrp->bcgrpn', bc, xdt * decay_to_end[..., None])
    chunk_decay = jnp.exp(a[:, :, -1])

    def step(carry, inp):
        st, dec = inp
        return carry * dec[..., None, None] + st, carry

    init = jnp.zeros((b, g, r, p, n), jnp.float32)
    _, h_in = lax.scan(step, init, (jnp.moveaxis(states, 1, 0), jnp.moveaxis(chunk_decay, 1, 0)))
    h_in = jnp.moveaxis(h_in, 0, 1)
    y_off = jnp.einsum('bclgn,bcgrpn->bclgrp', cc, h_in) * jnp.exp(a)[..., None]
    return (y_diag + y_off).reshape(b, s, h, p)


def gated_group_rmsnorm(y, z, g):
    yf = y.astype(jnp.float32) * jax.nn.silu(z.astype(jnp.float32))
    b, s, dim = yf.shape
    yg = yf.reshape(b, s, SSM_GROUPS, dim // SSM_GROUPS)
    yg = yg * lax.rsqrt(jnp.mean(yg * yg, axis=-1, keepdims=True) + NORM_EPS)
    return (yg.reshape(b, s, dim) * g.astype(jnp.float32)).astype(z.dtype)


def ssd_mixer(h, w_in, conv_w, conv_b, dt_bias, a_log, d_skip, norm_g, w_out):
    b, s, _ = h.shape
    proj = h @ w_in
    z = proj[..., :SSM_D_INNER]
    xbc = proj[..., SSM_D_INNER:SSM_D_INNER + SSM_CONV_DIM]
    dt = proj[..., SSM_D_INNER + SSM_CONV_DIM:]
    xbc = jax.nn.silu(causal_depthwise_conv(xbc, conv_w, conv_b))
    xs = xbc[..., :SSM_D_INNER].reshape(b, s, SSM_HEADS, SSM_HEAD_DIM)
    bm = xbc[..., SSM_D_INNER:SSM_D_INNER + SSM_GROUPS * SSM_STATE].reshape(b, s, SSM_GROUPS, SSM_STATE)
    cm = xbc[..., SSM_D_INNER + SSM_GROUPS * SSM_STATE:].reshape(b, s, SSM_GROUPS, SSM_STATE)
    dt = jax.nn.softplus((dt + dt_bias).astype(jnp.float32))
    A = -jnp.exp(a_log.astype(jnp.float32))
    y = ssd_scan(xs, dt, A, bm, cm)
    y = y + d_skip.astype(jnp.float32)[:, None] * xs.astype(jnp.float32)
    y = gated_group_rmsnorm(y.reshape(b, s, SSM_D_INNER), z, norm_g)
    return y @ w_out


def squared_relu_mlp(h, w1, w2):
    return jnp.square(jax.nn.relu(h @ w1)) @ w2


def setup_inputs(seed: int = 0) -> dict:
    key = jax.random.key(seed)
    ks = jax.random.split(key, 24)
    f32 = jnp.float32
    nrm = lambda k, shape, scale: jax.random.normal(k, shape, f32) * scale
    D = D_MODEL
    dt0 = jnp.exp(jax.random.uniform(ks[16], (N_ODD, SSM_HEADS), f32, minval=math.log(1e-3), maxval=math.log(1e-1)))
    return {
        "x": nrm(ks[0], (BATCH, SEQ, D), 1.0),
        "c": nrm(ks[1], (BATCH, D), 1.0),
        "ada_w": nrm(ks[2], (DEPTH, D, 6 * D), 0.5 * D ** -0.5),
        "ada_b": nrm(ks[3], (DEPTH, 6 * D), 0.02),
        "norm_mix": 1.0 + nrm(ks[4], (DEPTH, D), 0.02),
        "norm_ffn": 1.0 + nrm(ks[5], (DEPTH, D), 0.02),
        "ffn_w1": nrm(ks[6], (DEPTH, D, FFN_HIDDEN), D ** -0.5),
        "ffn_w2": nrm(ks[7], (DEPTH, FFN_HIDDEN, D), FFN_HIDDEN ** -0.5),
        "even_w_in": nrm(ks[8], (N_EVEN, D, EVEN_IN_WIDTH), D ** -0.5),
        "pool_w": nrm(ks[9], (N_EVEN, len(POOL_WINDOWS), POOL_GROUP_WIDTH, POOL_GROUP_WIDTH), POOL_GROUP_WIDTH ** -0.5),
        "pool_scale": 1.0 + nrm(ks[10], (N_EVEN, POOL_WIDTH), 0.02),
        "even_w_out": nrm(ks[11], (N_EVEN, EVEN_OUT_WIDTH, D), EVEN_OUT_WIDTH ** -0.5),
        "ssm_w_in": nrm(ks[12], (N_ODD, D, SSM_IN_WIDTH), D ** -0.5),
        "ssm_conv_w": nrm(ks[13], (N_ODD, SSM_CONV, SSM_CONV_DIM), SSM_CONV ** -0.5),
        "ssm_conv_b": nrm(ks[14], (N_ODD, SSM_CONV_DIM), 0.02),
        "ssm_dt_bias": dt0 + jnp.log(-jnp.expm1(-dt0)),
        "ssm_a_log": jnp.log(jax.random.uniform(ks[15], (N_ODD, SSM_HEADS), f32, minval=1.0, maxval=16.0)),
        "ssm_d": 1.0 + nrm(ks[17], (N_ODD, SSM_HEADS), 0.02),
        "ssm_norm": 1.0 + nrm(ks[18], (N_ODD, SSM_D_INNER), 0.02),
        "ssm_w_out": nrm(ks[19], (N_ODD, SSM_D_INNER, D), SSM_D_INNER ** -0.5),
        "final_norm": 1.0 + nrm(ks[20], (D,), 0.02),
    }


def reference(x, c, ada_w, ada_b, norm_mix, norm_ffn, ffn_w1, ffn_w2, even_w_in, pool_w, pool_scale,
              even_w_out, ssm_w_in, ssm_conv_w, ssm_conv_b, ssm_dt_bias, ssm_a_log, ssm_d, ssm_norm,
              ssm_w_out, final_norm):
    cond = jax.nn.silu(c)
    for i in range(DEPTH):
        mod = cond @ ada_w[i] + ada_b[i]
        sh1, sc1, g1, sh2, sc2, g2 = jnp.split(mod, 6, axis=-1)
        h = modulate(rmsnorm(x, norm_mix[i]), sh1, sc1)
        j = i // 2
        if i % 2 == 0:
            y = attn_pool_mixer(h, even_w_in[j], pool_w[j], pool_scale[j], even_w_out[j])
        else:
            y = ssd_mixer(h, ssm_w_in[j], ssm_conv_w[j], ssm_conv_b[j], ssm_dt_bias[j], ssm_a_log[j],
                          ssm_d[j], ssm_norm[j], ssm_w_out[j])
        x = x + g1[:, None, :] * y
        h = modulate(rmsnorm(x, norm_ffn[i]), sh2, sc2)
        x = x + g2[:, None, :] * squared_relu_mlp(h, ffn_w1[i], ffn_w2[i])
    return rmsnorm(x, final_norm)
```

```python
import functools

import jax
import jax.numpy as jnp
from jax import lax
from jax.experimental import pallas as pl
from jax.experimental.pallas import tpu as pltpu

F32 = jnp.float32
BF16 = jnp.bfloat16

NORM_EPS = 1e-6
NEG_BIG = -1e30

ATTN_GROUPS = ((128, 1), (512, 4), (2048, 16))
ATTN_HEADS = 8
ATTN_HEAD_DIM = 64
ATTN_BLOCK = 128
ATTN_GW = ATTN_HEADS * ATTN_HEAD_DIM
N_GROUPS = len(ATTN_GROUPS)
QKV_W = 3 * N_GROUPS * ATTN_GW
POOL_WINDOWS = (2, 4, 8, 16)
POOL_GW = 128
POOL_W = len(POOL_WINDOWS) * POOL_GW
POOL_HALO = 16

SSM_HEAD_DIM = 64
SSM_GROUPS = 8
SSM_STATE = 128
SSM_CONV = 4
SSM_CHUNK = 128
CONV_HALO = 8

VMEM_LIMIT = 56 * 1024 * 1024
LANES = 128


def _cparams(n_axes):
    return pltpu.CompilerParams(dimension_semantics=("arbitrary",) * n_axes, vmem_limit_bytes=VMEM_LIMIT)


def _ada_kernel(c_ref, w_ref, b_ref, o_ref):
    c = c_ref[...]
    cond = (c * jax.nn.sigmoid(c)).astype(BF16)
    w = w_ref[...].astype(BF16)
    o_ref[...] = jnp.dot(cond, w, preferred_element_type=F32) + b_ref[...]


def _ada_mod(c_pad, ada_w, ada_b):
    depth, d, n = ada_w.shape
    rows = c_pad.shape[0]
    tn = 1024
    return pl.pallas_call(
        _ada_kernel,
        grid=(depth, n // tn),
        in_specs=[
            pl.BlockSpec((rows, d), lambda l, j: (0, 0)),
            pl.BlockSpec((None, d, tn), lambda l, j: (l, 0, j)),
            pl.BlockSpec((None, 1, tn), lambda l, j: (l, 0, j)),
        ],
        out_specs=pl.BlockSpec((None, rows, tn), lambda l, j: (l, 0, j)),
        out_shape=jax.ShapeDtypeStruct((depth, rows, n), F32),
        compiler_params=_cparams(2),
        name="ada_mod",
    )(c_pad, ada_w, ada_b.reshape(depth, 1, n))


def _norm_mod_to_scratch(x_ref, g_ref, sh_ref, sc_ref, h_scr, rows_per_step=256):
    tm = x_ref.shape[0]
    gain = g_ref[...] * (1.0 + sc_ref[...])
    shift = sh_ref[...]

    def body(i, carry):
        r0 = pl.multiple_of(i * rows_per_step, rows_per_step)
        x = x_ref[pl.ds(r0, rows_per_step), :]
        ms = jnp.mean(x * x, axis=-1, keepdims=True)
        h = x * lax.rsqrt(ms + NORM_EPS) * gain + shift
        h_scr[pl.ds(r0, rows_per_step), :] = h.astype(BF16)
        return carry

    lax.fori_loop(0, tm // rows_per_step, body, 0)


def _mod_spec(d, idx_base, tiles_per_batch):
    return pl.BlockSpec((None, 1, d), lambda i, *_: (idx_base + i // tiles_per_batch, 0, 0))


def _even_in_kernel(x_ref, g_ref, sh_ref, sc_ref, w_ref, qkv_ref, u_ref, h_scr, *, n_qkv_tiles):
    j = pl.program_id(1)

    @pl.when(j == 0)
    def _():
        _norm_mod_to_scratch(x_ref, g_ref, sh_ref, sc_ref, h_scr)

    r = jnp.dot(h_scr[...], w_ref[...], preferred_element_type=F32)

    @pl.when(j < n_qkv_tiles)
    def _():
        qkv_ref[...] = r.astype(BF16)

    @pl.when(j >= n_qkv_tiles)
    def _():
        u_ref[...] = r


def _even_in_proj(x, g, modr, mod_base, w_bf, seq):
    t, d = x.shape
    n = w_bf.shape[1]
    tm, tn = 1024, 512
    tpb = seq // tm
    n_qkv_tiles = QKV_W // tn
    n_tiles = n // tn
    batch = t // seq
    return pl.pallas_call(
        functools.partial(_even_in_kernel, n_qkv_tiles=n_qkv_tiles),
        grid=(t // tm, n_tiles),
        in_specs=[
            pl.BlockSpec((tm, d), lambda i, j: (i, 0)),
            pl.BlockSpec((1, d), lambda i, j: (0, 0)),
            _mod_spec(d, mod_base + 0 * batch, tpb),
            _mod_spec(d, mod_base + 1 * batch, tpb),
            pl.BlockSpec((d, tn), lambda i, j: (0, j)),
        ],
        out_specs=[
            pl.BlockSpec((tm, tn), lambda i, j: (i, jnp.minimum(j, n_qkv_tiles - 1))),
            pl.BlockSpec((tm, tn), lambda i, j: (i, jnp.maximum(j - n_qkv_tiles, 0))),
        ],
        out_shape=[
            jax.ShapeDtypeStruct((t, QKV_W), BF16),
            jax.ShapeDtypeStruct((t, n - QKV_W), F32),
        ],
        scratch_shapes=[pltpu.VMEM((tm, d), BF16)],
        compiler_params=_cparams(2),
        name="even_in_proj",
    )(x, g, modr, modr, w_bf)


def _attn_kernel(*refs, first, last):
    if first:
        q_ref, k_ref, v_ref, o_ref, l_ref, kprev, vprev = refs
        op_ref = lp_ref = None
    elif last:
        q_ref, k_ref, v_ref, op_ref, lp_ref, o_ref, kprev, vprev = refs
        l_ref = None
    else:
        q_ref, k_ref, v_ref, op_ref, lp_ref, o_ref, l_ref, kprev, vprev = refs
    n = pl.program_id(2)

    @pl.when(n == 0)
    def _():
        kprev[...] = jnp.zeros_like(kprev)
        vprev[...] = jnp.zeros_like(vprev)

    blk = ATTN_BLOCK
    qi = lax.broadcasted_iota(jnp.int32, (blk, blk), 0)
    kj = lax.broadcasted_iota(jnp.int32, (blk, blk), 1)
    prev_bias = jnp.where(kj >= qi, 0.0, NEG_BIG) + jnp.where(n > 0, 0.0, NEG_BIG)
    cur_bias = jnp.where(kj <= qi, 0.0, NEG_BIG)
    lane = lax.broadcasted_iota(jnp.int32, (blk, LANES), 1)
    lo_half = lane < ATTN_HEAD_DIM
    contract_last = (((1,), (1,)), ((), ()))

    for hp in range(ATTN_GW // LANES):
        sl = slice(hp * LANES, (hp + 1) * LANES)
        qp = q_ref[:, sl] * jnp.asarray(ATTN_HEAD_DIM ** -0.5, BF16)
        kc, vc = k_ref[:, sl], v_ref[:, sl]
        kp, vp = kprev[:, sl], vprev[:, sl]
        outs, lses = [], []
        for half in range(2):
            keep = lo_half if half == 0 else jnp.logical_not(lo_half)
            qm = jnp.where(keep, qp, jnp.zeros_like(qp))
            s_prev = lax.dot_general(qm, kp, contract_last, preferred_element_type=F32) + prev_bias
            s_cur = lax.dot_general(qm, kc, contract_last, preferred_element_type=F32) + cur_bias
            m = jnp.maximum(jnp.max(s_prev, axis=-1, keepdims=True), jnp.max(s_cur, axis=-1, keepdims=True))
            p_prev = jnp.exp(s_prev - m)
            p_cur = jnp.exp(s_cur - m)
            den = jnp.sum(p_prev, axis=-1, keepdims=True) + jnp.sum(p_cur, axis=-1, keepdims=True)
            acc = jnp.dot(p_prev.astype(BF16), vp, preferred_element_type=F32)
            acc = acc + jnp.dot(p_cur.astype(BF16), vc, preferred_element_type=F32)
            outs.append(acc / den)
            lses.append(jnp.broadcast_to(m + jnp.log(den), (blk, LANES)))
        o_new = jnp.where(lo_half, outs[0], outs[1])
        lse_new = jnp.where(lo_half, lses[0], lses[1])
        if not first:
            lse_old = lp_ref[:, sl]
            o_old = op_ref[:, sl]
            mx = jnp.maximum(lse_old, lse_new)
            w_old = jnp.exp(lse_old - mx)
            w_new = jnp.exp(lse_new - mx)
            tot = w_old + w_new
            o_new = (o_old * w_old + o_new * w_new) / tot
            lse_new = mx + jnp.log(tot)
        o_ref[:, sl] = o_new.astype(o_ref.dtype)
        if not last:
            l_ref[:, sl] = lse_new

    kprev[...] = k_ref[...]
    vprev[...] = v_ref[...]


def _attn_group(qkv, prev, gi, batch, seq):
    dil = ATTN_GROUPS[gi][1]
    first, last = gi == 0, gi == N_GROUPS - 1
    sub_len = seq // dil
    nb = sub_len // ATTN_BLOCK
    cols_per_pos = QKV_W // ATTN_GW
    qkv_v = qkv.reshape(batch, sub_len, dil * QKV_W)
    blk = (None, ATTN_BLOCK, ATTN_GW)
    in_specs = [
        pl.BlockSpec(blk, lambda b, r, n: (b, n, r * cols_per_pos + gi)),
        pl.BlockSpec(blk, lambda b, r, n: (b, n, r * cols_per_pos + N_GROUPS + gi)),
        pl.BlockSpec(blk, lambda b, r, n: (b, n, r * cols_per_pos + 2 * N_GROUPS + gi)),
    ]
    args = [qkv_v, qkv_v, qkv_v]
    o_spec = pl.BlockSpec(blk, lambda b, r, n: (b, n, r))
    if not first:
        in_specs += [o_spec, o_spec]
        args += [prev[0].reshape(batch, sub_len, dil * ATTN_GW), prev[1].reshape(batch, sub_len, dil * ATTN_GW)]
    o_shape = jax.ShapeDtypeStruct((batch, sub_len, dil * ATTN_GW), BF16 if last else F32)
    l_shape = jax.ShapeDtypeStruct((batch, sub_len, dil * ATTN_GW), F32)
    outs = pl.pallas_call(
        functools.partial(_attn_kernel, first=first, last=last),
        grid=(batch, dil, nb),
        in_specs=in_specs,
        out_specs=[o_spec] if last else [o_spec, o_spec],
        out_shape=[o_shape] if last else [o_shape, l_shape],
        scratch_shapes=[pltpu.VMEM((ATTN_BLOCK, ATTN_GW), BF16), pltpu.VMEM((ATTN_BLOCK, ATTN_GW), BF16)],
        compiler_params=_cparams(3),
        name=f"dilated_attn_g{gi}",
    )(*args)
    outs = [o.reshape(batch * seq, ATTN_GW) for o in outs]
    return outs[0] if last else tuple(outs)


def _even_out_kernel(attn_ref, u_ref, halo_ref, pw_ref, ps_ref, w_ref, x_ref, gate_ref, o_ref, ue, a_scr, *, seq):
    tm = u_ref.shape[0]
    i = pl.program_id(0)
    row0 = (i * tm) % seq
    halo_ok = jnp.where(row0 > 0, 1.0, 0.0)
    ue[0:POOL_HALO, :] = halo_ref[...] * halo_ok
    ue[POOL_HALO:POOL_HALO + tm, :] = u_ref[...]
    a_scr[:, 0:ATTN_GW] = attn_ref[...]
    pos = row0 + lax.broadcasted_iota(jnp.int32, (tm, POOL_GW), 0)
    for gi, win in enumerate(POOL_WINDOWS):
        cols = slice(gi * POOL_GW, (gi + 1) * POOL_GW)
        tok = ue[POOL_HALO:POOL_HALO + tm, cols]
        acc = tok
        for back in range(1, win):
            acc = acc + ue[POOL_HALO - back:POOL_HALO - back + tm, cols]
        cnt = jnp.minimum(pos + 1, win).astype(F32)
        diff = acc / cnt - tok
        y = jnp.dot(diff.astype(BF16), pw_ref[gi], preferred_element_type=F32) * ps_ref[:, cols]
        a_scr[:, ATTN_GW + gi * POOL_GW:ATTN_GW + (gi + 1) * POOL_GW] = y.astype(BF16)
    y = jnp.dot(a_scr[...], w_ref[...], preferred_element_type=F32)
    o_ref[...] = x_ref[...] + gate_ref[...] * y


def _even_out_proj(attn, u, pool_w_bf, pool_scale, w_bf, x, modr, gate_base, seq):
    t, d = x.shape
    tm = 512
    tpb = seq // tm
    k = w_bf.shape[0]
    halo_blocks = tm // POOL_HALO
    return pl.pallas_call(
        functools.partial(_even_out_kernel, seq=seq),
        grid=(t // tm,),
        in_specs=[
            pl.BlockSpec((tm, ATTN_GW), lambda i: (i, 0)),
            pl.BlockSpec((tm, POOL_W), lambda i: (i, 0)),
            pl.BlockSpec((POOL_HALO, POOL_W), lambda i: (jnp.maximum(i * halo_blocks - 1, 0), 0)),
            pl.BlockSpec(pool_w_bf.shape, lambda i: (0, 0, 0)),
            pl.BlockSpec((1, POOL_W), lambda i: (0, 0)),
            pl.BlockSpec((k, d), lambda i: (0, 0)),
            pl.BlockSpec((tm, d), lambda i: (i, 0)),
            _mod_spec(d, gate_base, tpb),
        ],
        out_specs=pl.BlockSpec((tm, d), lambda i: (i, 0)),
        out_shape=jax.ShapeDtypeStruct((t, d), F32),
        scratch_shapes=[pltpu.VMEM((tm + POOL_HALO, POOL_W), F32), pltpu.VMEM((tm, k), BF16)],
        compiler_params=_cparams(1),
        name="even_out_proj",
    )(attn, u, u, pool_w_bf, pool_scale, w_bf, x, modr)


def _ffn_kernel(x_ref, g_ref, sh_ref, sc_ref, gate_ref, w1_ref, w2_ref, fin_ref, o_ref, h_scr, *, final_norm):
    j = pl.program_id(1)

    @pl.when(j == 0)
    def _():
        _norm_mod_to_scratch(x_ref, g_ref, sh_ref, sc_ref, h_scr)

    a = jnp.dot(h_scr[...], w1_ref[...], preferred_element_type=F32)
    a = jnp.maximum(a, 0.0)
    part = jnp.dot((a * a).astype(BF16), w2_ref[...], preferred_element_type=F32)

    @pl.when(j == 0)
    def _():
        o_ref[...] = part

    @pl.when(j > 0)
    def _():
        o_ref[...] += part

    @pl.when(j == pl.num_programs(1) - 1)
    def _():
        rows = 256
        gate = gate_ref[...]
        fin = fin_ref[...]

        def body(i, carry):
            r0 = pl.multiple_of(i * rows, rows)
            y = x_ref[pl.ds(r0, rows), :] + gate * o_ref[pl.ds(r0, rows), :]
            if final_norm:
                ms = jnp.mean(y * y, axis=-1, keepdims=True)
                y = y * lax.rsqrt(ms + NORM_EPS) * fin
            o_ref[pl.ds(r0, rows), :] = y
            return carry

        lax.fori_loop(0, x_ref.shape[0] // rows, body, 0)


def _ffn(x, g, modr, mod_base, w1_bf, w2_bf, fin, seq, final_norm):
    t, d = x.shape
    hdim = w1_bf.shape[1]
    tm, th = 1024, 512
    tpb = seq // tm
    batch = t // seq
    return pl.pallas_call(
        functools.partial(_ffn_kernel, final_norm=final_norm),
        grid=(t // tm, hdim // th),
        in_specs=[
            pl.BlockSpec((tm, d), lambda i, j: (i, 0), pipeline_mode=pl.Buffered(1)),
            pl.BlockSpec((1, d), lambda i, j: (0, 0)),
            _mod_spec(d, mod_base + 3 * batch, tpb),
            _mod_spec(d, mod_base + 4 * batch, tpb),
            _mod_spec(d, mod_base + 5 * batch, tpb),
            pl.BlockSpec((d, th), lambda i, j: (0, j)),
            pl.BlockSpec((th, d), lambda i, j: (j, 0)),
            pl.BlockSpec((1, d), lambda i, j: (0, 0)),
        ],
        out_specs=pl.BlockSpec((tm, d), lambda i, j: (i, 0)),
        out_shape=jax.ShapeDtypeStruct((t, d), F32),
        scratch_shapes=[pltpu.VMEM((tm, d), BF16)],
        compiler_params=_cparams(2),
        name="ffn_final" if final_norm else "ffn",
    )(x, g, modr, modr, modr, w1_bf, w2_bf, fin)


def _ssm_in_kernel(x_ref, g_ref, sh_ref, sc_ref, w_ref, wdt_ref, z_ref, xbc_ref, dt_ref, h_scr, *, n_z_tiles):
    j = pl.program_id(1)

    @pl.when(j == 0)
    def _():
        _norm_mod_to_scratch(x_ref, g_ref, sh_ref, sc_ref, h_scr)
        dt_ref[...] = jnp.dot(h_scr[...], wdt_ref[...], preferred_element_type=F32)

    r = jnp.dot(h_scr[...], w_ref[...], preferred_element_type=F32).astype(BF16)

    @pl.when(j < n_z_tiles)
    def _():
        z_ref[...] = r

    @pl.when(j >= n_z_tiles)
    def _():
        xbc_ref[...] = r


def _ssm_in_proj(x, g, modr, mod_base, w_bf, wdt_bf, d_inner, seq):
    t, d = x.shape
    n_main = w_bf.shape[1]
    tm, tn = 1024, 512
    tpb = seq // tm
    batch = t // seq
    n_z_tiles = d_inner // tn
    return pl.pallas_call(
        functools.partial(_ssm_in_kernel, n_z_tiles=n_z_tiles),
        grid=(t // tm, n_main // tn),
        in_specs=[
            pl.BlockSpec((tm, d), lambda i, j: (i, 0)),
            pl.BlockSpec((1, d), lambda i, j: (0, 0)),
            _mod_spec(d, mod_base + 0 * batch, tpb),
            _mod_spec(d, mod_base + 1 * batch, tpb),
            pl.BlockSpec((d, tn), lambda i, j: (0, j)),
            pl.BlockSpec((d, LANES), lambda i, j: (0, 0)),
        ],
        out_specs=[
            pl.BlockSpec((tm, tn), lambda i, j: (i, jnp.minimum(j, n_z_tiles - 1))),
            pl.BlockSpec((tm, tn), lambda i, j: (i, jnp.maximum(j - n_z_tiles, 0))),
            pl.BlockSpec((tm, LANES), lambda i, j: (i, 0)),
        ],
        out_shape=[
            jax.ShapeDtypeStruct((t, d_inner), BF16),
            jax.ShapeDtypeStruct((t, n_main - d_inner), BF16),
            jax.ShapeDtypeStruct((t, LANES), F32),
        ],
        scratch_shapes=[pltpu.VMEM((tm, d), BF16)],
        compiler_params=_cparams(2),
        name="ssm_in_proj",
    )(x, g, modr, modr, w_bf, wdt_bf)


def _split3_bf16(v):
    hi = v.astype(BF16)
    r1 = v - hi.astype(F32)
    mid = r1.astype(BF16)
    lo = (r1 - mid.astype(F32)).astype(BF16)
    return hi, mid, lo


def _ssd_kernel(z_ref, xbc_ref, dt_ref, cw_ref, cb_ref, dtb_ref, alog_ref, dskip_ref, ng_ref, e_ref, y_ref,
                state, xext, xs_f, xs_b, bm, cm, ex, y_scr, *, d_inner):
    q = SSM_CHUNK
    n_heads_pad = LANES
    gw = d_inner // SSM_GROUPS
    gs = SSM_STATE

    @pl.when(pl.program_id(1) == 0)
    def _():
        state[...] = jnp.zeros_like(state)
        xext[0:CONV_HALO, :] = jnp.zeros((CONV_HALO, xext.shape[1]), F32)

    xext[CONV_HALO:CONV_HALO + q, :] = xbc_ref[...].astype(F32)
    n_ch = xext.shape[1]
    for c0 in range(0, n_ch, gw):
        cols = slice(c0, c0 + gw)
        acc = cb_ref[:, cols] + cw_ref[0:1, cols] * xext[CONV_HALO - 3:CONV_HALO - 3 + q, cols]
        for k in range(1, SSM_CONV):
            off = CONV_HALO - 3 + k
            acc = acc + cw_ref[k:k + 1, cols] * xext[off:off + q, cols]
        act = acc * jax.nn.sigmoid(acc)
        if c0 < d_inner:
            xs_f[:, cols] = act
            xs_b[:, cols] = act.astype(BF16)
        elif c0 < d_inner + SSM_GROUPS * gs:
            bm[:, c0 - d_inner:c0 - d_inner + gw] = act.astype(BF16)
        else:
            c1 = c0 - d_inner - SSM_GROUPS * gs
            cm[:, c1:c1 + gw] = act.astype(BF16)
    xext[0:CONV_HALO, :] = xext[q:q + CONV_HALO, :]

    v = dt_ref[...] + dtb_ref[...]
    dt = jnp.maximum(v, 0.0) + jnp.log(1.0 + jnp.exp(-jnp.abs(v)))
    a_neg = -jnp.exp(alog_ref[...])
    d_a = dt * a_neg
    row = lax.broadcasted_iota(jnp.int32, (q, q), 0)
    col = lax.broadcasted_iota(jnp.int32, (q, q), 1)
    causal = row >= col
    tril = jnp.where(causal, 1.0, 0.0).astype(BF16)
    a_cum = None
    for part in _split3_bf16(d_a):
        term = jnp.dot(tril, part, preferred_element_type=F32)
        a_cum = term if a_cum is None else a_cum + term
    a_t = a_cum.T
    dt_t = dt.T
    a_end = a_cum[q - 1:q, :]
    ea = jnp.exp(a_cum)
    wgt = dt * jnp.exp(a_end - a_cum)
    ea_hi = ea.astype(BF16)
    ea_lo = (ea - ea_hi.astype(F32)).astype(BF16)
    top = jnp.concatenate([ea_hi, ea_lo], axis=1)
    bot = jnp.concatenate([wgt.astype(BF16), jnp.zeros((q, n_heads_pad), BF16)], axis=1)
    ex[0:q, :] = jnp.dot(top, e_ref[...], preferred_element_type=F32)
    ex[q:2 * q, :] = jnp.dot(bot, e_ref[...], preferred_element_type=F32)

    lane = lax.broadcasted_iota(jnp.int32, (q, LANES), 1)
    lo_half = lane < SSM_HEAD_DIM
    contract_last = (((1,), (1,)), ((), ()))
    contract_first = (((0,), (0,)), ((), ()))
    heads_per_group = gw // SSM_HEAD_DIM

    for g in range(SSM_GROUPS):
        gcols = slice(g * gw, (g + 1) * gw)
        b_g = bm[:, g * gs:(g + 1) * gs]
        c_g = cm[:, g * gs:(g + 1) * gs]
        cb = lax.dot_general(c_g, b_g, contract_last, preferred_element_type=F32)
        s_in = state[g]
        y_off = jnp.dot(c_g, s_in.astype(BF16), preferred_element_type=F32)
        for pr in range(heads_per_group // 2):
            pcols = slice(g * gw + pr * LANES, g * gw + (pr + 1) * LANES)
            x_pair = xs_b[:, pcols]
            halves = []
            for half in range(2):
                h = g * heads_per_group + 2 * pr + half
                diff = a_cum[:, h:h + 1] - a_t[h:h + 1, :]
                dec = jnp.exp(jnp.where(causal, diff, NEG_BIG))
                mat = (cb * dec * dt_t[h:h + 1, :]).astype(BF16)
                halves.append(jnp.dot(mat, x_pair, preferred_element_type=F32))
            y_diag = jnp.where(lo_half, halves[0], halves[1])
            y_scr[:, pcols] = (y_diag + y_off[:, pr * LANES:(pr + 1) * LANES] * ex[0:q, pcols]
                               + dskip_ref[:, pcols] * xs_f[:, pcols])
        xw = (xs_f[:, gcols] * ex[q:2 * q, gcols]).astype(BF16)
        upd = lax.dot_general(b_g, xw, contract_first, preferred_element_type=F32)
        state[g] = s_in * ex[q - 1:q, gcols] + upd

        zg = z_ref[:, gcols].astype(F32)
        yz = y_scr[:, gcols] * (zg * jax.nn.sigmoid(zg))
        ms = jnp.mean(yz * yz, axis=-1, keepdims=True)
        y_ref[:, gcols] = (yz * lax.rsqrt(ms + NORM_EPS) * ng_ref[:, gcols]).astype(y_ref.dtype)


def _ssd(z, xbc, dt_raw, conv_w, conv_b, dt_bias_pad, a_log_pad, d_exp, norm_g, expand, batch, seq):
    t, d_inner = z.shape
    n_conv = xbc.shape[1]
    q = SSM_CHUNK
    nc = seq // q
    gw = d_inner // SSM_GROUPS
    row_map = lambda b, c: (b * nc + c, 0)
    const = lambda b, c: (0, 0)
    return pl.pallas_call(
        functools.partial(_ssd_kernel, d_inner=d_inner),
        grid=(batch, nc),
        in_specs=[
            pl.BlockSpec((q, d_inner), row_map),
            pl.BlockSpec((q, n_conv), row_map),
            pl.BlockSpec((q, LANES), row_map),
            pl.BlockSpec((SSM_CONV, n_conv), const),
            pl.BlockSpec((1, n_conv), const),
            pl.BlockSpec((1, LANES), const),
            pl.BlockSpec((1, LANES), const),
            pl.BlockSpec((1, d_inner), const),
            pl.BlockSpec((1, d_inner), const),
            pl.BlockSpec((2 * LANES, d_inner), const),
        ],
        out_specs=pl.BlockSpec((q, d_inner), row_map),
        out_shape=jax.ShapeDtypeStruct((t, d_inner), BF16),
        scratch_shapes=[
            pltpu.VMEM((SSM_GROUPS, SSM_STATE, gw), F32),
            pltpu.VMEM((q + CONV_HALO, n_conv), F32),
            pltpu.VMEM((q, d_inner), F32),
            pltpu.VMEM((q, d_inner), BF16),
            pltpu.VMEM((q, SSM_GROUPS * SSM_STATE), BF16),
            pltpu.VMEM((q, SSM_GROUPS * SSM_STATE), BF16),
            pltpu.VMEM((2 * q, d_inner), F32),
            pltpu.VMEM((q, d_inner), F32),
        ],
        compiler_params=_cparams(2),
        name="ssd_scan",
    )(z, xbc, dt_raw, conv_w, conv_b, dt_bias_pad, a_log_pad, d_exp, norm_g, expand)


def _mm_res_kernel(a_ref, w_ref, x_ref, gate_ref, o_ref):
    y = jnp.dot(a_ref[...], w_ref[...], preferred_element_type=F32)
    o_ref[...] = x_ref[...] + gate_ref[...] * y


def _matmul_residual(a, w_bf, x, modr, gate_base, seq):
    t, k = a.shape
    d = x.shape[1]
    tm, tn = 1024, 512
    tpb = seq // tm
    gate_spec = pl.BlockSpec((None, 1, tn), lambda i, j: (gate_base + i // tpb, 0, j))
    return pl.pallas_call(
        _mm_res_kernel,
        grid=(t // tm, d // tn),
        in_specs=[
            pl.BlockSpec((tm, k), lambda i, j: (i, 0)),
            pl.BlockSpec((k, tn), lambda i, j: (0, j)),
            pl.BlockSpec((tm, tn), lambda i, j: (i, j)),
            gate_spec,
        ],
        out_specs=pl.BlockSpec((tm, tn), lambda i, j: (i, j)),
        out_shape=jax.ShapeDtypeStruct((t, d), F32),
        compiler_params=_cparams(2),
        name="ssm_out_proj",
    )(a, w_bf, x, modr)


def kernel(x, c, ada_w, ada_b, norm_mix, norm_ffn, ffn_w1, ffn_w2, even_w_in, pool_w, pool_scale, even_w_out,
           ssm_w_in, ssm_conv_w, ssm_conv_b, ssm_dt_bias, ssm_a_log, ssm_d, ssm_norm, ssm_w_out, final_norm):
    batch, seq, d = x.shape
    depth = ada_w.shape[0]
    t = batch * seq
    xf = x.reshape(t, d)

    c_pad = jnp.pad(c, ((0, 8 - batch), (0, 0)))
    mod = _ada_mod(c_pad, ada_w, ada_b)[:, :batch]
    modr = mod.reshape(depth, batch, 6, d).transpose(0, 2, 1, 3).reshape(depth * 6 * batch, 1, d)

    d_inner = ssm_w_out.shape[1]
    n_heads = ssm_dt_bias.shape[1]
    head_of_channel = jnp.arange(d_inner) // SSM_HEAD_DIM
    expand1 = (jnp.arange(LANES)[:, None] == head_of_channel[None, :]).astype(BF16)
    expand = jnp.concatenate([expand1, expand1], axis=0)

    for i in range(depth):
        base = i * 6 * batch
        j = i // 2
        g_mix = norm_mix[i].reshape(1, d)
        g_ffn = norm_ffn[i].reshape(1, d)
        if i % 2 == 0:
            qkv, u = _even_in_proj(xf, g_mix, modr, base, even_w_in[j].astype(BF16), seq)
            merged = None
            for gi in range(N_GROUPS):
                merged = _attn_group(qkv, merged, gi, batch, seq)
            xf = _even_out_proj(merged, u, pool_w[j].astype(BF16), pool_scale[j].reshape(1, POOL_W),
                                even_w_out[j].astype(BF16), xf, modr, base + 2 * batch, seq)
        else:
            w_in = ssm_w_in[j]
            n_main = w_in.shape[1] - n_heads
            wdt = jnp.pad(w_in[:, n_main:], ((0, 0), (0, LANES - n_heads))).astype(BF16)
            z, xbc, dt_raw = _ssm_in_proj(xf, g_mix, modr, base, w_in[:, :n_main].astype(BF16), wdt, d_inner, seq)
            pad_h = ((0, 0), (0, LANES - n_heads))
            y = _ssd(z, xbc, dt_raw, ssm_conv_w[j], ssm_conv_b[j].reshape(1, -1),
                     jnp.pad(ssm_dt_bias[j].reshape(1, -1), pad_h), jnp.pad(ssm_a_log[j].reshape(1, -1), pad_h),
                     jnp.repeat(ssm_d[j], SSM_HEAD_DIM).reshape(1, d_inner), ssm_norm[j].reshape(1, d_inner),
                     expand, batch, seq)
            xf = _matmul_residual(y, ssm_w_out[j].astype(BF16), xf, modr, base + 2 * batch, seq)
        xf = _ffn(xf, g_ffn, modr, base, ffn_w1[i].astype(BF16), ffn_w2[i].astype(BF16),
                  final_norm.reshape(1, d), seq, final_norm=(i == depth - 1))
    return xf.reshape(batch, seq, d)
```

```python
import functools

import jax
import jax.numpy as jnp
from jax import lax
from jax.experimental import pallas as pl
from jax.experimental.pallas import tpu as pltpu

F32 = jnp.float32
BF16 = jnp.bfloat16

NORM_EPS = 1e-6
NEG_BIG = -1e30

ATTN_GROUPS = ((128, 1), (512, 4), (2048, 16))
ATTN_HEADS = 8
ATTN_HEAD_DIM = 64
ATTN_BLOCK = 128
ATTN_GW = ATTN_HEADS * ATTN_HEAD_DIM
N_GROUPS = len(ATTN_GROUPS)
QKV_W = 3 * N_GROUPS * ATTN_GW
POOL_WINDOWS = (2, 4, 8, 16)
POOL_GW = 128
POOL_W = len(POOL_WINDOWS) * POOL_GW
POOL_HALO = 16

SSM_HEAD_DIM = 64
SSM_GROUPS = 8
SSM_STATE = 128
SSM_CONV = 4
SSM_CHUNK = 128
CONV_HALO = 8

VMEM_LIMIT = 56 * 1024 * 1024
LANES = 128


def _cparams(n_axes):
    return pltpu.CompilerParams(dimension_semantics=("arbitrary",) * n_axes, vmem_limit_bytes=VMEM_LIMIT)


def _ada_kernel(c_ref, w_ref, b_ref, o_ref):
    c = c_ref[...]
    cond = (c * jax.nn.sigmoid(c)).astype(BF16)
    w = w_ref[...].astype(BF16)
    o_ref[...] = jnp.dot(cond, w, preferred_element_type=F32) + b_ref[...]


def _ada_mod(c_pad, ada_w, ada_b):
    depth, d, n = ada_w.shape
    rows = c_pad.shape[0]
    tn = 1024
    return pl.pallas_call(
        _ada_kernel,
        grid=(depth, n // tn),
        in_specs=[
            pl.BlockSpec((rows, d), lambda l, j: (0, 0)),
            pl.BlockSpec((None, d, tn), lambda l, j: (l, 0, j)),
            pl.BlockSpec((None, 1, tn), lambda l, j: (l, 0, j)),
        ],
        out_specs=pl.BlockSpec((None, rows, tn), lambda l, j: (l, 0, j)),
        out_shape=jax.ShapeDtypeStruct((depth, rows, n), F32),
        compiler_params=_cparams(2),
        name="ada_mod",
    )(c_pad, ada_w, ada_b.reshape(depth, 1, n))


def _norm_mod_to_scratch(x_ref, g_ref, sh_ref, sc_ref, h_scr, rows_per_step=256):
    tm = x_ref.shape[0]
    gain = g_ref[...] * (1.0 + sc_ref[...])
    shift = sh_ref[...]

    def body(i, carry):
        r0 = pl.multiple_of(i * rows_per_step, rows_per_step)
        x = x_ref[pl.ds(r0, rows_per_step), :]
        ms = jnp.mean(x * x, axis=-1, keepdims=True)
        h = x * lax.rsqrt(ms + NORM_EPS) * gain + shift
        h_scr[pl.ds(r0, rows_per_step), :] = h.astype(BF16)
        return carry

    lax.fori_loop(0, tm // rows_per_step, body, 0)


def _mod_spec(d, idx_base, tiles_per_batch):
    return pl.BlockSpec((None, 1, d), lambda i, *_: (idx_base + i // tiles_per_batch, 0, 0))


def _even_in_kernel(x_ref, g_ref, sh_ref, sc_ref, w_ref, wu_ref, qkv_ref, u_ref, h_scr):
    @pl.when(pl.program_id(1) == 0)
    def _():
        _norm_mod_to_scratch(x_ref, g_ref, sh_ref, sc_ref, h_scr)
        u_ref[...] = jnp.dot(h_scr[...], wu_ref[...], preferred_element_type=F32)

    qkv_ref[...] = jnp.dot(h_scr[...], w_ref[...], preferred_element_type=F32).astype(BF16)


def _even_in_proj(x, g, modr, mod_base, w_bf, seq):
    t, d = x.shape
    n = w_bf.shape[1]
    tm, tn = 1024, 768
    tpb = seq // tm
    batch = t // seq
    return pl.pallas_call(
        _even_in_kernel,
        grid=(t // tm, QKV_W // tn),
        in_specs=[
            pl.BlockSpec((tm, d), lambda i, j: (i, 0)),
            pl.BlockSpec((1, d), lambda i, j: (0, 0)),
            _mod_spec(d, mod_base + 0 * batch, tpb),
            _mod_spec(d, mod_base + 1 * batch, tpb),
            pl.BlockSpec((d, tn), lambda i, j: (0, j)),
            pl.BlockSpec((d, n - QKV_W), lambda i, j: (0, QKV_W // (n - QKV_W))),
        ],
        out_specs=[
            pl.BlockSpec((tm, tn), lambda i, j: (i, j)),
            pl.BlockSpec((tm, n - QKV_W), lambda i, j: (i, 0)),
        ],
        out_shape=[
            jax.ShapeDtypeStruct((t, QKV_W), BF16),
            jax.ShapeDtypeStruct((t, n - QKV_W), F32),
        ],
        scratch_shapes=[pltpu.VMEM((tm, d), BF16)],
        compiler_params=_cparams(2),
        name="even_in_proj",
    )(x, g, modr, modr, w_bf, w_bf)


def _attn_kernel(*refs, first, last):
    if first:
        q_ref, k_ref, v_ref, o_ref, l_ref, kprev, vprev = refs
        op_ref = lp_ref = None
    elif last:
        q_ref, k_ref, v_ref, op_ref, lp_ref, o_ref, kprev, vprev = refs
        l_ref = None
    else:
        q_ref, k_ref, v_ref, op_ref, lp_ref, o_ref, l_ref, kprev, vprev = refs
    n = pl.program_id(2)

    @pl.when(n == 0)
    def _():
        kprev[...] = jnp.zeros_like(kprev)
        vprev[...] = jnp.zeros_like(vprev)

    blk = ATTN_BLOCK
    qi = lax.broadcasted_iota(jnp.int32, (blk, blk), 0)
    kj = lax.broadcasted_iota(jnp.int32, (blk, blk), 1)
    prev_bias = jnp.where(kj >= qi, 0.0, NEG_BIG) + jnp.where(n > 0, 0.0, NEG_BIG)
    cur_bias = jnp.where(kj <= qi, 0.0, NEG_BIG)
    lane = lax.broadcasted_iota(jnp.int32, (blk, LANES), 1)
    lo_half = lane < ATTN_HEAD_DIM
    contract_last = (((1,), (1,)), ((), ()))

    for hp in range(ATTN_GW // LANES):
        sl = slice(hp * LANES, (hp + 1) * LANES)
        qp = q_ref[:, sl] * jnp.asarray(ATTN_HEAD_DIM ** -0.5, BF16)
        kc, vc = k_ref[:, sl], v_ref[:, sl]
        kp, vp = kprev[:, sl], vprev[:, sl]
        outs, lses = [], []
        for half in range(2):
            keep = lo_half if half == 0 else jnp.logical_not(lo_half)
            qm = jnp.where(keep, qp, jnp.zeros_like(qp))
            s_prev = lax.dot_general(qm, kp, contract_last, preferred_element_type=F32) + prev_bias
            s_cur = lax.dot_general(qm, kc, contract_last, preferred_element_type=F32) + cur_bias
            m = jnp.maximum(jnp.max(s_prev, axis=-1, keepdims=True), jnp.max(s_cur, axis=-1, keepdims=True))
            p_prev = jnp.exp(s_prev - m)
            p_cur = jnp.exp(s_cur - m)
            den = jnp.sum(p_prev, axis=-1, keepdims=True) + jnp.sum(p_cur, axis=-1, keepdims=True)
            acc = jnp.dot(p_prev.astype(BF16), vp, preferred_element_type=F32)
            acc = acc + jnp.dot(p_cur.astype(BF16), vc, preferred_element_type=F32)
            outs.append(acc / den)
            lses.append(jnp.broadcast_to(m + jnp.log(den), (blk, LANES)))
        o_new = jnp.where(lo_half, outs[0], outs[1])
        lse_new = jnp.where(lo_half, lses[0], lses[1])
        if not first:
            lse_old = lp_ref[:, sl]
            o_old = op_ref[:, sl]
            mx = jnp.maximum(lse_old, lse_new)
            w_old = jnp.exp(lse_old - mx)
            w_new = jnp.exp(lse_new - mx)
            tot = w_old + w_new
            o_new = (o_old * w_old + o_new * w_new) / tot
            lse_new = mx + jnp.log(tot)
        o_ref[:, sl] = o_new.astype(o_ref.dtype)
        if not last:
            l_ref[:, sl] = lse_new

    kprev[...] = k_ref[...]
    vprev[...] = v_ref[...]


def _attn_group(qkv, prev, gi, batch, seq):
    dil = ATTN_GROUPS[gi][1]
    first, last = gi == 0, gi == N_GROUPS - 1
    sub_len = seq // dil
    nb = sub_len // ATTN_BLOCK
    cols_per_pos = QKV_W // ATTN_GW
    qkv_v = qkv.reshape(batch, sub_len, dil * QKV_W)
    blk = (None, ATTN_BLOCK, ATTN_GW)
    in_specs = [
        pl.BlockSpec(blk, lambda b, r, n: (b, n, r * cols_per_pos + gi)),
        pl.BlockSpec(blk, lambda b, r, n: (b, n, r * cols_per_pos + N_GROUPS + gi)),
        pl.BlockSpec(blk, lambda b, r, n: (b, n, r * cols_per_pos + 2 * N_GROUPS + gi)),
    ]
    args = [qkv_v, qkv_v, qkv_v]
    o_spec = pl.BlockSpec(blk, lambda b, r, n: (b, n, r))
    if not first:
        in_specs += [o_spec, o_spec]
        args += [prev[0].reshape(batch, sub_len, dil * ATTN_GW), prev[1].reshape(batch, sub_len, dil * ATTN_GW)]
    o_shape = jax.ShapeDtypeStruct((batch, sub_len, dil * ATTN_GW), BF16 if last else F32)
    l_shape = jax.ShapeDtypeStruct((batch, sub_len, dil * ATTN_GW), F32)
    outs = pl.pallas_call(
        functools.partial(_attn_kernel, first=first, last=last),
        grid=(batch, dil, nb),
        in_specs=in_specs,
        out_specs=[o_spec] if last else [o_spec, o_spec],
        out_shape=[o_shape] if last else [o_shape, l_shape],
        scratch_shapes=[pltpu.VMEM((ATTN_BLOCK, ATTN_GW), BF16), pltpu.VMEM((ATTN_BLOCK, ATTN_GW), BF16)],
        compiler_params=_cparams(3),
        name=f"dilated_attn_g{gi}",
    )(*args)
    outs = [o.reshape(batch * seq, ATTN_GW) for o in outs]
    return outs[0] if last else tuple(outs)


def _even_out_kernel(attn_ref, u_ref, halo_ref, pw_ref, ps_ref, w_ref, x_ref, gate_ref, o_ref, ue, a_scr, *, seq):
    tm = u_ref.shape[0]
    i = pl.program_id(0)
    row0 = (i * tm) % seq
    halo_ok = jnp.where(row0 > 0, 1.0, 0.0)
    ue[0:POOL_HALO, :] = halo_ref[...] * halo_ok
    ue[POOL_HALO:POOL_HALO + tm, :] = u_ref[...]
    a_scr[:, 0:ATTN_GW] = attn_ref[...]
    pos = row0 + lax.broadcasted_iota(jnp.int32, (tm, POOL_GW), 0)
    for gi, win in enumerate(POOL_WINDOWS):
        cols = slice(gi * POOL_GW, (gi + 1) * POOL_GW)
        tok = ue[POOL_HALO:POOL_HALO + tm, cols]
        acc = tok
        for back in range(1, win):
            acc = acc + ue[POOL_HALO - back:POOL_HALO - back + tm, cols]
        cnt = jnp.minimum(pos + 1, win).astype(F32)
        diff = acc / cnt - tok
        y = jnp.dot(diff.astype(BF16), pw_ref[gi], preferred_element_type=F32) * ps_ref[:, cols]
        a_scr[:, ATTN_GW + gi * POOL_GW:ATTN_GW + (gi + 1) * POOL_GW] = y.astype(BF16)
    y = jnp.dot(a_scr[...], w_ref[...], preferred_element_type=F32)
    o_ref[...] = x_ref[...] + gate_ref[...] * y


def _even_out_proj(attn, u, pool_w_bf, pool_scale, w_bf, x, modr, gate_base, seq):
    t, d = x.shape
    tm = 512
    tpb = seq // tm
    k = w_bf.shape[0]
    halo_blocks = tm // POOL_HALO
    return pl.pallas_call(
        functools.partial(_even_out_kernel, seq=seq),
        grid=(t // tm,),
        in_specs=[
            pl.BlockSpec((tm, ATTN_GW), lambda i: (i, 0)),
            pl.BlockSpec((tm, POOL_W), lambda i: (i, 0)),
            pl.BlockSpec((POOL_HALO, POOL_W), lambda i: (jnp.maximum(i * halo_blocks - 1, 0), 0)),
            pl.BlockSpec(pool_w_bf.shape, lambda i: (0, 0, 0)),
            pl.BlockSpec((1, POOL_W), lambda i: (0, 0)),
            pl.BlockSpec((k, d), lambda i: (0, 0)),
            pl.BlockSpec((tm, d), lambda i: (i, 0)),
            _mod_spec(d, gate_base, tpb),
        ],
        out_specs=pl.BlockSpec((tm, d), lambda i: (i, 0)),
        out_shape=jax.ShapeDtypeStruct((t, d), F32),
        scratch_shapes=[pltpu.VMEM((tm + POOL_HALO, POOL_W), F32), pltpu.VMEM((tm, k), BF16)],
        compiler_params=_cparams(1),
        name="even_out_proj",
    )(attn, u, u, pool_w_bf, pool_scale, w_bf, x, modr)


def _ffn_kernel(x_ref, g_ref, sh_ref, sc_ref, gate_ref, w1_ref, w2_ref, fin_ref, o_ref, h_scr, *, final_norm):
    j = pl.program_id(1)

    @pl.when(j == 0)
    def _():
        _norm_mod_to_scratch(x_ref, g_ref, sh_ref, sc_ref, h_scr)
        o_ref[...] = jnp.zeros_like(o_ref)

    a = jnp.dot(h_scr[...], w1_ref[...], preferred_element_type=F32)
    a = jnp.maximum(a, 0.0)
    o_ref[...] += jnp.dot((a * a).astype(BF16), w2_ref[...], preferred_element_type=F32)

    @pl.when(j == pl.num_programs(1) - 1)
    def _():
        rows = 256
        gate = gate_ref[...]
        fin = fin_ref[...]

        def body(i, carry):
            r0 = pl.multiple_of(i * rows, rows)
            y = x_ref[pl.ds(r0, rows), :] + gate * o_ref[pl.ds(r0, rows), :]
            if final_norm:
                ms = jnp.mean(y * y, axis=-1, keepdims=True)
                y = y * lax.rsqrt(ms + NORM_EPS) * fin
            o_ref[pl.ds(r0, rows), :] = y
            return carry

        lax.fori_loop(0, x_ref.shape[0] // rows, body, 0)


def _ffn(x, g, modr, mod_base, w1_bf, w2_bf, fin, seq, final_norm):
    t, d = x.shape
    hdim = w1_bf.shape[1]
    tm, th = 1024, 1024
    tpb = seq // tm
    batch = t // seq
    return pl.pallas_call(
        functools.partial(_ffn_kernel, final_norm=final_norm),
        grid=(t // tm, hdim // th),
        in_specs=[
            pl.BlockSpec((tm, d), lambda i, j: (i, 0), pipeline_mode=pl.Buffered(1)),
            pl.BlockSpec((1, d), lambda i, j: (0, 0)),
            _mod_spec(d, mod_base + 3 * batch, tpb),
            _mod_spec(d, mod_base + 4 * batch, tpb),
            _mod_spec(d, mod_base + 5 * batch, tpb),
            pl.BlockSpec((d, th), lambda i, j: (0, j)),
            pl.BlockSpec((th, d), lambda i, j: (j, 0)),
            pl.BlockSpec((1, d), lambda i, j: (0, 0)),
        ],
        out_specs=pl.BlockSpec((tm, d), lambda i, j: (i, 0)),
        out_shape=jax.ShapeDtypeStruct((t, d), F32),
        scratch_shapes=[pltpu.VMEM((tm, d), BF16)],
        compiler_params=_cparams(2),
        name="ffn_final" if final_norm else "ffn",
    )(x, g, modr, modr, modr, w1_bf, w2_bf, fin)


def _ssm_in_kernel(x_ref, g_ref, sh_ref, sc_ref, w_ref, wdt_ref, zxbc_ref, dt_ref, h_scr):
    @pl.when(pl.program_id(1) == 0)
    def _():
        _norm_mod_to_scratch(x_ref, g_ref, sh_ref, sc_ref, h_scr)
        dt_ref[...] = jnp.dot(h_scr[...], wdt_ref[...], preferred_element_type=F32)

    zxbc_ref[...] = jnp.dot(h_scr[...], w_ref[...], preferred_element_type=F32).astype(BF16)


def _ssm_in_proj(x, g, modr, mod_base, w_bf, wdt_bf, seq):
    t, d = x.shape
    n_main = w_bf.shape[1]
    tm, tn = 1024, 1024
    tpb = seq // tm
    batch = t // seq
    return pl.pallas_call(
        _ssm_in_kernel,
        grid=(t // tm, n_main // tn),
        in_specs=[
            pl.BlockSpec((tm, d), lambda i, j: (i, 0)),
            pl.BlockSpec((1, d), lambda i, j: (0, 0)),
            _mod_spec(d, mod_base + 0 * batch, tpb),
            _mod_spec(d, mod_base + 1 * batch, tpb),
            pl.BlockSpec((d, tn), lambda i, j: (0, j)),
            pl.BlockSpec((d, LANES), lambda i, j: (0, 0)),
        ],
        out_specs=[
            pl.BlockSpec((tm, tn), lambda i, j: (i, j)),
            pl.BlockSpec((tm, LANES), lambda i, j: (i, 0)),
        ],
        out_shape=[
            jax.ShapeDtypeStruct((t, n_main), BF16),
            jax.ShapeDtypeStruct((t, LANES), F32),
        ],
        scratch_shapes=[pltpu.VMEM((tm, d), BF16)],
        compiler_params=_cparams(2),
        name="ssm_in_proj",
    )(x, g, modr, modr, w_bf, wdt_bf)


def _split3_bf16(v):
    hi = v.astype(BF16)
    r1 = v - hi.astype(F32)
    mid = r1.astype(BF16)
    lo = (r1 - mid.astype(F32)).astype(BF16)
    return hi, mid, lo


def _ssd_kernel(z_ref, xa_ref, xb_ref, xc_ref, dt_ref, cw_ref, cb_ref, dtb_ref, alog_ref, dskip_ref, ng_ref, e_ref,
                y_ref, state, xext, xs_f, xs_b, bm, cm, ex, y_scr, *, d_inner):
    q = SSM_CHUNK
    n_heads_pad = LANES
    gw = d_inner // SSM_GROUPS
    gs = SSM_STATE

    @pl.when(pl.program_id(1) == 0)
    def _():
        state[...] = jnp.zeros_like(state)
        xext[0:CONV_HALO, :] = jnp.zeros((CONV_HALO, xext.shape[1]), F32)

    part_w = xa_ref.shape[1]
    for k, part in enumerate((xa_ref, xb_ref, xc_ref)):
        xext[CONV_HALO:CONV_HALO + q, k * part_w:(k + 1) * part_w] = part[...].astype(F32)
    n_ch = xext.shape[1]
    for c0 in range(0, n_ch, gw):
        cols = slice(c0, c0 + gw)
        acc = cb_ref[:, cols] + cw_ref[0:1, cols] * xext[CONV_HALO - 3:CONV_HALO - 3 + q, cols]
        for k in range(1, SSM_CONV):
            off = CONV_HALO - 3 + k
            acc = acc + cw_ref[k:k + 1, cols] * xext[off:off + q, cols]
        act = acc * jax.nn.sigmoid(acc)
        if c0 < d_inner:
            xs_f[:, cols] = act
            xs_b[:, cols] = act.astype(BF16)
        elif c0 < d_inner + SSM_GROUPS * gs:
            bm[:, c0 - d_inner:c0 - d_inner + gw] = act.astype(BF16)
        else:
            c1 = c0 - d_inner - SSM_GROUPS * gs
            cm[:, c1:c1 + gw] = act.astype(BF16)
    xext[0:CONV_HALO, :] = xext[q:q + CONV_HALO, :]

    v = dt_ref[...] + dtb_ref[...]
    dt = jnp.maximum(v, 0.0) + jnp.log(1.0 + jnp.exp(-jnp.abs(v)))
    a_neg = -jnp.exp(alog_ref[...])
    d_a = dt * a_neg
    row = lax.broadcasted_iota(jnp.int32, (q, q), 0)
    col = lax.broadcasted_iota(jnp.int32, (q, q), 1)
    causal = row >= col
    tril = jnp.where(causal, 1.0, 0.0).astype(BF16)
    a_cum = None
    for part in _split3_bf16(d_a):
        term = jnp.dot(tril, part, preferred_element_type=F32)
        a_cum = term if a_cum is None else a_cum + term
    a_t = a_cum.T
    dt_t = dt.T
    a_end = a_cum[q - 1:q, :]
    ea = jnp.exp(a_cum)
    wgt = dt * jnp.exp(a_end - a_cum)
    ea_hi = ea.astype(BF16)
    ea_lo = (ea - ea_hi.astype(F32)).astype(BF16)
    top = jnp.concatenate([ea_hi, ea_lo], axis=1)
    bot = jnp.concatenate([wgt.astype(BF16), jnp.zeros((q, n_heads_pad), BF16)], axis=1)
    ex[0:q, :] = jnp.dot(top, e_ref[...], preferred_element_type=F32)
    ex[q:2 * q, :] = jnp.dot(bot, e_ref[...], preferred_element_type=F32)

    lane = lax.broadcasted_iota(jnp.int32, (q, LANES), 1)
    lo_half = lane < SSM_HEAD_DIM
    contract_last = (((1,), (1,)), ((), ()))
    contract_first = (((0,), (0,)), ((), ()))
    heads_per_group = gw // SSM_HEAD_DIM

    for g in range(SSM_GROUPS):
        gcols = slice(g * gw, (g + 1) * gw)
        b_g = bm[:, g * gs:(g + 1) * gs]
        c_g = cm[:, g * gs:(g + 1) * gs]
        cb = lax.dot_general(c_g, b_g, contract_last, preferred_element_type=F32)
        s_in = state[g]
        y_off = jnp.dot(c_g, s_in.astype(BF16), preferred_element_type=F32)
        for pr in range(heads_per_group // 2):
            pcols = slice(g * gw + pr * LANES, g * gw + (pr + 1) * LANES)
            x_pair = xs_b[:, pcols]
            halves = []
            for half in range(2):
                h = g * heads_per_group + 2 * pr + half
                diff = a_cum[:, h:h + 1] - a_t[h:h + 1, :]
                dec = jnp.exp(jnp.where(causal, diff, NEG_BIG))
                mat = (cb * dec * dt_t[h:h + 1, :]).astype(BF16)
                halves.append(jnp.dot(mat, x_pair, preferred_element_type=F32))
            y_diag = jnp.where(lo_half, halves[0], halves[1])
            y_scr[:, pcols] = (y_diag + y_off[:, pr * LANES:(pr + 1) * LANES] * ex[0:q, pcols]
                               + dskip_ref[:, pcols] * xs_f[:, pcols])
        xw = (xs_f[:, gcols] * ex[q:2 * q, gcols]).astype(BF16)
        upd = lax.dot_general(b_g, xw, contract_first, preferred_element_type=F32)
        state[g] = s_in * ex[q - 1:q, gcols] + upd

        zg = z_ref[:, gcols].astype(F32)
        yz = y_scr[:, gcols] * (zg * jax.nn.sigmoid(zg))
        ms = jnp.mean(yz * yz, axis=-1, keepdims=True)
        y_ref[:, gcols] = (yz * lax.rsqrt(ms + NORM_EPS) * ng_ref[:, gcols]).astype(y_ref.dtype)


def _ssd(zxbc, dt_raw, conv_w, conv_b, dt_bias_pad, a_log_pad, d_exp, norm_g, expand, batch, seq):
    t = zxbc.shape[0]
    d_inner = d_exp.shape[1]
    n_conv = zxbc.shape[1] - d_inner
    q = SSM_CHUNK
    nc = seq // q
    gw = d_inner // SSM_GROUPS
    part_w = n_conv // 3
    first_part = d_inner // part_w
    row_map = lambda b, c: (b * nc + c, 0)
    const = lambda b, c: (0, 0)
    part_specs = [pl.BlockSpec((q, part_w), functools.partial(lambda b, c, k: (b * nc + c, first_part + k), k=k))
                  for k in range(3)]
    return pl.pallas_call(
        functools.partial(_ssd_kernel, d_inner=d_inner),
        grid=(batch, nc),
        in_specs=[
            pl.BlockSpec((q, d_inner), row_map),
            *part_specs,
            pl.BlockSpec((q, LANES), row_map),
            pl.BlockSpec((SSM_CONV, n_conv), const),
            pl.BlockSpec((1, n_conv), const),
            pl.BlockSpec((1, LANES), const),
            pl.BlockSpec((1, LANES), const),
            pl.BlockSpec((1, d_inner), const),
            pl.BlockSpec((1, d_inner), const),
            pl.BlockSpec((2 * LANES, d_inner), const),
        ],
        out_specs=pl.BlockSpec((q, d_inner), row_map),
        out_shape=jax.ShapeDtypeStruct((t, d_inner), BF16),
        scratch_shapes=[
            pltpu.VMEM((SSM_GROUPS, SSM_STATE, gw), F32),
            pltpu.VMEM((q + CONV_HALO, n_conv), F32),
            pltpu.VMEM((q, d_inner), F32),
            pltpu.VMEM((q, d_inner), BF16),
            pltpu.VMEM((q, SSM_GROUPS * SSM_STATE), BF16),
            pltpu.VMEM((q, SSM_GROUPS * SSM_STATE), BF16),
            pltpu.VMEM((2 * q, d_inner), F32),
            pltpu.VMEM((q, d_inner), F32),
        ],
        compiler_params=_cparams(2),
        name="ssd_scan",
    )(zxbc, zxbc, zxbc, zxbc, dt_raw, conv_w, conv_b, dt_bias_pad, a_log_pad, d_exp, norm_g, expand)


def _mm_res_kernel(a_ref, w_ref, x_ref, gate_ref, o_ref):
    y = jnp.dot(a_ref[...], w_ref[...], preferred_element_type=F32)
    o_ref[...] = x_ref[...] + gate_ref[...] * y


def _matmul_residual(a, w_bf, x, modr, gate_base, seq):
    t, k = a.shape
    d = x.shape[1]
    tm, tn = 1024, 512
    tpb = seq // tm
    gate_spec = pl.BlockSpec((None, 1, tn), lambda i, j: (gate_base + i // tpb, 0, j))
    return pl.pallas_call(
        _mm_res_kernel,
        grid=(t // tm, d // tn),
        in_specs=[
            pl.BlockSpec((tm, k), lambda i, j: (i, 0)),
            pl.BlockSpec((k, tn), lambda i, j: (0, j)),
            pl.BlockSpec((tm, tn), lambda i, j: (i, j)),
            gate_spec,
        ],
        out_specs=pl.BlockSpec((tm, tn), lambda i, j: (i, j)),
        out_shape=jax.ShapeDtypeStruct((t, d), F32),
        compiler_params=_cparams(2),
        name="ssm_out_proj",
    )(a, w_bf, x, modr)


def kernel(x, c, ada_w, ada_b, norm_mix, norm_ffn, ffn_w1, ffn_w2, even_w_in, pool_w, pool_scale, even_w_out,
           ssm_w_in, ssm_conv_w, ssm_conv_b, ssm_dt_bias, ssm_a_log, ssm_d, ssm_norm, ssm_w_out, final_norm):
    batch, seq, d = x.shape
    depth = ada_w.shape[0]
    t = batch * seq
    xf = x.reshape(t, d)

    c_pad = jnp.pad(c, ((0, 8 - batch), (0, 0)))
    mod = _ada_mod(c_pad, ada_w, ada_b)[:, :batch]
    modr = mod.reshape(depth, batch, 6, d).transpose(0, 2, 1, 3).reshape(depth * 6 * batch, 1, d)

    d_inner = ssm_w_out.shape[1]
    n_heads = ssm_dt_bias.shape[1]
    head_of_channel = jnp.arange(d_inner) // SSM_HEAD_DIM
    expand1 = (jnp.arange(LANES)[:, None] == head_of_channel[None, :]).astype(BF16)
    expand = jnp.concatenate([expand1, expand1], axis=0)

    for i in range(depth):
        base = i * 6 * batch
        j = i // 2
        g_mix = norm_mix[i].reshape(1, d)
        g_ffn = norm_ffn[i].reshape(1, d)
        if i % 2 == 0:
            qkv, u = _even_in_proj(xf, g_mix, modr, base, even_w_in[j].astype(BF16), seq)
            merged = None
            for gi in range(N_GROUPS):
                merged = _attn_group(qkv, merged, gi, batch, seq)
            xf = _even_out_proj(merged, u, pool_w[j].astype(BF16), pool_scale[j].reshape(1, POOL_W),
                                even_w_out[j].astype(BF16), xf, modr, base + 2 * batch, seq)
        else:
            w_in = ssm_w_in[j]
            n_main = w_in.shape[1] - n_heads
            wdt = jnp.pad(w_in[:, n_main:], ((0, 0), (0, LANES - n_heads))).astype(BF16)
            zxbc, dt_raw = _ssm_in_proj(xf, g_mix, modr, base, w_in[:, :n_main].astype(BF16), wdt, seq)
            pad_h = ((0, 0), (0, LANES - n_heads))
            y = _ssd(zxbc, dt_raw, ssm_conv_w[j], ssm_conv_b[j].reshape(1, -1),
                     jnp.pad(ssm_dt_bias[j].reshape(1, -1), pad_h), jnp.pad(ssm_a_log[j].reshape(1, -1), pad_h),
                     jnp.repeat(ssm_d[j], SSM_HEAD_DIM).reshape(1, d_inner), ssm_norm[j].reshape(1, d_inner),
                     expand, batch, seq)
            xf = _matmul_residual(y, ssm_w_out[j].astype(BF16), xf, modr, base + 2 * batch, seq)
        xf = _ffn(xf, g_ffn, modr, base, ffn_w1[i].astype(BF16), ffn_w2[i].astype(BF16),
                  final_norm.reshape(1, d), seq, final_norm=(i == depth - 1))
    return xf.reshape(batch, seq, d)
```

```python
import functools

import jax
import jax.numpy as jnp
from jax import lax
from jax.experimental import pallas as pl
from jax.experimental.pallas import tpu as pltpu

F32 = jnp.float32
BF16 = jnp.bfloat16

NORM_EPS = 1e-6
NEG_BIG = -1e30

ATTN_GROUPS = ((128, 1), (512, 4), (2048, 16))
ATTN_HEADS = 8
ATTN_HEAD_DIM = 64
ATTN_BLOCK = 128
ATTN_GW = ATTN_HEADS * ATTN_HEAD_DIM
N_GROUPS = len(ATTN_GROUPS)
QKV_W = 3 * N_GROUPS * ATTN_GW
POOL_WINDOWS = (2, 4, 8, 16)
POOL_GW = 128
POOL_W = len(POOL_WINDOWS) * POOL_GW
POOL_HALO = 16

SSM_HEAD_DIM = 64
SSM_GROUPS = 8
SSM_STATE = 128
SSM_CONV = 4
SSM_CHUNK = 128
CONV_HALO = 8

VMEM_LIMIT = 56 * 1024 * 1024
LANES = 128
ROW_TILE = 1024


def _cparams(n_axes):
    return pltpu.CompilerParams(dimension_semantics=("arbitrary",) * n_axes, vmem_limit_bytes=VMEM_LIMIT)


def _ada_kernel(c_ref, w_ref, b_ref, o_ref):
    c = c_ref[...]
    cond = (c * jax.nn.sigmoid(c)).astype(BF16)
    w = w_ref[...].astype(BF16)
    o_ref[...] = jnp.dot(cond, w, preferred_element_type=F32) + b_ref[...]


def _ada_mod(c_pad, ada_w, ada_b):
    depth, d, n = ada_w.shape
    rows = c_pad.shape[0]
    tn = 1024
    return pl.pallas_call(
        _ada_kernel,
        grid=(depth, n // tn),
        in_specs=[
            pl.BlockSpec((rows, d), lambda l, j: (0, 0)),
            pl.BlockSpec((None, d, tn), lambda l, j: (l, 0, j)),
            pl.BlockSpec((None, 1, tn), lambda l, j: (l, 0, j)),
        ],
        out_specs=pl.BlockSpec((None, rows, tn), lambda l, j: (l, 0, j)),
        out_shape=jax.ShapeDtypeStruct((depth, rows, n), F32),
        compiler_params=_cparams(2),
        name="ada_mod",
    )(c_pad, ada_w, ada_b.reshape(depth, 1, n))


def _norm_mod_rows(x_ref, g_ref, sh_ref, sc_ref, store, rows_per_step=256):
    tm = x_ref.shape[0]
    gain = g_ref[...] * (1.0 + sc_ref[...])
    shift = sh_ref[...]

    def body(i, carry):
        r0 = pl.multiple_of(i * rows_per_step, rows_per_step)
        x = x_ref[pl.ds(r0, rows_per_step), :]
        ms = jnp.mean(x * x, axis=-1, keepdims=True)
        store(r0, x * lax.rsqrt(ms + NORM_EPS) * gain + shift, rows_per_step)
        return carry

    lax.fori_loop(0, tm // rows_per_step, body, 0)


def _norm_mod_to_scratch(x_ref, g_ref, sh_ref, sc_ref, h_scr):
    def store(r0, h, n):
        h_scr[pl.ds(r0, n), :] = h.astype(BF16)

    _norm_mod_rows(x_ref, g_ref, sh_ref, sc_ref, store)


def _mod_spec(d, idx_base, tiles_per_batch):
    return pl.BlockSpec((None, 1, d), lambda i, *_: (idx_base + i // tiles_per_batch, 0, 0))


def _even_in_kernel(x_ref, g_ref, sh_ref, sc_ref, w_ref, wu_ref, qkv_ref, u_ref, hf, h3):
    j = pl.program_id(1)
    tm = x_ref.shape[0]
    n_slabs = hf.shape[0]

    @pl.when(j == 0)
    def _():
        def store(r0, h, n):
            h3[0, pl.ds(r0, n), :] = h.astype(BF16)
            for c in range(n_slabs):
                hf[c, pl.ds(r0, n), :] = h[:, c * LANES:(c + 1) * LANES]

        _norm_mod_rows(x_ref, g_ref, sh_ref, sc_ref, store)
        u_ref[...] = jnp.dot(h3[0], wu_ref[...].astype(BF16), preferred_element_type=F32)
        for gi, (_, dil) in enumerate(ATTN_GROUPS):
            if dil == 1:
                continue
            rows = tm // dil
            for c in range(n_slabs):
                for r in range(dil):
                    piece = hf[c, pl.ds(r, rows, stride=dil), :]
                    h3[gi, r * rows:(r + 1) * rows, c * LANES:(c + 1) * LANES] = piece.astype(BF16)

    gi = j // N_GROUPS
    qkv_ref[...] = jnp.dot(h3[gi], w_ref[...].astype(BF16), preferred_element_type=F32).astype(BF16)


def _even_in_proj(x, g, modr, mod_base, w_in, seq):
    t, d = x.shape
    tm, tn = ROW_TILE, ATTN_GW
    tpb = seq // tm
    batch = t // seq
    n_u = w_in.shape[1] - QKV_W
    return pl.pallas_call(
        _even_in_kernel,
        grid=(t // tm, QKV_W // tn),
        in_specs=[
            pl.BlockSpec((tm, d), lambda i, j: (i, 0), pipeline_mode=pl.Buffered(1)),
            pl.BlockSpec((1, d), lambda i, j: (0, 0)),
            _mod_spec(d, mod_base + 0 * batch, tpb),
            _mod_spec(d, mod_base + 1 * batch, tpb),
            pl.BlockSpec((d, tn), lambda i, j: (0, (j % N_GROUPS) * N_GROUPS + j // N_GROUPS)),
            pl.BlockSpec((d, n_u), lambda i, j: (0, QKV_W // n_u), pipeline_mode=pl.Buffered(1)),
        ],
        out_specs=[
            pl.BlockSpec((tm, tn), lambda i, j: (i, j)),
            pl.BlockSpec((tm, n_u), lambda i, j: (i, 0)),
        ],
        out_shape=[
            jax.ShapeDtypeStruct((t, QKV_W), BF16),
            jax.ShapeDtypeStruct((t, n_u), F32),
        ],
        scratch_shapes=[pltpu.VMEM((d // LANES, tm, LANES), F32), pltpu.VMEM((N_GROUPS, tm, d), BF16)],
        compiler_params=_cparams(2),
        name="even_in_proj",
    )(x, g, modr, modr, w_in, w_in)


def _rows_of(ref, sl):
    if len(ref.shape) == 2:
        return ref[:, sl]
    return jnp.concatenate([ref[p, :, sl] for p in range(ref.shape[0])], axis=0)


def _attn_kernel(*refs, first, last, dil):
    if first:
        q_ref, k_ref, v_ref, o_ref, l_ref, kk, vv = refs
        op_ref = lp_ref = None
    elif last:
        q_ref, k_ref, v_ref, op_ref, lp_ref, o_ref, kk, vv = refs
        l_ref = None
    else:
        q_ref, k_ref, v_ref, op_ref, lp_ref, o_ref, l_ref, kk, vv = refs
    n = pl.program_id(1)
    r = pl.program_id(2)
    blk = ATTN_BLOCK
    full = slice(None)

    @pl.when(n == 0)
    def _():
        kk[r, 0:blk, :] = jnp.zeros((blk, ATTN_GW), BF16)
        vv[r, 0:blk, :] = jnp.zeros((blk, ATTN_GW), BF16)

    k_cur = _rows_of(k_ref, full)
    v_cur = _rows_of(v_ref, full)
    kk[r, blk:2 * blk, :] = k_cur
    vv[r, blk:2 * blk, :] = v_cur

    qi = lax.broadcasted_iota(jnp.int32, (2 * blk, 2 * blk), 0) & (blk - 1)
    kj = lax.broadcasted_iota(jnp.int32, (2 * blk, 2 * blk), 1)
    prev_pen = jnp.where(n > 0, 0.0, NEG_BIG)
    bias = jnp.where(kj < blk,
                     jnp.where(kj >= qi, prev_pen, NEG_BIG),
                     jnp.where(kj - blk <= qi, 0.0, NEG_BIG))
    lane = lax.broadcasted_iota(jnp.int32, (blk, LANES), 1)
    lo_half = lane < ATTN_HEAD_DIM
    contract_last = (((1,), (1,)), ((), ()))
    rows_out = pl.ds(r, blk, stride=dil) if dil > 1 else slice(None)

    for hp in range(ATTN_GW // LANES):
        sl = slice(hp * LANES, (hp + 1) * LANES)
        qp = _rows_of(q_ref, sl) * jnp.asarray(ATTN_HEAD_DIM ** -0.5, BF16)
        zero = jnp.zeros_like(qp)
        qm = jnp.concatenate([jnp.where(lo_half, qp, zero), jnp.where(lo_half, zero, qp)], axis=0)
        s = lax.dot_general(qm, kk[r, :, sl], contract_last, preferred_element_type=F32) + bias
        m = jnp.max(s, axis=-1, keepdims=True)
        p = jnp.exp(s - m)
        den = jnp.sum(p, axis=-1, keepdims=True)
        acc = jnp.dot(p.astype(BF16), vv[r, :, sl], preferred_element_type=F32)
        o2 = acc / den
        lse2 = jnp.broadcast_to(m + jnp.log(den), (2 * blk, LANES))
        o_new = jnp.where(lo_half, o2[0:blk], o2[blk:2 * blk])
        lse_new = jnp.where(lo_half, lse2[0:blk], lse2[blk:2 * blk])
        if not first:
            lse_old = lp_ref[hp, rows_out, :]
            o_old = op_ref[hp, rows_out, :]
            mx = jnp.maximum(lse_old, lse_new)
            w_old = jnp.exp(lse_old - mx)
            w_new = jnp.exp(lse_new - mx)
            tot = w_old + w_new
            o_new = (o_old * w_old + o_new * w_new) / tot
            lse_new = mx + jnp.log(tot)
        o_ref[hp, rows_out, :] = o_new
        if not last:
            l_ref[hp, rows_out, :] = lse_new

    kk[r, 0:blk, :] = k_cur
    vv[r, 0:blk, :] = v_cur


def _attn_group(qkv, prev, gi, batch, seq):
    dil = ATTN_GROUPS[gi][1]
    first, last = gi == 0, gi == N_GROUPS - 1
    t = batch * seq
    nb = seq // dil // ATTN_BLOCK
    chunk = ROW_TILE // dil
    qkv_v = qkv.reshape(batch, seq // ROW_TILE, dil, chunk, QKV_W)
    if chunk >= ATTN_BLOCK:
        sub = chunk // ATTN_BLOCK
        blk_shape = (None, None, None, ATTN_BLOCK, ATTN_GW)
        idx = lambda b, n, r, col: (b, n // sub, r, n % sub, col)
    else:
        blk_shape = (None, ATTN_BLOCK // chunk, None, chunk, ATTN_GW)
        idx = lambda b, n, r, col: (b, n, r, 0, col)
    in_specs = [pl.BlockSpec(blk_shape, functools.partial(lambda b, n, r, col: idx(b, n, r, col), col=gi * 3 + which))
                for which in range(3)]
    args = [qkv_v, qkv_v, qkv_v]
    n_pairs = ATTN_GW // LANES
    span = ATTN_BLOCK * dil
    o_spec = pl.BlockSpec((n_pairs, span, LANES), lambda b, n, r: (0, b * nb + n, 0))
    if not first:
        in_specs += [o_spec, o_spec]
        args += list(prev)
    o_shape = jax.ShapeDtypeStruct((n_pairs, t, LANES), F32)
    outs = pl.pallas_call(
        functools.partial(_attn_kernel, first=first, last=last, dil=dil),
        grid=(batch, nb, dil),
        in_specs=in_specs,
        out_specs=[o_spec] if last else [o_spec, o_spec],
        out_shape=[o_shape] if last else [o_shape, o_shape],
        scratch_shapes=[pltpu.VMEM((dil, 2 * ATTN_BLOCK, ATTN_GW), BF16),
                        pltpu.VMEM((dil, 2 * ATTN_BLOCK, ATTN_GW), BF16)],
        compiler_params=_cparams(3),
        name=f"dilated_attn_g{gi}",
    )(*args)
    return outs[0] if last else tuple(outs)


def _even_out_kernel(attn_ref, u_ref, halo_ref, pw_ref, ps_ref, w_ref, x_ref, gate_ref, o_ref, ue, a_scr, w_bf,
                     *, seq):
    tm = u_ref.shape[0]
    i = pl.program_id(0)

    @pl.when(i == 0)
    def _():
        w_bf[...] = w_ref[...].astype(BF16)

    row0 = (i * tm) % seq
    halo_ok = jnp.where(row0 > 0, 1.0, 0.0)
    ue[0:POOL_HALO, :] = halo_ref[...] * halo_ok
    ue[POOL_HALO:POOL_HALO + tm, :] = u_ref[...]
    for hp in range(ATTN_GW // LANES):
        a_scr[:, hp * LANES:(hp + 1) * LANES] = attn_ref[hp].astype(BF16)
    pos = row0 + lax.broadcasted_iota(jnp.int32, (tm, POOL_GW), 0)
    for gi, win in enumerate(POOL_WINDOWS):
        cols = slice(gi * POOL_GW, (gi + 1) * POOL_GW)
        tok = ue[POOL_HALO:POOL_HALO + tm, cols]
        acc = tok
        for back in range(1, win):
            acc = acc + ue[POOL_HALO - back:POOL_HALO - back + tm, cols]
        cnt = jnp.minimum(pos + 1, win).astype(F32)
        diff = acc / cnt - tok
        y = jnp.dot(diff.astype(BF16), pw_ref[gi].astype(BF16), preferred_element_type=F32) * ps_ref[:, cols]
        a_scr[:, ATTN_GW + gi * POOL_GW:ATTN_GW + (gi + 1) * POOL_GW] = y.astype(BF16)
    y = jnp.dot(a_scr[...], w_bf[...], preferred_element_type=F32)
    o_ref[...] = x_ref[...] + gate_ref[...] * y


def _even_out_proj(attn, u, pool_w, pool_scale, w_out, x, modr, gate_base, seq):
    t, d = x.shape
    tm = 512
    tpb = seq // tm
    k = w_out.shape[0]
    halo_blocks = tm // POOL_HALO
    n_pairs = attn.shape[0]
    return pl.pallas_call(
        functools.partial(_even_out_kernel, seq=seq),
        grid=(t // tm,),
        in_specs=[
            pl.BlockSpec((n_pairs, tm, LANES), lambda i: (0, i, 0)),
            pl.BlockSpec((tm, POOL_W), lambda i: (i, 0)),
            pl.BlockSpec((POOL_HALO, POOL_W), lambda i: (jnp.maximum(i * halo_blocks - 1, 0), 0)),
            pl.BlockSpec(pool_w.shape, lambda i: (0, 0, 0)),
            pl.BlockSpec((1, POOL_W), lambda i: (0, 0)),
            pl.BlockSpec((k, d), lambda i: (0, 0), pipeline_mode=pl.Buffered(1)),
            pl.BlockSpec((tm, d), lambda i: (i, 0)),
            _mod_spec(d, gate_base, tpb),
        ],
        out_specs=pl.BlockSpec((tm, d), lambda i: (i, 0)),
        out_shape=jax.ShapeDtypeStruct((t, d), F32),
        scratch_shapes=[pltpu.VMEM((tm + POOL_HALO, POOL_W), F32), pltpu.VMEM((tm, k), BF16),
                        pltpu.VMEM((k, d), BF16)],
        compiler_params=_cparams(1),
        name="even_out_proj",
    )(attn, u, u, pool_w, pool_scale, w_out, x, modr)


def _ffn_kernel(x_ref, g_ref, sh_ref, sc_ref, gate_ref, w1_ref, w2_ref, fin_ref, o_ref, h_scr, *, final_norm):
    j = pl.program_id(1)

    @pl.when(j == 0)
    def _():
        _norm_mod_to_scratch(x_ref, g_ref, sh_ref, sc_ref, h_scr)
        o_ref[...] = jnp.zeros_like(o_ref)

    a = jnp.dot(h_scr[...], w1_ref[...].astype(BF16), preferred_element_type=F32)
    a = jnp.maximum(a, 0.0)
    o_ref[...] += jnp.dot((a * a).astype(BF16), w2_ref[...].astype(BF16), preferred_element_type=F32)

    @pl.when(j == pl.num_programs(1) - 1)
    def _():
        rows = 256
        gate = gate_ref[...]
        fin = fin_ref[...]

        def body(i, carry):
            r0 = pl.multiple_of(i * rows, rows)
            y = x_ref[pl.ds(r0, rows), :] + gate * o_ref[pl.ds(r0, rows), :]
            if final_norm:
                ms = jnp.mean(y * y, axis=-1, keepdims=True)
                y = y * lax.rsqrt(ms + NORM_EPS) * fin
            o_ref[pl.ds(r0, rows), :] = y
            return carry

        lax.fori_loop(0, x_ref.shape[0] // rows, body, 0)


def _ffn(x, g, modr, mod_base, w1, w2, fin, seq, final_norm):
    t, d = x.shape
    hdim = w1.shape[1]
    tm, th = ROW_TILE, 512
    tpb = seq // tm
    batch = t // seq
    return pl.pallas_call(
        functools.partial(_ffn_kernel, final_norm=final_norm),
        grid=(t // tm, hdim // th),
        in_specs=[
            pl.BlockSpec((tm, d), lambda i, j: (i, 0), pipeline_mode=pl.Buffered(1)),
            pl.BlockSpec((1, d), lambda i, j: (0, 0)),
            _mod_spec(d, mod_base + 3 * batch, tpb),
            _mod_spec(d, mod_base + 4 * batch, tpb),
            _mod_spec(d, mod_base + 5 * batch, tpb),
            pl.BlockSpec((d, th), lambda i, j: (0, j)),
            pl.BlockSpec((th, d), lambda i, j: (j, 0)),
            pl.BlockSpec((1, d), lambda i, j: (0, 0)),
        ],
        out_specs=pl.BlockSpec((tm, d), lambda i, j: (i, 0)),
        out_shape=jax.ShapeDtypeStruct((t, d), F32),
        scratch_shapes=[pltpu.VMEM((tm, d), BF16)],
        compiler_params=_cparams(2),
        name="ffn_final" if final_norm else "ffn",
    )(x, g, modr, modr, modr, w1, w2, fin)


def _ssm_in_kernel(x_ref, g_ref, sh_ref, sc_ref, w_ref, wdt_ref, zxbc_ref, dt_ref, h_scr):
    @pl.when(pl.program_id(1) == 0)
    def _():
        _norm_mod_to_scratch(x_ref, g_ref, sh_ref, sc_ref, h_scr)
        dt_ref[...] = jnp.dot(h_scr[...], wdt_ref[...], preferred_element_type=F32)

    zxbc_ref[...] = jnp.dot(h_scr[...], w_ref[...].astype(BF16), preferred_element_type=F32).astype(BF16)


def _ssm_in_proj(x, g, modr, mod_base, w_in, wdt_bf, n_main, seq):
    t, d = x.shape
    tm, tn = ROW_TILE, 1024
    tpb = seq // tm
    batch = t // seq
    return pl.pallas_call(
        _ssm_in_kernel,
        grid=(t // tm, n_main // tn),
        in_specs=[
            pl.BlockSpec((tm, d), lambda i, j: (i, 0)),
            pl.BlockSpec((1, d), lambda i, j: (0, 0)),
            _mod_spec(d, mod_base + 0 * batch, tpb),
            _mod_spec(d, mod_base + 1 * batch, tpb),
            pl.BlockSpec((d, tn), lambda i, j: (0, j)),
            pl.BlockSpec((d, LANES), lambda i, j: (0, 0)),
        ],
        out_specs=[
            pl.BlockSpec((tm, tn), lambda i, j: (i, j)),
            pl.BlockSpec((tm, LANES), lambda i, j: (i, 0)),
        ],
        out_shape=[
            jax.ShapeDtypeStruct((t, n_main), BF16),
            jax.ShapeDtypeStruct((t, LANES), F32),
        ],
        scratch_shapes=[pltpu.VMEM((tm, d), BF16)],
        compiler_params=_cparams(2),
        name="ssm_in_proj",
    )(x, g, modr, modr, w_in, wdt_bf)


def _split3_bf16(v):
    hi = v.astype(BF16)
    r1 = v - hi.astype(F32)
    mid = r1.astype(BF16)
    lo = (r1 - mid.astype(F32)).astype(BF16)
    return hi, mid, lo


def _ssd_kernel(z_ref, xr_ref, bcr_ref, dt_ref, cw_ref, cb_ref, dtb_ref, alog_ref, dskip_ref, ng_ref, e_ref, y_ref,
                state, tail, xs_f, xs_b, bc, ex, y_scr):
    q = SSM_CHUNK
    d_inner = xr_ref.shape[1]
    gw = d_inner // SSM_GROUPS
    gs = SSM_STATE
    bc_split = SSM_GROUPS * gs

    @pl.when(pl.program_id(1) == 0)
    def _():
        state[...] = jnp.zeros_like(state)
        tail[...] = jnp.zeros_like(tail)

    for c0 in range(0, d_inner + 2 * bc_split, gw):
        cols = slice(c0, c0 + gw)
        if c0 < d_inner:
            cur = xr_ref[:, cols].astype(F32)
        else:
            cur = bcr_ref[:, c0 - d_inner:c0 - d_inner + gw].astype(F32)
        ext = jnp.concatenate([tail[:, cols], cur], axis=0)
        acc = cb_ref[:, cols] + cw_ref[SSM_CONV - 1:SSM_CONV, cols] * cur
        for k in range(SSM_CONV - 1):
            shifted = pltpu.roll(ext, SSM_CONV - 1 - k, 0)[CONV_HALO:CONV_HALO + q]
            acc = acc + cw_ref[k:k + 1, cols] * shifted
        tail[:, cols] = cur[q - CONV_HALO:q]
        act = acc * jax.nn.sigmoid(acc)
        if c0 < d_inner:
            xs_f[:, cols] = act
            xs_b[:, cols] = act.astype(BF16)
        else:
            bc[:, c0 - d_inner:c0 - d_inner + gw] = act.astype(BF16)

    v = dt_ref[...] + dtb_ref[...]
    dt = jnp.maximum(v, 0.0) + jnp.log(1.0 + jnp.exp(-jnp.abs(v)))
    a_neg = -jnp.exp(alog_ref[...])
    d_a = dt * a_neg
    row = lax.broadcasted_iota(jnp.int32, (q, q), 0)
    col = lax.broadcasted_iota(jnp.int32, (q, q), 1)
    causal = row >= col
    tril = jnp.where(causal, 1.0, 0.0).astype(BF16)
    a_cum = None
    for part in _split3_bf16(d_a):
        term = jnp.dot(tril, part, preferred_element_type=F32)
        a_cum = term if a_cum is None else a_cum + term
    a_t = a_cum.T
    dt_t = dt.T
    a_end = a_cum[q - 1:q, :]
    ea = jnp.exp(a_cum)
    wgt = dt * jnp.exp(a_end - a_cum)
    ea_hi = ea.astype(BF16)
    ea_lo = (ea - ea_hi.astype(F32)).astype(BF16)
    top = jnp.concatenate([ea_hi, ea_lo], axis=1)
    bot = jnp.concatenate([wgt.astype(BF16), jnp.zeros((q, LANES), BF16)], axis=1)
    ex[0:q, :] = jnp.dot(top, e_ref[...], preferred_element_type=F32)
    ex[q:2 * q, :] = jnp.dot(bot, e_ref[...], preferred_element_type=F32)

    lane = lax.broadcasted_iota(jnp.int32, (q, LANES), 1)
    lo_half = lane < SSM_HEAD_DIM
    contract_last = (((1,), (1,)), ((), ()))
    contract_first = (((0,), (0,)), ((), ()))
    heads_per_group = gw // SSM_HEAD_DIM

    for g in range(SSM_GROUPS):
        gcols = slice(g * gw, (g + 1) * gw)
        b_g = bc[:, g * gs:(g + 1) * gs]
        c_g = bc[:, bc_split + g * gs:bc_split + (g + 1) * gs]
        cb = lax.dot_general(c_g, b_g, contract_last, preferred_element_type=F32)
        s_in = state[g]
        y_off = jnp.dot(c_g, s_in.astype(BF16), preferred_element_type=F32)
        for pr in range(heads_per_group // 2):
            pcols = slice(g * gw + pr * LANES, g * gw + (pr + 1) * LANES)
            x_pair = xs_b[:, pcols]
            halves = []
            for half in range(2):
                h = g * heads_per_group + 2 * pr + half
                diff = a_cum[:, h:h + 1] - a_t[h:h + 1, :]
                dec = jnp.exp(jnp.where(causal, diff, NEG_BIG))
                mat = (cb * dec * dt_t[h:h + 1, :]).astype(BF16)
                halves.append(jnp.dot(mat, x_pair, preferred_element_type=F32))
            y_diag = jnp.where(lo_half, halves[0], halves[1])
            y_scr[:, pcols] = (y_diag + y_off[:, pr * LANES:(pr + 1) * LANES] * ex[0:q, pcols]
                               + dskip_ref[:, pcols] * xs_f[:, pcols])
        xw = (xs_f[:, gcols] * ex[q:2 * q, gcols]).astype(BF16)
        upd = lax.dot_general(b_g, xw, contract_first, preferred_element_type=F32)
        state[g] = s_in * ex[q - 1:q, gcols] + upd

        zg = z_ref[:, gcols].astype(F32)
        yz = y_scr[:, gcols] * (zg * jax.nn.sigmoid(zg))
        ms = jnp.mean(yz * yz, axis=-1, keepdims=True)
        y_ref[:, gcols] = (yz * lax.rsqrt(ms + NORM_EPS) * ng_ref[:, gcols]).astype(y_ref.dtype)


def _ssd(zxbc, dt_raw, conv_w, conv_b, dt_bias_pad, a_log_pad, d_exp, norm_g, expand, batch, seq):
    t = zxbc.shape[0]
    d_inner = d_exp.shape[1]
    n_bc = zxbc.shape[1] - 2 * d_inner
    n_conv = d_inner + n_bc
    q = SSM_CHUNK
    nc = seq // q
    gw = d_inner // SSM_GROUPS
    row_map = lambda b, c: (b * nc + c, 0)
    const = lambda b, c: (0, 0)
    return pl.pallas_call(
        _ssd_kernel,
        grid=(batch, nc),
        in_specs=[
            pl.BlockSpec((q, d_inner), row_map),
            pl.BlockSpec((q, d_inner), lambda b, c: (b * nc + c, 1)),
            pl.BlockSpec((q, n_bc), lambda b, c: (b * nc + c, 2 * d_inner // n_bc)),
            pl.BlockSpec((q, LANES), row_map),
            pl.BlockSpec((SSM_CONV, n_conv), const),
            pl.BlockSpec((1, n_conv), const),
            pl.BlockSpec((1, LANES), const),
            pl.BlockSpec((1, LANES), const),
            pl.BlockSpec((1, d_inner), const),
            pl.BlockSpec((1, d_inner), const),
            pl.BlockSpec((2 * LANES, d_inner), const),
        ],
        out_specs=pl.BlockSpec((q, d_inner), row_map),
        out_shape=jax.ShapeDtypeStruct((t, d_inner), BF16),
        scratch_shapes=[
            pltpu.VMEM((SSM_GROUPS, SSM_STATE, gw), F32),
            pltpu.VMEM((CONV_HALO, n_conv), F32),
            pltpu.VMEM((q, d_inner), F32),
            pltpu.VMEM((q, d_inner), BF16),
            pltpu.VMEM((q, n_bc), BF16),
            pltpu.VMEM((2 * q, d_inner), F32),
            pltpu.VMEM((q, d_inner), F32),
        ],
        compiler_params=_cparams(2),
        name="ssd_scan",
    )(zxbc, zxbc, zxbc, dt_raw, conv_w, conv_b, dt_bias_pad, a_log_pad, d_exp, norm_g, expand)


def _mm_res_kernel(a_ref, w_ref, x_ref, gate_ref, o_ref):
    y = jnp.dot(a_ref[...], w_ref[...].astype(BF16), preferred_element_type=F32)
    o_ref[...] = x_ref[...] + gate_ref[...] * y


def _matmul_residual(a, w, x, modr, gate_base, seq):
    t, k = a.shape
    d = x.shape[1]
    tm, tn = ROW_TILE, 512
    tpb = seq // tm
    gate_spec = pl.BlockSpec((None, 1, tn), lambda i, j: (gate_base + i // tpb, 0, j))
    return pl.pallas_call(
        _mm_res_kernel,
        grid=(t // tm, d // tn),
        in_specs=[
            pl.BlockSpec((tm, k), lambda i, j: (i, 0)),
            pl.BlockSpec((k, tn), lambda i, j: (0, j)),
            pl.BlockSpec((tm, tn), lambda i, j: (i, j)),
            gate_spec,
        ],
        out_specs=pl.BlockSpec((tm, tn), lambda i, j: (i, j)),
        out_shape=jax.ShapeDtypeStruct((t, d), F32),
        compiler_params=_cparams(2),
        name="ssm_out_proj",
    )(a, w, x, modr)


def kernel(x, c, ada_w, ada_b, norm_mix, norm_ffn, ffn_w1, ffn_w2, even_w_in, pool_w, pool_scale, even_w_out,
           ssm_w_in, ssm_conv_w, ssm_conv_b, ssm_dt_bias, ssm_a_log, ssm_d, ssm_norm, ssm_w_out, final_norm):
    batch, seq, d = x.shape
    depth = ada_w.shape[0]
    t = batch * seq
    xf = x.reshape(t, d)

    c_pad = jnp.pad(c, ((0, 8 - batch), (0, 0)))
    mod = _ada_mod(c_pad, ada_w, ada_b)[:, :batch]
    modr = mod.reshape(depth, batch, 6, d).transpose(0, 2, 1, 3).reshape(depth * 6 * batch, 1, d)

    d_inner = ssm_w_out.shape[1]
    n_heads = ssm_dt_bias.shape[1]
    head_of_channel = jnp.arange(d_inner) // SSM_HEAD_DIM
    expand1 = (jnp.arange(LANES)[:, None] == head_of_channel[None, :]).astype(BF16)
    expand = jnp.concatenate([expand1, expand1], axis=0)

    for i in range(depth):
        base = i * 6 * batch
        j = i // 2
        g_mix = norm_mix[i].reshape(1, d)
        g_ffn = norm_ffn[i].reshape(1, d)
        if i % 2 == 0:
            qkv, u = _even_in_proj(xf, g_mix, modr, base, even_w_in[j], seq)
            merged = None
            for gi in range(N_GROUPS):
                merged = _attn_group(qkv, merged, gi, batch, seq)
            xf = _even_out_proj(merged, u, pool_w[j], pool_scale[j].reshape(1, POOL_W), even_w_out[j], xf, modr,
                                base + 2 * batch, seq)
        else:
            w_in = ssm_w_in[j]
            n_main = w_in.shape[1] - n_heads
            pad_h = ((0, 0), (0, LANES - n_heads))
            wdt = jnp.pad(w_in[:, n_main:], pad_h).astype(BF16)
            zxbc, dt_raw = _ssm_in_proj(xf, g_mix, modr, base, w_in, wdt, n_main, seq)
            y = _ssd(zxbc, dt_raw, ssm_conv_w[j], ssm_conv_b[j].reshape(1, -1),
                     jnp.pad(ssm_dt_bias[j].reshape(1, -1), pad_h),
                     jnp.pad(ssm_a_log[j].reshape(1, -1), pad_h),
                     jnp.repeat(ssm_d[j], SSM_HEAD_DIM).reshape(1, d_inner), ssm_norm[j].reshape(1, d_inner),
                     expand, batch, seq)
            xf = _matmul_residual(y, ssm_w_out[j], xf, modr, base + 2 * batch, seq)
        xf = _ffn(xf, g_ffn, modr, base, ffn_w1[i], ffn_w2[i], final_norm.reshape(1, d), seq,
                  final_norm=(i == depth - 1))
    return xf.reshape(batch, seq, d)
```

```python
import functools

import jax
import jax.numpy as jnp
from jax import lax
from jax.experimental import pallas as pl
from jax.experimental.pallas import tpu as pltpu

F32 = jnp.float32
BF16 = jnp.bfloat16

NORM_EPS = 1e-6
NEG_BIG = -1e30
LOG2E = 1.4426950408889634

ATTN_GROUPS = ((128, 1), (512, 4), (2048, 16))
ATTN_HEADS = 8
ATTN_HEAD_DIM = 64
ATTN_BLOCK = 128
ATTN_GW = ATTN_HEADS * ATTN_HEAD_DIM
N_GROUPS = len(ATTN_GROUPS)
QKV_W = 3 * N_GROUPS * ATTN_GW
POOL_WINDOWS = (2, 4, 8, 16)
POOL_GW = 128
POOL_W = len(POOL_WINDOWS) * POOL_GW
POOL_HALO = 16

SSM_HEAD_DIM = 64
SSM_GROUPS = 8
SSM_STATE = 128
SSM_CONV = 4
SSM_CHUNK = 128
CONV_TAIL_ROWS = 16

VMEM_LIMIT = 56 * 1024 * 1024
LANES = 128
ROW_TILE = 1024


def _cparams(n_axes):
    return pltpu.CompilerParams(dimension_semantics=("arbitrary",) * n_axes, vmem_limit_bytes=VMEM_LIMIT)


def _ada_kernel(c_ref, w_ref, b_ref, o_ref):
    c = c_ref[...]
    cond = (c * jax.nn.sigmoid(c)).astype(BF16)
    w = w_ref[...].astype(BF16)
    o_ref[...] = jnp.dot(cond, w, preferred_element_type=F32) + b_ref[...]


def _ada_mod(c_pad, ada_w, ada_b):
    depth, d, n = ada_w.shape
    rows = c_pad.shape[0]
    tn = 1024
    return pl.pallas_call(
        _ada_kernel,
        grid=(depth, n // tn),
        in_specs=[
            pl.BlockSpec((rows, d), lambda l, j: (0, 0)),
            pl.BlockSpec((None, d, tn), lambda l, j: (l, 0, j)),
            pl.BlockSpec((None, 1, tn), lambda l, j: (l, 0, j)),
        ],
        out_specs=pl.BlockSpec((None, rows, tn), lambda l, j: (l, 0, j)),
        out_shape=jax.ShapeDtypeStruct((depth, rows, n), F32),
        compiler_params=_cparams(2),
        name="ada_mod",
    )(c_pad, ada_w, ada_b.reshape(depth, 1, n))


def _norm_mod_rows(x_ref, g_ref, sh_ref, sc_ref, store, rows_per_step=256):
    tm = x_ref.shape[0]
    gain = g_ref[...] * (1.0 + sc_ref[...])
    shift = sh_ref[...]

    def body(i, carry):
        r0 = pl.multiple_of(i * rows_per_step, rows_per_step)
        x = x_ref[pl.ds(r0, rows_per_step), :]
        ms = jnp.mean(x * x, axis=-1, keepdims=True)
        store(r0, x * lax.rsqrt(ms + NORM_EPS) * gain + shift, rows_per_step)
        return carry

    lax.fori_loop(0, tm // rows_per_step, body, 0)


def _norm_mod_to_scratch(x_ref, g_ref, sh_ref, sc_ref, h_scr):
    def store(r0, h, n):
        h_scr[pl.ds(r0, n), :] = h.astype(BF16)

    _norm_mod_rows(x_ref, g_ref, sh_ref, sc_ref, store)


def _mod_spec(d, idx_base, tiles_per_batch):
    return pl.BlockSpec((None, 1, d), lambda i, *_: (idx_base + i // tiles_per_batch, 0, 0))


def _even_in_kernel(x_ref, g_ref, sh_ref, sc_ref, w_ref, wu_ref, qkv_ref, u_ref, hf, h3):
    j = pl.program_id(1)
    tm = x_ref.shape[0]
    n_slabs = hf.shape[0]

    @pl.when(j == 0)
    def _():
        def store(r0, h, n):
            h3[0, pl.ds(r0, n), :] = h.astype(BF16)
            for c in range(n_slabs):
                hf[c, pl.ds(r0, n), :] = h[:, c * LANES:(c + 1) * LANES]

        _norm_mod_rows(x_ref, g_ref, sh_ref, sc_ref, store)
        u_ref[...] = jnp.dot(h3[0], wu_ref[...].astype(BF16), preferred_element_type=F32)
        for gi, (_, dil) in enumerate(ATTN_GROUPS):
            if dil == 1:
                continue
            rows = tm // dil
            for c in range(n_slabs):
                for r in range(dil):
                    piece = hf[c, pl.ds(r, rows, stride=dil), :]
                    h3[gi, r * rows:(r + 1) * rows, c * LANES:(c + 1) * LANES] = piece.astype(BF16)

    gi = j // N_GROUPS
    qkv_ref[...] = jnp.dot(h3[gi], w_ref[...].astype(BF16), preferred_element_type=F32).astype(BF16)


def _even_in_proj(x, g, modr, mod_base, w_in, seq):
    t, d = x.shape
    tm, tn = ROW_TILE, ATTN_GW
    tpb = seq // tm
    batch = t // seq
    n_u = w_in.shape[1] - QKV_W
    return pl.pallas_call(
        _even_in_kernel,
        grid=(t // tm, QKV_W // tn),
        in_specs=[
            pl.BlockSpec((tm, d), lambda i, j: (i, 0), pipeline_mode=pl.Buffered(1)),
            pl.BlockSpec((1, d), lambda i, j: (0, 0)),
            _mod_spec(d, mod_base + 0 * batch, tpb),
            _mod_spec(d, mod_base + 1 * batch, tpb),
            pl.BlockSpec((d, tn), lambda i, j: (0, (j % N_GROUPS) * N_GROUPS + j // N_GROUPS)),
            pl.BlockSpec((d, n_u), lambda i, j: (0, QKV_W // n_u), pipeline_mode=pl.Buffered(1)),
        ],
        out_specs=[
            pl.BlockSpec((tm, tn), lambda i, j: (i, j)),
            pl.BlockSpec((tm, n_u), lambda i, j: (i, 0)),
        ],
        out_shape=[
            jax.ShapeDtypeStruct((t, QKV_W), BF16),
            jax.ShapeDtypeStruct((t, n_u), F32),
        ],
        scratch_shapes=[pltpu.VMEM((d // LANES, tm, LANES), F32), pltpu.VMEM((N_GROUPS, tm, d), BF16)],
        compiler_params=_cparams(2),
        name="even_in_proj",
    )(x, g, modr, modr, w_in, w_in)


def _rows_of(ref, sl):
    if len(ref.shape) == 2:
        return ref[:, sl]
    return jnp.concatenate([ref[p, :, sl] for p in range(ref.shape[0])], axis=0)


def _attn_kernel(*refs, first, last, dil):
    if first:
        q_ref, k_ref, v_ref, o_ref, l_ref, kk, vv = refs
        op_ref = lp_ref = None
    elif last:
        q_ref, k_ref, v_ref, op_ref, lp_ref, o_ref, kk, vv = refs
        l_ref = None
    else:
        q_ref, k_ref, v_ref, op_ref, lp_ref, o_ref, l_ref, kk, vv = refs
    n = pl.program_id(1)
    r = pl.program_id(2)
    blk = ATTN_BLOCK
    full = slice(None)

    @pl.when(n == 0)
    def _():
        kk[r, 0:blk, :] = jnp.zeros((blk, ATTN_GW), BF16)
        vv[r, 0:blk, :] = jnp.zeros((blk, ATTN_GW), BF16)

    k_cur = _rows_of(k_ref, full)
    v_cur = _rows_of(v_ref, full)
    kk[r, blk:2 * blk, :] = k_cur
    vv[r, blk:2 * blk, :] = v_cur

    qi = lax.broadcasted_iota(jnp.int32, (2 * blk, 2 * blk), 0) & (blk - 1)
    kj = lax.broadcasted_iota(jnp.int32, (2 * blk, 2 * blk), 1)
    prev_pen = jnp.where(n > 0, 0.0, NEG_BIG)
    bias = jnp.where(kj < blk,
                     jnp.where(kj >= qi, prev_pen, NEG_BIG),
                     jnp.where(kj - blk <= qi, 0.0, NEG_BIG))
    lane = lax.broadcasted_iota(jnp.int32, (blk, LANES), 1)
    lo_half = lane < ATTN_HEAD_DIM
    contract_last = (((1,), (1,)), ((), ()))
    rows_out = pl.ds(r, blk, stride=dil) if dil > 1 else slice(None)

    for hp in range(ATTN_GW // LANES):
        sl = slice(hp * LANES, (hp + 1) * LANES)
        qp = _rows_of(q_ref, sl) * jnp.asarray(ATTN_HEAD_DIM ** -0.5, BF16)
        zero = jnp.zeros_like(qp)
        qm = jnp.concatenate([jnp.where(lo_half, qp, zero), jnp.where(lo_half, zero, qp)], axis=0)
        s = lax.dot_general(qm, kk[r, :, sl], contract_last, preferred_element_type=F32) + bias
        m = jnp.max(s, axis=-1, keepdims=True)
        p = jnp.exp(s - m)
        den = jnp.sum(p, axis=-1, keepdims=True)
        acc = jnp.dot(p.astype(BF16), vv[r, :, sl], preferred_element_type=F32)
        o2 = acc / den
        lse2 = jnp.broadcast_to(m + jnp.log(den), (2 * blk, LANES))
        o_new = jnp.where(lo_half, o2[0:blk], o2[blk:2 * blk])
        lse_new = jnp.where(lo_half, lse2[0:blk], lse2[blk:2 * blk])
        if not first:
            lse_old = lp_ref[hp, rows_out, :]
            o_old = op_ref[hp, rows_out, :]
            mx = jnp.maximum(lse_old, lse_new)
            w_old = jnp.exp(lse_old - mx)
            w_new = jnp.exp(lse_new - mx)
            tot = w_old + w_new
            o_new = (o_old * w_old + o_new * w_new) / tot
            lse_new = mx + jnp.log(tot)
        o_ref[hp, rows_out, :] = o_new
        if not last:
            l_ref[hp, rows_out, :] = lse_new

    kk[r, 0:blk, :] = k_cur
    vv[r, 0:blk, :] = v_cur


def _attn_group(qkv, prev, gi, batch, seq):
    dil = ATTN_GROUPS[gi][1]
    first, last = gi == 0, gi == N_GROUPS - 1
    t = batch * seq
    nb = seq // dil // ATTN_BLOCK
    chunk = ROW_TILE // dil
    qkv_v = qkv.reshape(batch, seq // ROW_TILE, dil, chunk, QKV_W)
    if chunk >= ATTN_BLOCK:
        sub = chunk // ATTN_BLOCK
        blk_shape = (None, None, None, ATTN_BLOCK, ATTN_GW)
        idx = lambda b, n, r, col: (b, n // sub, r, n % sub, col)
    else:
        blk_shape = (None, ATTN_BLOCK // chunk, None, chunk, ATTN_GW)
        idx = lambda b, n, r, col: (b, n, r, 0, col)
    in_specs = [pl.BlockSpec(blk_shape, functools.partial(lambda b, n, r, col: idx(b, n, r, col), col=gi * 3 + which))
                for which in range(3)]
    args = [qkv_v, qkv_v, qkv_v]
    n_pairs = ATTN_GW // LANES
    span = ATTN_BLOCK * dil
    o_spec = pl.BlockSpec((n_pairs, span, LANES), lambda b, n, r: (0, b * nb + n, 0))
    if not first:
        in_specs += [o_spec, o_spec]
        args += list(prev)
    o_shape = jax.ShapeDtypeStruct((n_pairs, t, LANES), F32)
    outs = pl.pallas_call(
        functools.partial(_attn_kernel, first=first, last=last, dil=dil),
        grid=(batch, nb, dil),
        in_specs=in_specs,
        out_specs=[o_spec] if last else [o_spec, o_spec],
        out_shape=[o_shape] if last else [o_shape, o_shape],
        scratch_shapes=[pltpu.VMEM((dil, 2 * ATTN_BLOCK, ATTN_GW), BF16),
                        pltpu.VMEM((dil, 2 * ATTN_BLOCK, ATTN_GW), BF16)],
        compiler_params=_cparams(3),
        name=f"dilated_attn_g{gi}",
    )(*args)
    return outs[0] if last else tuple(outs)


def _even_out_kernel(attn_ref, u_ref, halo_ref, pw_ref, ps_ref, w_ref, x_ref, gate_ref, o_ref, ue, a_scr, w_bf,
                     *, seq):
    tm = u_ref.shape[0]
    i = pl.program_id(0)

    @pl.when(i == 0)
    def _():
        w_bf[...] = w_ref[...].astype(BF16)

    row0 = (i * tm) % seq
    halo_ok = jnp.where(row0 > 0, 1.0, 0.0)
    ue[0:POOL_HALO, :] = halo_ref[...] * halo_ok
    ue[POOL_HALO:POOL_HALO + tm, :] = u_ref[...]
    for hp in range(ATTN_GW // LANES):
        a_scr[:, hp * LANES:(hp + 1) * LANES] = attn_ref[hp].astype(BF16)
    pos = row0 + lax.broadcasted_iota(jnp.int32, (tm, POOL_GW), 0)
    for gi, win in enumerate(POOL_WINDOWS):
        cols = slice(gi * POOL_GW, (gi + 1) * POOL_GW)
        tok = ue[POOL_HALO:POOL_HALO + tm, cols]
        acc = tok
        for back in range(1, win):
            acc = acc + ue[POOL_HALO - back:POOL_HALO - back + tm, cols]
        cnt = jnp.minimum(pos + 1, win).astype(F32)
        diff = acc / cnt - tok
        y = jnp.dot(diff.astype(BF16), pw_ref[gi].astype(BF16), preferred_element_type=F32) * ps_ref[:, cols]
        a_scr[:, ATTN_GW + gi * POOL_GW:ATTN_GW + (gi + 1) * POOL_GW] = y.astype(BF16)
    y = jnp.dot(a_scr[...], w_bf[...], preferred_element_type=F32)
    o_ref[...] = x_ref[...] + gate_ref[...] * y


def _even_out_proj(attn, u, pool_w, pool_scale, w_out, x, modr, gate_base, seq):
    t, d = x.shape
    tm = 512
    tpb = seq // tm
    k = w_out.shape[0]
    halo_blocks = tm // POOL_HALO
    n_pairs = attn.shape[0]
    return pl.pallas_call(
        functools.partial(_even_out_kernel, seq=seq),
        grid=(t // tm,),
        in_specs=[
            pl.BlockSpec((n_pairs, tm, LANES), lambda i: (0, i, 0)),
            pl.BlockSpec((tm, POOL_W), lambda i: (i, 0)),
            pl.BlockSpec((POOL_HALO, POOL_W), lambda i: (jnp.maximum(i * halo_blocks - 1, 0), 0)),
            pl.BlockSpec(pool_w.shape, lambda i: (0, 0, 0)),
            pl.BlockSpec((1, POOL_W), lambda i: (0, 0)),
            pl.BlockSpec((k, d), lambda i: (0, 0), pipeline_mode=pl.Buffered(1)),
            pl.BlockSpec((tm, d), lambda i: (i, 0)),
            _mod_spec(d, gate_base, tpb),
        ],
        out_specs=pl.BlockSpec((tm, d), lambda i: (i, 0)),
        out_shape=jax.ShapeDtypeStruct((t, d), F32),
        scratch_shapes=[pltpu.VMEM((tm + POOL_HALO, POOL_W), F32), pltpu.VMEM((tm, k), BF16),
                        pltpu.VMEM((k, d), BF16)],
        compiler_params=_cparams(1),
        name="even_out_proj",
    )(attn, u, u, pool_w, pool_scale, w_out, x, modr)


def _ffn_kernel(x_ref, g_ref, sh_ref, sc_ref, gate_ref, w1_ref, w2_ref, fin_ref, o_ref, h_scr, *, final_norm):
    j = pl.program_id(1)

    @pl.when(j == 0)
    def _():
        _norm_mod_to_scratch(x_ref, g_ref, sh_ref, sc_ref, h_scr)
        o_ref[...] = jnp.zeros_like(o_ref)

    a = jnp.dot(h_scr[...], w1_ref[...].astype(BF16), preferred_element_type=F32)
    a = jnp.maximum(a, 0.0)
    o_ref[...] += jnp.dot((a * a).astype(BF16), w2_ref[...].astype(BF16), preferred_element_type=F32)

    @pl.when(j == pl.num_programs(1) - 1)
    def _():
        rows = 256
        gate = gate_ref[...]
        fin = fin_ref[...]

        def body(i, carry):
            r0 = pl.multiple_of(i * rows, rows)
            y = x_ref[pl.ds(r0, rows), :] + gate * o_ref[pl.ds(r0, rows), :]
            if final_norm:
                ms = jnp.mean(y * y, axis=-1, keepdims=True)
                y = y * lax.rsqrt(ms + NORM_EPS) * fin
            o_ref[pl.ds(r0, rows), :] = y
            return carry

        lax.fori_loop(0, x_ref.shape[0] // rows, body, 0)


def _ffn(x, g, modr, mod_base, w1_all, w2_all, layer, fin, seq, final_norm):
    t, d = x.shape
    hdim = w1_all.shape[2]
    tm, th = ROW_TILE, 512
    tpb = seq // tm
    batch = t // seq
    return pl.pallas_call(
        functools.partial(_ffn_kernel, final_norm=final_norm),
        grid=(t // tm, hdim // th),
        in_specs=[
            pl.BlockSpec((tm, d), lambda i, j: (i, 0), pipeline_mode=pl.Buffered(1)),
            pl.BlockSpec((1, d), lambda i, j: (0, 0)),
            _mod_spec(d, mod_base + 3 * batch, tpb),
            _mod_spec(d, mod_base + 4 * batch, tpb),
            _mod_spec(d, mod_base + 5 * batch, tpb),
            pl.BlockSpec((None, d, th), lambda i, j: (layer, 0, j)),
            pl.BlockSpec((None, th, d), lambda i, j: (layer, j, 0)),
            pl.BlockSpec((1, d), lambda i, j: (0, 0)),
        ],
        out_specs=pl.BlockSpec((tm, d), lambda i, j: (i, 0)),
        out_shape=jax.ShapeDtypeStruct((t, d), F32),
        scratch_shapes=[pltpu.VMEM((tm, d), BF16)],
        compiler_params=_cparams(2),
        name="ffn_final" if final_norm else "ffn",
    )(x, g, modr, modr, modr, w1_all, w2_all, fin)


def _ssm_in_kernel(x_ref, g_ref, sh_ref, sc_ref, wt_ref, wdt_ref, zxbc_ref, dt_ref, h_scr):
    contract_last = (((1,), (1,)), ((), ()))

    @pl.when(pl.program_id(1) == 0)
    def _():
        _norm_mod_to_scratch(x_ref, g_ref, sh_ref, sc_ref, h_scr)
        dt = lax.dot_general(h_scr[...], wdt_ref[...].astype(BF16), contract_last, preferred_element_type=F32)
        dt_ref[...] = jnp.concatenate([dt, jnp.zeros((dt.shape[0], LANES - dt.shape[1]), F32)], axis=1)

    zxbc_ref[...] = lax.dot_general(h_scr[...], wt_ref[...].astype(BF16), contract_last,
                                    preferred_element_type=F32).astype(BF16)


def _ssm_in_proj(x, g, modr, mod_base, w_in_t, n_main, seq):
    t, d = x.shape
    tm, tn = ROW_TILE, 1024
    tpb = seq // tm
    batch = t // seq
    n_dt = w_in_t.shape[0] - n_main
    return pl.pallas_call(
        _ssm_in_kernel,
        grid=(t // tm, n_main // tn),
        in_specs=[
            pl.BlockSpec((tm, d), lambda i, j: (i, 0)),
            pl.BlockSpec((1, d), lambda i, j: (0, 0)),
            _mod_spec(d, mod_base + 0 * batch, tpb),
            _mod_spec(d, mod_base + 1 * batch, tpb),
            pl.BlockSpec((tn, d), lambda i, j: (j, 0)),
            pl.BlockSpec((n_dt, d), lambda i, j: (n_main // n_dt, 0)),
        ],
        out_specs=[
            pl.BlockSpec((tm, tn), lambda i, j: (i, j)),
            pl.BlockSpec((tm, LANES), lambda i, j: (i, 0)),
        ],
        out_shape=[
            jax.ShapeDtypeStruct((t, n_main), BF16),
            jax.ShapeDtypeStruct((t, LANES), F32),
        ],
        scratch_shapes=[pltpu.VMEM((tm, d), BF16)],
        compiler_params=_cparams(2),
        name="ssm_in_proj",
    )(x, g, modr, modr, w_in_t, w_in_t)


def _split3_bf16(v):
    hi = v.astype(BF16)
    r1 = v - hi.astype(F32)
    mid = r1.astype(BF16)
    lo = (r1 - mid.astype(F32)).astype(BF16)
    return hi, mid, lo


def _ssd_kernel(z_ref, xr_ref, bcr_ref, dt_ref, cw_ref, cb_ref, dtb_ref, alog_ref, dskip_ref, ng_ref, e_ref, shift_ref,
                y_ref, state, tail, xs_f, xs_b, bc, ex, y_scr):
    q = SSM_CHUNK
    d_inner = xr_ref.shape[1]
    gw = d_inner // SSM_GROUPS
    gs = SSM_STATE
    bc_split = SSM_GROUPS * gs

    @pl.when(pl.program_id(1) == 0)
    def _():
        state[...] = jnp.zeros_like(state)
        tail[...] = jnp.zeros_like(tail)

    halo = tail.shape[0]
    cw_blk = 256
    for c0 in range(0, d_inner + 2 * bc_split, cw_blk):
        cols = slice(c0, c0 + cw_blk)
        if c0 < d_inner:
            cur = xr_ref[:, cols]
        else:
            cur = bcr_ref[:, c0 - d_inner:c0 - d_inner + cw_blk]
        ext = jnp.concatenate([tail[:, cols], cur], axis=0)
        sh = jnp.dot(shift_ref[...], ext, preferred_element_type=F32)
        acc = cb_ref[:, cols] + cw_ref[SSM_CONV - 1:SSM_CONV, cols] * cur.astype(F32)
        for k in range(SSM_CONV - 1):
            acc = acc + cw_ref[k:k + 1, cols] * sh[k * q:(k + 1) * q]
        tail[:, cols] = cur[q - halo:q]
        act = acc * jax.nn.sigmoid(acc)
        if c0 < d_inner:
            xs_f[:, cols] = act
            xs_b[:, cols] = act.astype(BF16)
        else:
            bc[:, c0 - d_inner:c0 - d_inner + cw_blk] = act.astype(BF16)

    v = dt_ref[...] + dtb_ref[...]
    dt = jnp.maximum(v, 0.0) + jnp.log(1.0 + jnp.exp(-jnp.abs(v)))
    a_neg = -jnp.exp(alog_ref[...])
    d_a = dt * a_neg
    row = lax.broadcasted_iota(jnp.int32, (q, q), 0)
    col = lax.broadcasted_iota(jnp.int32, (q, q), 1)
    causal = row >= col
    tril = jnp.where(causal, 1.0, 0.0).astype(BF16)
    a_cum = None
    for part in _split3_bf16(d_a):
        term = jnp.dot(tril, part, preferred_element_type=F32)
        a_cum = term if a_cum is None else a_cum + term
    a2 = a_cum * LOG2E
    b2_t = (a2 - jnp.log2(dt)).T
    a_end = a_cum[q - 1:q, :]
    ea = jnp.exp(a_cum)
    wgt = dt * jnp.exp(a_end - a_cum)
    ea_hi = ea.astype(BF16)
    ea_lo = (ea - ea_hi.astype(F32)).astype(BF16)
    top = jnp.concatenate([ea_hi, ea_lo], axis=1)
    bot = jnp.concatenate([wgt.astype(BF16), jnp.zeros((q, LANES), BF16)], axis=1)
    ex[0:q, :] = jnp.dot(top, e_ref[...], preferred_element_type=F32)
    ex[q:2 * q, :] = jnp.dot(bot, e_ref[...], preferred_element_type=F32)

    lane = lax.broadcasted_iota(jnp.int32, (q, LANES), 1)
    lo_half = lane < SSM_HEAD_DIM
    contract_last = (((1,), (1,)), ((), ()))
    contract_first = (((0,), (0,)), ((), ()))
    heads_per_group = gw // SSM_HEAD_DIM

    for g in range(SSM_GROUPS):
        gcols = slice(g * gw, (g + 1) * gw)
        b_g = bc[:, g * gs:(g + 1) * gs]
        c_g = bc[:, bc_split + g * gs:bc_split + (g + 1) * gs]
        cb = lax.dot_general(c_g, b_g, contract_last, preferred_element_type=F32)
        s_in = state[g]
        y_off = jnp.dot(c_g, s_in.astype(BF16), preferred_element_type=F32)
        for pr in range(heads_per_group // 2):
            pcols = slice(g * gw + pr * LANES, g * gw + (pr + 1) * LANES)
            x_pair = xs_b[:, pcols]
            halves = []
            for half in range(2):
                h = g * heads_per_group + 2 * pr + half
                diff = a2[:, h:h + 1] - b2_t[h:h + 1, :]
                mat = (cb * jnp.exp2(jnp.where(causal, diff, NEG_BIG))).astype(BF16)
                halves.append(jnp.dot(mat, x_pair, preferred_element_type=F32))
            y_diag = jnp.where(lo_half, halves[0], halves[1])
            y_scr[:, pcols] = (y_diag + y_off[:, pr * LANES:(pr + 1) * LANES] * ex[0:q, pcols]
                               + dskip_ref[:, pcols] * xs_f[:, pcols])
        xw = (xs_f[:, gcols] * ex[q:2 * q, gcols]).astype(BF16)
        upd = lax.dot_general(b_g, xw, contract_first, preferred_element_type=F32)
        state[g] = s_in * ex[q - 1:q, gcols] + upd

        zg = z_ref[:, gcols].astype(F32)
        yz = y_scr[:, gcols] * (zg * jax.nn.sigmoid(zg))
        ms = jnp.mean(yz * yz, axis=-1, keepdims=True)
        y_ref[:, gcols] = (yz * lax.rsqrt(ms + NORM_EPS) * ng_ref[:, gcols]).astype(y_ref.dtype)


def _ssd(zxbc, dt_raw, conv_w, conv_b, dt_bias_pad, a_log_pad, d_exp, norm_g, expand, batch, seq):
    t = zxbc.shape[0]
    d_inner = d_exp.shape[1]
    n_bc = zxbc.shape[1] - 2 * d_inner
    n_conv = d_inner + n_bc
    q = SSM_CHUNK
    nc = seq // q
    gw = d_inner // SSM_GROUPS
    row_map = lambda b, c: (b * nc + c, 0)
    const = lambda b, c: (0, 0)
    halo = CONV_TAIL_ROWS
    src = jnp.arange(q)[None, :, None] + (halo - (SSM_CONV - 1)) + jnp.arange(SSM_CONV - 1)[:, None, None]
    shift = (jnp.arange(halo + q)[None, None, :] == src).astype(BF16).reshape((SSM_CONV - 1) * q, halo + q)
    return pl.pallas_call(
        _ssd_kernel,
        grid=(batch, nc),
        in_specs=[
            pl.BlockSpec((q, d_inner), row_map),
            pl.BlockSpec((q, d_inner), lambda b, c: (b * nc + c, 1)),
            pl.BlockSpec((q, n_bc), lambda b, c: (b * nc + c, 2 * d_inner // n_bc)),
            pl.BlockSpec((q, LANES), row_map),
            pl.BlockSpec((SSM_CONV, n_conv), const),
            pl.BlockSpec((1, n_conv), const),
            pl.BlockSpec((1, LANES), const),
            pl.BlockSpec((1, LANES), const),
            pl.BlockSpec((1, d_inner), const),
            pl.BlockSpec((1, d_inner), const),
            pl.BlockSpec((2 * LANES, d_inner), const),
            pl.BlockSpec(shift.shape, const),
        ],
        out_specs=pl.BlockSpec((q, d_inner), row_map),
        out_shape=jax.ShapeDtypeStruct((t, d_inner), BF16),
        scratch_shapes=[
            pltpu.VMEM((SSM_GROUPS, SSM_STATE, gw), F32),
            pltpu.VMEM((halo, n_conv), BF16),
            pltpu.VMEM((q, d_inner), F32),
            pltpu.VMEM((q, d_inner), BF16),
            pltpu.VMEM((q, n_bc), BF16),
            pltpu.VMEM((2 * q, d_inner), F32),
            pltpu.VMEM((q, d_inner), F32),
        ],
        compiler_params=_cparams(2),
        name="ssd_scan",
    )(zxbc, zxbc, zxbc, dt_raw, conv_w, conv_b, dt_bias_pad, a_log_pad, d_exp, norm_g, expand, shift)


def _mm_res_kernel(a_ref, w_ref, x_ref, gate_ref, o_ref):
    y = jnp.dot(a_ref[...], w_ref[...].astype(BF16), preferred_element_type=F32)
    o_ref[...] = x_ref[...] + gate_ref[...] * y


def _matmul_residual(a, w, x, modr, gate_base, seq):
    t, k = a.shape
    d = x.shape[1]
    tm, tn = ROW_TILE, 512
    tpb = seq // tm
    gate_spec = pl.BlockSpec((None, 1, tn), lambda i, j: (gate_base + i // tpb, 0, j))
    return pl.pallas_call(
        _mm_res_kernel,
        grid=(t // tm, d // tn),
        in_specs=[
            pl.BlockSpec((tm, k), lambda i, j: (i, 0)),
            pl.BlockSpec((k, tn), lambda i, j: (0, j)),
            pl.BlockSpec((tm, tn), lambda i, j: (i, j)),
            gate_spec,
        ],
        out_specs=pl.BlockSpec((tm, tn), lambda i, j: (i, j)),
        out_shape=jax.ShapeDtypeStruct((t, d), F32),
        compiler_params=_cparams(2),
        name="ssm_out_proj",
    )(a, w, x, modr)


def kernel(x, c, ada_w, ada_b, norm_mix, norm_ffn, ffn_w1, ffn_w2, even_w_in, pool_w, pool_scale, even_w_out,
           ssm_w_in, ssm_conv_w, ssm_conv_b, ssm_dt_bias, ssm_a_log, ssm_d, ssm_norm, ssm_w_out, final_norm):
    batch, seq, d = x.shape
    depth = ada_w.shape[0]
    t = batch * seq
    xf = x.reshape(t, d)

    c_pad = jnp.pad(c, ((0, 8 - batch), (0, 0)))
    mod = _ada_mod(c_pad, ada_w, ada_b)[:, :batch]
    modr = mod.reshape(depth, batch, 6, d).transpose(0, 2, 1, 3).reshape(depth * 6 * batch, 1, d)

    d_inner = ssm_w_out.shape[1]
    n_heads = ssm_dt_bias.shape[1]
    head_of_channel = jnp.arange(d_inner) // SSM_HEAD_DIM
    expand1 = (jnp.arange(LANES)[:, None] == head_of_channel[None, :]).astype(BF16)
    expand = jnp.concatenate([expand1, expand1], axis=0)

    for i in range(depth):
        base = i * 6 * batch
        j = i // 2
        g_mix = norm_mix[i].reshape(1, d)
        g_ffn = norm_ffn[i].reshape(1, d)
        if i % 2 == 0:
            qkv, u = _even_in_proj(xf, g_mix, modr, base, even_w_in[j], seq)
            merged = None
            for gi in range(N_GROUPS):
                merged = _attn_group(qkv, merged, gi, batch, seq)
            xf = _even_out_proj(merged, u, pool_w[j], pool_scale[j].reshape(1, POOL_W), even_w_out[j], xf, modr,
                                base + 2 * batch, seq)
        else:
            w_in_t = jnp.swapaxes(ssm_w_in[j], 0, 1)
            n_main = w_in_t.shape[0] - n_heads
            pad_h = ((0, 0), (0, LANES - n_heads))
            zxbc, dt_raw = _ssm_in_proj(xf, g_mix, modr, base, w_in_t, n_main, seq)
            y = _ssd(zxbc, dt_raw, ssm_conv_w[j], ssm_conv_b[j].reshape(1, -1),
                     jnp.pad(ssm_dt_bias[j].reshape(1, -1), pad_h),
                     jnp.pad(ssm_a_log[j].reshape(1, -1), pad_h),
                     jnp.repeat(ssm_d[j], SSM_HEAD_DIM).reshape(1, d_inner), ssm_norm[j].reshape(1, d_inner),
                     expand, batch, seq)
            xf = _matmul_residual(y, ssm_w_out[j], xf, modr, base + 2 * batch, seq)
        xf = _ffn(xf, g_ffn, modr, base, ffn_w1, ffn_w2, i, final_norm.reshape(1, d), seq,
                  final_norm=(i == depth - 1))
    return xf.reshape(batch, seq, d)
```

```python
import functools

import jax
import jax.numpy as jnp
from jax import lax
from jax.experimental import pallas as pl
from jax.experimental.pallas import tpu as pltpu

F32 = jnp.float32
BF16 = jnp.bfloat16

NORM_EPS = 1e-6
NEG_BIG = -1e30
LOG2E = 1.4426950408889634

ATTN_GROUPS = ((128, 1), (512, 4), (2048, 16))
ATTN_HEADS = 8
ATTN_HEAD_DIM = 64
ATTN_BLOCK = 128
ATTN_UNITS_PER_STEP = 8
ATTN_GW = ATTN_HEADS * ATTN_HEAD_DIM
N_GROUPS = len(ATTN_GROUPS)
QKV_W = 3 * N_GROUPS * ATTN_GW
POOL_WINDOWS = (2, 4, 8, 16)
POOL_GW = 128
POOL_W = len(POOL_WINDOWS) * POOL_GW
POOL_HALO = 16

SSM_HEAD_DIM = 64
SSM_GROUPS = 8
SSM_STATE = 128
SSM_CONV = 4
SSM_CHUNK = 128
CONV_TAIL_ROWS = 16

VMEM_LIMIT = 56 * 1024 * 1024
LANES = 128
ROW_TILE = 1024


def _cparams(n_axes):
    return pltpu.CompilerParams(dimension_semantics=("arbitrary",) * n_axes, vmem_limit_bytes=VMEM_LIMIT)


def _ada_kernel(c_ref, w_ref, b_ref, o_ref):
    c = c_ref[...]
    cond = (c * jax.nn.sigmoid(c)).astype(BF16)
    w = w_ref[...].astype(BF16)
    o_ref[...] = jnp.dot(cond, w, preferred_element_type=F32) + b_ref[...]


def _ada_mod(c_pad, ada_w, ada_b):
    depth, d, n = ada_w.shape
    rows = c_pad.shape[0]
    tn = 1024
    return pl.pallas_call(
        _ada_kernel,
        grid=(depth, n // tn),
        in_specs=[
            pl.BlockSpec((rows, d), lambda l, j: (0, 0)),
            pl.BlockSpec((None, d, tn), lambda l, j: (l, 0, j)),
            pl.BlockSpec((None, 1, tn), lambda l, j: (l, 0, j)),
        ],
        out_specs=pl.BlockSpec((None, rows, tn), lambda l, j: (l, 0, j)),
        out_shape=jax.ShapeDtypeStruct((depth, rows, n), F32),
        compiler_params=_cparams(2),
        name="ada_mod",
    )(c_pad, ada_w, ada_b.reshape(depth, 1, n))


def _norm_mod_rows(x_ref, g_ref, sh_ref, sc_ref, store, rows_per_step=256):
    tm = x_ref.shape[0]
    gain = g_ref[...] * (1.0 + sc_ref[...])
    shift = sh_ref[...]

    def body(i, carry):
        r0 = pl.multiple_of(i * rows_per_step, rows_per_step)
        x = x_ref[pl.ds(r0, rows_per_step), :]
        ms = jnp.mean(x * x, axis=-1, keepdims=True)
        store(r0, x * lax.rsqrt(ms + NORM_EPS) * gain + shift, rows_per_step)
        return carry

    lax.fori_loop(0, tm // rows_per_step, body, 0)


def _norm_mod_to_scratch(x_ref, g_ref, sh_ref, sc_ref, h_scr):
    def store(r0, h, n):
        h_scr[pl.ds(r0, n), :] = h.astype(BF16)

    _norm_mod_rows(x_ref, g_ref, sh_ref, sc_ref, store)


def _mod_spec(d, idx_base, tiles_per_batch):
    return pl.BlockSpec((None, 1, d), lambda i, *_: (idx_base + i // tiles_per_batch, 0, 0))


def _even_in_kernel(x_ref, g_ref, sh_ref, sc_ref, w_ref, wu_ref, qkv_ref, u_ref, hf, h3):
    j = pl.program_id(1)
    tm = x_ref.shape[0]
    n_slabs = hf.shape[0]

    @pl.when(j == 0)
    def _():
        def store(r0, h, n):
            h3[0, pl.ds(r0, n), :] = h.astype(BF16)
            for c in range(n_slabs):
                hf[c, pl.ds(r0, n), :] = h[:, c * LANES:(c + 1) * LANES]

        _norm_mod_rows(x_ref, g_ref, sh_ref, sc_ref, store)
        u_ref[...] = jnp.dot(h3[0], wu_ref[...].astype(BF16), preferred_element_type=F32)
        for gi, (_, dil) in enumerate(ATTN_GROUPS):
            if dil == 1:
                continue
            rows = tm // dil
            for c in range(n_slabs):
                for r in range(dil):
                    piece = hf[c, pl.ds(r, rows, stride=dil), :]
                    h3[gi, r * rows:(r + 1) * rows, c * LANES:(c + 1) * LANES] = piece.astype(BF16)

    gi = j // N_GROUPS
    qkv_ref[...] = jnp.dot(h3[gi], w_ref[...].astype(BF16), preferred_element_type=F32).astype(BF16)


def _even_in_proj(x, g, modr, mod_base, w_in, seq):
    t, d = x.shape
    tm, tn = ROW_TILE, ATTN_GW
    tpb = seq // tm
    batch = t // seq
    n_u = w_in.shape[1] - QKV_W
    return pl.pallas_call(
        _even_in_kernel,
        grid=(t // tm, QKV_W // tn),
        in_specs=[
            pl.BlockSpec((tm, d), lambda i, j: (i, 0), pipeline_mode=pl.Buffered(1)),
            pl.BlockSpec((1, d), lambda i, j: (0, 0)),
            _mod_spec(d, mod_base + 0 * batch, tpb),
            _mod_spec(d, mod_base + 1 * batch, tpb),
            pl.BlockSpec((d, tn), lambda i, j: (0, (j % N_GROUPS) * N_GROUPS + j // N_GROUPS)),
            pl.BlockSpec((d, n_u), lambda i, j: (0, QKV_W // n_u), pipeline_mode=pl.Buffered(1)),
        ],
        out_specs=[
            pl.BlockSpec((tm, tn), lambda i, j: (i, j)),
            pl.BlockSpec((tm, n_u), lambda i, j: (i, 0)),
        ],
        out_shape=[
            jax.ShapeDtypeStruct((t, QKV_W), BF16),
            jax.ShapeDtypeStruct((t, n_u), F32),
        ],
        scratch_shapes=[pltpu.VMEM((d // LANES, tm, LANES), F32), pltpu.VMEM((N_GROUPS, tm, d), BF16)],
        compiler_params=_cparams(2),
        name="even_in_proj",
    )(x, g, modr, modr, w_in, w_in)


def _unit_rows(ref, u, sl, dil):
    if dil == 1:
        return ref[u * ATTN_BLOCK:(u + 1) * ATTN_BLOCK, sl]
    if len(ref.shape) == 3:
        return ref[u, :, sl]
    return jnp.concatenate([ref[p, u, :, sl] for p in range(ref.shape[0])], axis=0)


def _attn_kernel(*refs, first, last, dil, units):
    if first:
        q_ref, k_ref, v_ref, o_ref, l_ref, kk, vv = refs
        op_ref = lp_ref = None
    elif last:
        q_ref, k_ref, v_ref, op_ref, lp_ref, o_ref, kk, vv = refs
        l_ref = None
    else:
        q_ref, k_ref, v_ref, op_ref, lp_ref, o_ref, l_ref, kk, vv = refs
    step_n = pl.program_id(1)
    step_r = pl.program_id(2)
    blk = ATTN_BLOCK
    full = slice(None)

    @pl.when(step_n == 0)
    def _():
        n_zero = units if dil > 1 else 1
        kk[pl.ds(step_r * n_zero, n_zero), 0:blk, :] = jnp.zeros((n_zero, blk, ATTN_GW), BF16)
        vv[pl.ds(step_r * n_zero, n_zero), 0:blk, :] = jnp.zeros((n_zero, blk, ATTN_GW), BF16)

    qi = lax.broadcasted_iota(jnp.int32, (2 * blk, 2 * blk), 0) & (blk - 1)
    kj = lax.broadcasted_iota(jnp.int32, (2 * blk, 2 * blk), 1)
    cur_bias = jnp.where(kj - blk <= qi, 0.0, NEG_BIG)
    prev_ok = kj >= qi
    lane = lax.broadcasted_iota(jnp.int32, (blk, LANES), 1)
    lo_half = lane < ATTN_HEAD_DIM
    contract_last = (((1,), (1,)), ((), ()))

    for u in range(units):
        if dil > 1:
            r = step_r * units + u
            has_prev = step_n > 0
            rows_out = pl.ds(r, blk, stride=dil)
        else:
            r = 0
            has_prev = (step_n > 0) if u == 0 else True
            rows_out = slice(u * blk, (u + 1) * blk)
        prev_pen = 0.0 if has_prev is True else jnp.where(has_prev, 0.0, NEG_BIG)
        bias = jnp.where(kj < blk, jnp.where(prev_ok, prev_pen, NEG_BIG), cur_bias)
        k_cur = _unit_rows(k_ref, u, full, dil)
        v_cur = _unit_rows(v_ref, u, full, dil)
        kk[r, blk:2 * blk, :] = k_cur
        vv[r, blk:2 * blk, :] = v_cur
        for hp in range(ATTN_GW // LANES):
            sl = slice(hp * LANES, (hp + 1) * LANES)
            qp = _unit_rows(q_ref, u, sl, dil) * jnp.asarray(ATTN_HEAD_DIM ** -0.5, BF16)
            zero = jnp.zeros_like(qp)
            qm = jnp.concatenate([jnp.where(lo_half, qp, zero), jnp.where(lo_half, zero, qp)], axis=0)
            s = lax.dot_general(qm, kk[r, :, sl], contract_last, preferred_element_type=F32) + bias
            m = jnp.max(s, axis=-1, keepdims=True)
            p = jnp.exp(s - m)
            den = jnp.sum(p, axis=-1, keepdims=True)
            acc = jnp.dot(p.astype(BF16), vv[r, :, sl], preferred_element_type=F32)
            o2 = acc / den
            lse2 = jnp.broadcast_to(m + jnp.log(den), (2 * blk, LANES))
            o_new = jnp.where(lo_half, o2[0:blk], o2[blk:2 * blk])
            lse_new = jnp.where(lo_half, lse2[0:blk], lse2[blk:2 * blk])
            if not first:
                lse_old = lp_ref[hp, rows_out, :]
                o_old = op_ref[hp, rows_out, :]
                mx = jnp.maximum(lse_old, lse_new)
                w_old = jnp.exp(lse_old - mx)
                w_new = jnp.exp(lse_new - mx)
                tot = w_old + w_new
                o_new = (o_old * w_old + o_new * w_new) / tot
                lse_new = mx + jnp.log(tot)
            o_ref[hp, rows_out, :] = o_new
            if not last:
                l_ref[hp, rows_out, :] = lse_new
        kk[r, 0:blk, :] = k_cur
        vv[r, 0:blk, :] = v_cur


def _attn_group(qkv, prev, gi, batch, seq):
    dil = ATTN_GROUPS[gi][1]
    first, last = gi == 0, gi == N_GROUPS - 1
    t = batch * seq
    nb = seq // dil // ATTN_BLOCK
    units = ATTN_UNITS_PER_STEP if dil != 4 else 4
    n_pairs = ATTN_GW // LANES
    if dil == 1:
        qkv_v = qkv.reshape(batch, seq, QKV_W)
        grid = (batch, nb // units, 1)
        blk_shape = (None, units * ATTN_BLOCK, ATTN_GW)
        idx = lambda b, n, r, col: (b, n, col)
        o_spec = pl.BlockSpec((n_pairs, units * ATTN_BLOCK, LANES), lambda b, n, r: (0, b * (nb // units) + n, 0))
    else:
        chunk = ROW_TILE // dil
        qkv_v = qkv.reshape(batch, seq // ROW_TILE, dil, chunk, QKV_W)
        grid = (batch, nb, dil // units)
        if chunk >= ATTN_BLOCK:
            sub = chunk // ATTN_BLOCK
            blk_shape = (None, None, units, ATTN_BLOCK, ATTN_GW)
            idx = lambda b, n, r, col: (b, n // sub, r, n % sub, col)
        else:
            blk_shape = (None, ATTN_BLOCK // chunk, units, chunk, ATTN_GW)
            idx = lambda b, n, r, col: (b, n, r, 0, col)
        o_spec = pl.BlockSpec((n_pairs, ATTN_BLOCK * dil, LANES), lambda b, n, r: (0, b * nb + n, 0))
    in_specs = [pl.BlockSpec(blk_shape, functools.partial(lambda b, n, r, col: idx(b, n, r, col), col=gi * 3 + which))
                for which in range(3)]
    args = [qkv_v, qkv_v, qkv_v]
    if not first:
        in_specs += [o_spec, o_spec]
        args += list(prev)
    o_shape = jax.ShapeDtypeStruct((n_pairs, t, LANES), F32)
    outs = pl.pallas_call(
        functools.partial(_attn_kernel, first=first, last=last, dil=dil, units=units),
        grid=grid,
        in_specs=in_specs,
        out_specs=[o_spec] if last else [o_spec, o_spec],
        out_shape=[o_shape] if last else [o_shape, o_shape],
        scratch_shapes=[pltpu.VMEM((dil, 2 * ATTN_BLOCK, ATTN_GW), BF16),
                        pltpu.VMEM((dil, 2 * ATTN_BLOCK, ATTN_GW), BF16)],
        compiler_params=_cparams(3),
        name=f"dilated_attn_g{gi}",
    )(*args)
    return outs[0] if last else tuple(outs)


def _even_out_kernel(attn_ref, u_ref, halo_ref, pw_ref, ps_ref, w_ref, x_ref, gate_ref, o_ref, ue, a_scr, w_bf,
                     *, seq):
    tm = u_ref.shape[0]
    i = pl.program_id(0)

    @pl.when(i == 0)
    def _():
        w_bf[...] = w_ref[...].astype(BF16)

    row0 = (i * tm) % seq
    halo_ok = jnp.where(row0 > 0, 1.0, 0.0)
    ue[0:POOL_HALO, :] = halo_ref[...] * halo_ok
    ue[POOL_HALO:POOL_HALO + tm, :] = u_ref[...]
    for hp in range(ATTN_GW // LANES):
        a_scr[:, hp * LANES:(hp + 1) * LANES] = attn_ref[hp].astype(BF16)
    pos = row0 + lax.broadcasted_iota(jnp.int32, (tm, POOL_GW), 0)
    for gi, win in enumerate(POOL_WINDOWS):
        cols = slice(gi * POOL_GW, (gi + 1) * POOL_GW)
        tok = ue[POOL_HALO:POOL_HALO + tm, cols]
        acc = tok
        for back in range(1, win):
            acc = acc + ue[POOL_HALO - back:POOL_HALO - back + tm, cols]
        cnt = jnp.minimum(pos + 1, win).astype(F32)
        diff = acc / cnt - tok
        y = jnp.dot(diff.astype(BF16), pw_ref[gi].astype(BF16), preferred_element_type=F32) * ps_ref[:, cols]
        a_scr[:, ATTN_GW + gi * POOL_GW:ATTN_GW + (gi + 1) * POOL_GW] = y.astype(BF16)
    y = jnp.dot(a_scr[...], w_bf[...], preferred_element_type=F32)
    o_ref[...] = x_ref[...] + gate_ref[...] * y


def _even_out_proj(attn, u, pool_w, pool_scale, w_out, x, modr, gate_base, seq):
    t, d = x.shape
    tm = 512
    tpb = seq // tm
    k = w_out.shape[0]
    halo_blocks = tm // POOL_HALO
    n_pairs = attn.shape[0]
    return pl.pallas_call(
        functools.partial(_even_out_kernel, seq=seq),
        grid=(t // tm,),
        in_specs=[
            pl.BlockSpec((n_pairs, tm, LANES), lambda i: (0, i, 0)),
            pl.BlockSpec((tm, POOL_W), lambda i: (i, 0)),
            pl.BlockSpec((POOL_HALO, POOL_W), lambda i: (jnp.maximum(i * halo_blocks - 1, 0), 0)),
            pl.BlockSpec(pool_w.shape, lambda i: (0, 0, 0)),
            pl.BlockSpec((1, POOL_W), lambda i: (0, 0)),
            pl.BlockSpec((k, d), lambda i: (0, 0), pipeline_mode=pl.Buffered(1)),
            pl.BlockSpec((tm, d), lambda i: (i, 0)),
            _mod_spec(d, gate_base, tpb),
        ],
        out_specs=pl.BlockSpec((tm, d), lambda i: (i, 0)),
        out_shape=jax.ShapeDtypeStruct((t, d), F32),
        scratch_shapes=[pltpu.VMEM((tm + POOL_HALO, POOL_W), F32), pltpu.VMEM((tm, k), BF16),
                        pltpu.VMEM((k, d), BF16)],
        compiler_params=_cparams(1),
        name="even_out_proj",
    )(attn, u, u, pool_w, pool_scale, w_out, x, modr)


def _ffn_kernel(x_ref, g_ref, sh_ref, sc_ref, gate_ref, w1_ref, w2_ref, fin_ref, o_ref, h_scr, *, final_norm):
    j = pl.program_id(1)

    @pl.when(j == 0)
    def _():
        _norm_mod_to_scratch(x_ref, g_ref, sh_ref, sc_ref, h_scr)
        o_ref[...] = jnp.zeros_like(o_ref)

    a = jnp.dot(h_scr[...], w1_ref[...].astype(BF16), preferred_element_type=F32)
    a = jnp.maximum(a, 0.0)
    o_ref[...] += jnp.dot((a * a).astype(BF16), w2_ref[...].astype(BF16), preferred_element_type=F32)

    @pl.when(j == pl.num_programs(1) - 1)
    def _():
        rows = 256
        gate = gate_ref[...]
        fin = fin_ref[...]

        def body(i, carry):
            r0 = pl.multiple_of(i * rows, rows)
            y = x_ref[pl.ds(r0, rows), :] + gate * o_ref[pl.ds(r0, rows), :]
            if final_norm:
                ms = jnp.mean(y * y, axis=-1, keepdims=True)
                y = y * lax.rsqrt(ms + NORM_EPS) * fin
            o_ref[pl.ds(r0, rows), :] = y
            return carry

        lax.fori_loop(0, x_ref.shape[0] // rows, body, 0)


def _ffn(x, g, modr, mod_base, w1_all, w2_all, layer, fin, seq, final_norm):
    t, d = x.shape
    hdim = w1_all.shape[2]
    tm, th = ROW_TILE, 512
    tpb = seq // tm
    batch = t // seq
    return pl.pallas_call(
        functools.partial(_ffn_kernel, final_norm=final_norm),
        grid=(t // tm, hdim // th),
        in_specs=[
            pl.BlockSpec((tm, d), lambda i, j: (i, 0), pipeline_mode=pl.Buffered(1)),
            pl.BlockSpec((1, d), lambda i, j: (0, 0)),
            _mod_spec(d, mod_base + 3 * batch, tpb),
            _mod_spec(d, mod_base + 4 * batch, tpb),
            _mod_spec(d, mod_base + 5 * batch, tpb),
            pl.BlockSpec((None, d, th), lambda i, j: (layer, 0, j)),
            pl.BlockSpec((None, th, d), lambda i, j: (layer, j, 0)),
            pl.BlockSpec((1, d), lambda i, j: (0, 0)),
        ],
        out_specs=pl.BlockSpec((tm, d), lambda i, j: (i, 0)),
        out_shape=jax.ShapeDtypeStruct((t, d), F32),
        scratch_shapes=[pltpu.VMEM((tm, d), BF16)],
        compiler_params=_cparams(2),
        name="ffn_final" if final_norm else "ffn",
    )(x, g, modr, modr, modr, w1_all, w2_all, fin)


def _ssm_in_kernel(x_ref, g_ref, sh_ref, sc_ref, wt_ref, wdt_ref, zxbc_ref, dt_ref, h_scr):
    contract_last = (((1,), (1,)), ((), ()))

    @pl.when(pl.program_id(1) == 0)
    def _():
        _norm_mod_to_scratch(x_ref, g_ref, sh_ref, sc_ref, h_scr)
        dt = lax.dot_general(h_scr[...], wdt_ref[...].astype(BF16), contract_last, preferred_element_type=F32)
        dt_ref[...] = jnp.concatenate([dt, jnp.zeros((dt.shape[0], LANES - dt.shape[1]), F32)], axis=1)

    zxbc_ref[...] = lax.dot_general(h_scr[...], wt_ref[...].astype(BF16), contract_last,
                                    preferred_element_type=F32).astype(BF16)


def _ssm_in_proj(x, g, modr, mod_base, w_in_t, n_main, seq):
    t, d = x.shape
    tm, tn = ROW_TILE, 1024
    tpb = seq // tm
    batch = t // seq
    n_dt = w_in_t.shape[0] - n_main
    return pl.pallas_call(
        _ssm_in_kernel,
        grid=(t // tm, n_main // tn),
        in_specs=[
            pl.BlockSpec((tm, d), lambda i, j: (i, 0)),
            pl.BlockSpec((1, d), lambda i, j: (0, 0)),
            _mod_spec(d, mod_base + 0 * batch, tpb),
            _mod_spec(d, mod_base + 1 * batch, tpb),
            pl.BlockSpec((tn, d), lambda i, j: (j, 0)),
            pl.BlockSpec((n_dt, d), lambda i, j: (n_main // n_dt, 0)),
        ],
        out_specs=[
            pl.BlockSpec((tm, tn), lambda i, j: (i, j)),
            pl.BlockSpec((tm, LANES), lambda i, j: (i, 0)),
        ],
        out_shape=[
            jax.ShapeDtypeStruct((t, n_main), BF16),
            jax.ShapeDtypeStruct((t, LANES), F32),
        ],
        scratch_shapes=[pltpu.VMEM((tm, d), BF16)],
        compiler_params=_cparams(2),
        name="ssm_in_proj",
    )(x, g, modr, modr, w_in_t, w_in_t)


def _split3_bf16(v):
    hi = v.astype(BF16)
    r1 = v - hi.astype(F32)
    mid = r1.astype(BF16)
    lo = (r1 - mid.astype(F32)).astype(BF16)
    return hi, mid, lo


def _ssd_kernel(z_ref, xr_ref, bcr_ref, dt_ref, cw_ref, cb_ref, dtb_ref, alog_ref, dskip_ref, ng_ref, e_ref, shift_ref,
                y_ref, state, tail, xs_f, xs_b, bc, ex, y_scr):
    q = SSM_CHUNK
    d_inner = xr_ref.shape[1]
    gw = d_inner // SSM_GROUPS
    gs = SSM_STATE
    bc_split = SSM_GROUPS * gs

    @pl.when(pl.program_id(1) == 0)
    def _():
        state[...] = jnp.zeros_like(state)
        tail[...] = jnp.zeros_like(tail)

    halo = tail.shape[0]
    cw_blk = 256
    for c0 in range(0, d_inner + 2 * bc_split, cw_blk):
        cols = slice(c0, c0 + cw_blk)
        if c0 < d_inner:
            cur = xr_ref[:, cols]
        else:
            cur = bcr_ref[:, c0 - d_inner:c0 - d_inner + cw_blk]
        ext = jnp.concatenate([tail[:, cols], cur], axis=0)
        sh = jnp.dot(shift_ref[...], ext, preferred_element_type=F32)
        acc = cb_ref[:, cols] + cw_ref[SSM_CONV - 1:SSM_CONV, cols] * cur.astype(F32)
        for k in range(SSM_CONV - 1):
            acc = acc + cw_ref[k:k + 1, cols] * sh[k * q:(k + 1) * q]
        tail[:, cols] = cur[q - halo:q]
        act = acc * jax.nn.sigmoid(acc)
        if c0 < d_inner:
            xs_f[:, cols] = act
            xs_b[:, cols] = act.astype(BF16)
        else:
            bc[:, c0 - d_inner:c0 - d_inner + cw_blk] = act.astype(BF16)

    v = dt_ref[...] + dtb_ref[...]
    dt = jnp.maximum(v, 0.0) + jnp.log(1.0 + jnp.exp(-jnp.abs(v)))
    a_neg = -jnp.exp(alog_ref[...])
    d_a = dt * a_neg
    row = lax.broadcasted_iota(jnp.int32, (q, q), 0)
    col = lax.broadcasted_iota(jnp.int32, (q, q), 1)
    causal = row >= col
    tril = jnp.where(causal, 1.0, 0.0).astype(BF16)
    a_cum = None
    for part in _split3_bf16(d_a):
        term = jnp.dot(tril, part, preferred_element_type=F32)
        a_cum = term if a_cum is None else a_cum + term
    a2 = a_cum * LOG2E
    b2_t = (a2 - jnp.log2(dt)).T
    a_end = a_cum[q - 1:q, :]
    ea = jnp.exp(a_cum)
    wgt = dt * jnp.exp(a_end - a_cum)
    ea_hi = ea.astype(BF16)
    ea_lo = (ea - ea_hi.astype(F32)).astype(BF16)
    top = jnp.concatenate([ea_hi, ea_lo], axis=1)
    bot = jnp.concatenate([wgt.astype(BF16), jnp.zeros((q, LANES), BF16)], axis=1)
    ex[...] = jnp.dot(jnp.concatenate([top, bot], axis=0), e_ref[...], preferred_element_type=F32)

    lane = lax.broadcasted_iota(jnp.int32, (q, LANES), 1)
    lo_half = lane < SSM_HEAD_DIM
    contract_last = (((1,), (1,)), ((), ()))
    contract_first = (((0,), (0,)), ((), ()))
    heads_per_group = gw // SSM_HEAD_DIM

    def group_body(g):
        gcols = slice(g * gw, (g + 1) * gw)
        b_g = bc[:, g * gs:(g + 1) * gs]
        c_g = bc[:, bc_split + g * gs:bc_split + (g + 1) * gs]
        cb = lax.dot_general(c_g, b_g, contract_last, preferred_element_type=F32)
        s_in = state[g]
        y_off = jnp.dot(c_g, s_in.astype(BF16), preferred_element_type=F32)
        for pr in range(heads_per_group // 2):
            pcols = slice(g * gw + pr * LANES, g * gw + (pr + 1) * LANES)
            x_pair = xs_b[:, pcols]
            h0 = g * heads_per_group + 2 * pr
            halves = []
            for h in (h0, h0 + 1):
                diff = a2[:, h:h + 1] - b2_t[h:h + 1, :]
                mat = (cb * jnp.exp2(jnp.where(causal, diff, NEG_BIG))).astype(BF16)
                halves.append(jnp.dot(mat, x_pair, preferred_element_type=F32))
            y_diag = jnp.where(lo_half, halves[0], halves[1])
            y_scr[:, pcols] = (y_diag + y_off[:, pr * LANES:(pr + 1) * LANES] * ex[0:q, pcols]
                               + dskip_ref[:, pcols] * xs_f[:, pcols])
        xw = (xs_f[:, gcols] * ex[q:2 * q, gcols]).astype(BF16)
        upd = lax.dot_general(b_g, xw, contract_first, preferred_element_type=F32)
        state[g] = s_in * ex[q - 1:q, gcols] + upd

        zg = z_ref[:, gcols].astype(F32)
        yz = y_scr[:, gcols] * (zg * jax.nn.sigmoid(zg))
        ms = jnp.mean(yz * yz, axis=-1, keepdims=True)
        y_ref[:, gcols] = (yz * lax.rsqrt(ms + NORM_EPS) * ng_ref[:, gcols]).astype(y_ref.dtype)

    for g in range(SSM_GROUPS):
        group_body(g)


def _ssd(zxbc, dt_raw, conv_w, conv_b, dt_bias_pad, a_log_pad, d_exp, norm_g, expand, batch, seq):
    t = zxbc.shape[0]
    d_inner = d_exp.shape[1]
    n_bc = zxbc.shape[1] - 2 * d_inner
    n_conv = d_inner + n_bc
    q = SSM_CHUNK
    nc = seq // q
    gw = d_inner // SSM_GROUPS
    row_map = lambda b, c: (b * nc + c, 0)
    const = lambda b, c: (0, 0)
    halo = CONV_TAIL_ROWS
    src = jnp.arange(q)[None, :, None] + (halo - (SSM_CONV - 1)) + jnp.arange(SSM_CONV - 1)[:, None, None]
    shift = (jnp.arange(halo + q)[None, None, :] == src).astype(BF16).reshape((SSM_CONV - 1) * q, halo + q)
    return pl.pallas_call(
        _ssd_kernel,
        grid=(batch, nc),
        in_specs=[
            pl.BlockSpec((q, d_inner), row_map),
            pl.BlockSpec((q, d_inner), lambda b, c: (b * nc + c, 1)),
            pl.BlockSpec((q, n_bc), lambda b, c: (b * nc + c, 2 * d_inner // n_bc)),
            pl.BlockSpec((q, LANES), row_map),
            pl.BlockSpec((SSM_CONV, n_conv), const),
            pl.BlockSpec((1, n_conv), const),
            pl.BlockSpec((1, LANES), const),
            pl.BlockSpec((1, LANES), const),
            pl.BlockSpec((1, d_inner), const),
            pl.BlockSpec((1, d_inner), const),
            pl.BlockSpec((2 * LANES, d_inner), const),
            pl.BlockSpec(shift.shape, const),
        ],
        out_specs=pl.BlockSpec((q, d_inner), row_map),
        out_shape=jax.ShapeDtypeStruct((t, d_inner), BF16),
        scratch_shapes=[
            pltpu.VMEM((SSM_GROUPS, SSM_STATE, gw), F32),
            pltpu.VMEM((halo, n_conv), BF16),
            pltpu.VMEM((q, d_inner), F32),
            pltpu.VMEM((q, d_inner), BF16),
            pltpu.VMEM((q, n_bc), BF16),
            pltpu.VMEM((2 * q, d_inner), F32),
            pltpu.VMEM((q, d_inner), F32),
        ],
        compiler_params=_cparams(2),
        name="ssd_scan",
    )(zxbc, zxbc, zxbc, dt_raw, conv_w, conv_b, dt_bias_pad, a_log_pad, d_exp, norm_g, expand, shift)


def _mm_res_kernel(a_ref, w_ref, x_ref, gate_ref, o_ref):
    y = jnp.dot(a_ref[...], w_ref[...].astype(BF16), preferred_element_type=F32)
    o_ref[...] = x_ref[...] + gate_ref[...] * y


def _matmul_residual(a, w, x, modr, gate_base, seq):
    t, k = a.shape
    d = x.shape[1]
    tm, tn = ROW_TILE, 512
    tpb = seq // tm
    gate_spec = pl.BlockSpec((None, 1, tn), lambda i, j: (gate_base + i // tpb, 0, j))
    return pl.pallas_call(
        _mm_res_kernel,
        grid=(t // tm, d // tn),
        in_specs=[
            pl.BlockSpec((tm, k), lambda i, j: (i, 0)),
            pl.BlockSpec((k, tn), lambda i, j: (0, j)),
            pl.BlockSpec((tm, tn), lambda i, j: (i, j)),
            gate_spec,
        ],
        out_specs=pl.BlockSpec((tm, tn), lambda i, j: (i, j)),
        out_shape=jax.ShapeDtypeStruct((t, d), F32),
        compiler_params=_cparams(2),
        name="ssm_out_proj",
    )(a, w, x, modr)


def kernel(x, c, ada_w, ada_b, norm_mix, norm_ffn, ffn_w1, ffn_w2, even_w_in, pool_w, pool_scale, even_w_out,
           ssm_w_in, ssm_conv_w, ssm_conv_b, ssm_dt_bias, ssm_a_log, ssm_d, ssm_norm, ssm_w_out, final_norm):
    batch, seq, d = x.shape
    depth = ada_w.shape[0]
    t = batch * seq
    xf = x.reshape(t, d)

    c_pad = jnp.pad(c, ((0, 8 - batch), (0, 0)))
    mod = _ada_mod(c_pad, ada_w, ada_b)[:, :batch]
    modr = mod.reshape(depth, batch, 6, d).transpose(0, 2, 1, 3).reshape(depth * 6 * batch, 1, d)

    d_inner = ssm_w_out.shape[1]
    n_heads = ssm_dt_bias.shape[1]
    head_of_channel = jnp.arange(d_inner) // SSM_HEAD_DIM
    expand1 = (jnp.arange(LANES)[:, None] == head_of_channel[None, :]).astype(BF16)
    expand = jnp.concatenate([expand1, expand1], axis=0)

    for i in range(depth):
        base = i * 6 * batch
        j = i // 2
        g_mix = norm_mix[i].reshape(1, d)
        g_ffn = norm_ffn[i].reshape(1, d)
        if i % 2 == 0:
            qkv, u = _even_in_proj(xf, g_mix, modr, base, even_w_in[j], seq)
            merged = None
            for gi in range(N_GROUPS):
                merged = _attn_group(qkv, merged, gi, batch, seq)
            xf = _even_out_proj(merged, u, pool_w[j], pool_scale[j].reshape(1, POOL_W), even_w_out[j], xf, modr,
                                base + 2 * batch, seq)
        else:
            w_in_t = jnp.swapaxes(ssm_w_in[j], 0, 1)
            n_main = w_in_t.shape[0] - n_heads
            pad_h = ((0, 0), (0, LANES - n_heads))
            zxbc, dt_raw = _ssm_in_proj(xf, g_mix, modr, base, w_in_t, n_main, seq)
            y = _ssd(zxbc, dt_raw, ssm_conv_w[j], ssm_conv_b[j].reshape(1, -1),
                     jnp.pad(ssm_dt_bias[j].reshape(1, -1), pad_h),
                     jnp.pad(ssm_a_log[j].reshape(1, -1), pad_h),
                     jnp.repeat(ssm_d[j], SSM_HEAD_DIM).reshape(1, d_inner), ssm_norm[j].reshape(1, d_inner),
                     expand, batch, seq)
            xf = _matmul_residual(y, ssm_w_out[j], xf, modr, base + 2 * batch, seq)
        xf = _ffn(xf, g_ffn, modr, base, ffn_w1, ffn_w2, i, final_norm.reshape(1, d), seq,
                  final_norm=(i == depth - 1))
    return xf.reshape(batch, seq, d)
```

```python
import functools

import jax
import jax.numpy as jnp
from jax import lax
from jax.experimental import pallas as pl
from jax.experimental.pallas import tpu as pltpu

F32 = jnp.float32
BF16 = jnp.bfloat16

NORM_EPS = 1e-6
NEG_BIG = -1e30
LOG2E = 1.4426950408889634

ATTN_GROUPS = ((128, 1), (512, 4), (2048, 16))
ATTN_HEADS = 8
ATTN_HEAD_DIM = 64
ATTN_BLOCK = 128
ATTN_UNITS_PER_STEP = 8
ATTN_GW = ATTN_HEADS * ATTN_HEAD_DIM
N_GROUPS = len(ATTN_GROUPS)
QKV_W = 3 * N_GROUPS * ATTN_GW
POOL_WINDOWS = (2, 4, 8, 16)
POOL_GW = 128
POOL_W = len(POOL_WINDOWS) * POOL_GW
POOL_HALO = 16

SSM_HEAD_DIM = 64
SSM_GROUPS = 8
SSM_STATE = 128
SSM_CONV = 4
SSM_CHUNK = 128
CONV_TAIL_ROWS = 16

VMEM_LIMIT = 56 * 1024 * 1024
LANES = 128
ROW_TILE = 1024


def _cparams(n_axes):
    return pltpu.CompilerParams(dimension_semantics=("arbitrary",) * n_axes, vmem_limit_bytes=VMEM_LIMIT)


def _ada_kernel(c_ref, w_ref, b_ref, o_ref):
    c = c_ref[...]
    cond = (c * jax.nn.sigmoid(c)).astype(BF16)
    w = w_ref[...].astype(BF16)
    o_ref[...] = jnp.dot(cond, w, preferred_element_type=F32) + b_ref[...]


def _ada_mod(c_pad, ada_w, ada_b):
    depth, d, n = ada_w.shape
    rows = c_pad.shape[0]
    tn = 1024
    return pl.pallas_call(
        _ada_kernel,
        grid=(depth, n // tn),
        in_specs=[
            pl.BlockSpec((rows, d), lambda l, j: (0, 0)),
            pl.BlockSpec((None, d, tn), lambda l, j: (l, 0, j)),
            pl.BlockSpec((None, 1, tn), lambda l, j: (l, 0, j)),
        ],
        out_specs=pl.BlockSpec((None, rows, tn), lambda l, j: (l, 0, j)),
        out_shape=jax.ShapeDtypeStruct((depth, rows, n), F32),
        compiler_params=_cparams(2),
        name="ada_mod",
    )(c_pad, ada_w, ada_b.reshape(depth, 1, n))


def _norm_mod_rows(x_ref, g_ref, sh_ref, sc_ref, store, rows_per_step=256):
    tm = x_ref.shape[0]
    gain = g_ref[...] * (1.0 + sc_ref[...])
    shift = sh_ref[...]

    def body(i, carry):
        r0 = pl.multiple_of(i * rows_per_step, rows_per_step)
        x = x_ref[pl.ds(r0, rows_per_step), :]
        ms = jnp.mean(x * x, axis=-1, keepdims=True)
        store(r0, x * lax.rsqrt(ms + NORM_EPS) * gain + shift, rows_per_step)
        return carry

    lax.fori_loop(0, tm // rows_per_step, body, 0)


def _norm_mod_to_scratch(x_ref, g_ref, sh_ref, sc_ref, h_scr):
    def store(r0, h, n):
        h_scr[pl.ds(r0, n), :] = h.astype(BF16)

    _norm_mod_rows(x_ref, g_ref, sh_ref, sc_ref, store)


def _norm_mod_to_scratch_unrolled(x_ref, g_ref, sh_ref, sc_ref, h_scr, rows_per_step=64):
    gain = g_ref[...] * (1.0 + sc_ref[...])
    shift = sh_ref[...]
    for r0 in range(0, x_ref.shape[0], rows_per_step):
        x = x_ref[r0:r0 + rows_per_step, :]
        ms = jnp.mean(x * x, axis=-1, keepdims=True)
        h_scr[r0:r0 + rows_per_step, :] = (x * lax.rsqrt(ms + NORM_EPS) * gain + shift).astype(BF16)


def _mod_spec(d, idx_base, tiles_per_batch):
    return pl.BlockSpec((None, 1, d), lambda i, *_: (idx_base + i // tiles_per_batch, 0, 0))


def _even_in_kernel(x_ref, g_ref, sh_ref, sc_ref, w_ref, wu_ref, qkv_ref, u_ref, hf, h3):
    j = pl.program_id(1)
    tm = x_ref.shape[0]
    n_slabs = hf.shape[0]

    @pl.when(j == 0)
    def _():
        def store(r0, h, n):
            h3[0, pl.ds(r0, n), :] = h.astype(BF16)
            for c in range(n_slabs):
                hf[c, pl.ds(r0, n), :] = h[:, c * LANES:(c + 1) * LANES]

        _norm_mod_rows(x_ref, g_ref, sh_ref, sc_ref, store)
        u_ref[...] = jnp.dot(h3[0], wu_ref[...].astype(BF16), preferred_element_type=F32)
        qkv_ref[...] = jnp.dot(h3[0], w_ref[...].astype(BF16), preferred_element_type=F32).astype(BF16)
        for gi, (_, dil) in enumerate(ATTN_GROUPS):
            if dil == 1:
                continue
            rows = tm // dil
            for c in range(n_slabs):
                for r in range(dil):
                    piece = hf[c, pl.ds(r, rows, stride=dil), :]
                    h3[gi, r * rows:(r + 1) * rows, c * LANES:(c + 1) * LANES] = piece.astype(BF16)

    @pl.when(j > 0)
    def _():
        gi = j // N_GROUPS
        qkv_ref[...] = jnp.dot(h3[gi], w_ref[...].astype(BF16), preferred_element_type=F32).astype(BF16)


def _even_in_proj(x, g, modr, mod_base, w_in, seq):
    t, d = x.shape
    tm, tn = ROW_TILE, ATTN_GW
    tpb = seq // tm
    batch = t // seq
    n_u = w_in.shape[1] - QKV_W
    return pl.pallas_call(
        _even_in_kernel,
        grid=(t // tm, QKV_W // tn),
        in_specs=[
            pl.BlockSpec((tm, d), lambda i, j: (i, 0), pipeline_mode=pl.Buffered(1)),
            pl.BlockSpec((1, d), lambda i, j: (0, 0)),
            _mod_spec(d, mod_base + 0 * batch, tpb),
            _mod_spec(d, mod_base + 1 * batch, tpb),
            pl.BlockSpec((d, tn), lambda i, j: (0, (j % N_GROUPS) * N_GROUPS + j // N_GROUPS)),
            pl.BlockSpec((d, n_u), lambda i, j: (0, QKV_W // n_u), pipeline_mode=pl.Buffered(1)),
        ],
        out_specs=[
            pl.BlockSpec((tm, tn), lambda i, j: (i, j)),
            pl.BlockSpec((tm, n_u), lambda i, j: (i, 0)),
        ],
        out_shape=[
            jax.ShapeDtypeStruct((t, QKV_W), BF16),
            jax.ShapeDtypeStruct((t, n_u), F32),
        ],
        scratch_shapes=[pltpu.VMEM((d // LANES, tm, LANES), F32), pltpu.VMEM((N_GROUPS, tm, d), BF16)],
        compiler_params=_cparams(2),
        name="even_in_proj",
    )(x, g, modr, modr, w_in, w_in)


def _unit_rows(ref, u, sl, dil):
    if dil == 1:
        return ref[u * ATTN_BLOCK:(u + 1) * ATTN_BLOCK, sl]
    if len(ref.shape) == 3:
        return ref[u, :, sl]
    return jnp.concatenate([ref[p, u, :, sl] for p in range(ref.shape[0])], axis=0)


def _attn_kernel(*refs, first, last, dil, units):
    if first:
        q_ref, k_ref, v_ref, o_ref, l_ref, kk, vv = refs
        op_ref = lp_ref = None
    elif last:
        q_ref, k_ref, v_ref, op_ref, lp_ref, o_ref, kk, vv = refs
        l_ref = None
    else:
        q_ref, k_ref, v_ref, op_ref, lp_ref, o_ref, l_ref, kk, vv = refs
    step_n = pl.program_id(1)
    step_r = pl.program_id(2)
    blk = ATTN_BLOCK
    full = slice(None)

    @pl.when(step_n == 0)
    def _():
        n_zero = units if dil > 1 else 1
        kk[pl.ds(step_r * n_zero, n_zero), 0:blk, :] = jnp.zeros((n_zero, blk, ATTN_GW), BF16)
        vv[pl.ds(step_r * n_zero, n_zero), 0:blk, :] = jnp.zeros((n_zero, blk, ATTN_GW), BF16)

    qi = lax.broadcasted_iota(jnp.int32, (2 * blk, 2 * blk), 0) & (blk - 1)
    kj = lax.broadcasted_iota(jnp.int32, (2 * blk, 2 * blk), 1)
    cur_bias = jnp.where(kj - blk <= qi, 0.0, NEG_BIG)
    prev_ok = kj >= qi
    lane = lax.broadcasted_iota(jnp.int32, (blk, LANES), 1)
    lo_half = lane < ATTN_HEAD_DIM
    contract_last = (((1,), (1,)), ((), ()))

    for u in range(units):
        if dil > 1:
            r = step_r * units + u
            has_prev = step_n > 0
            rows_out = pl.ds(r, blk, stride=dil)
        else:
            r = 0
            has_prev = (step_n > 0) if u == 0 else True
            rows_out = slice(u * blk, (u + 1) * blk)
        prev_pen = 0.0 if has_prev is True else jnp.where(has_prev, 0.0, NEG_BIG)
        bias = jnp.where(kj < blk, jnp.where(prev_ok, prev_pen, NEG_BIG), cur_bias)
        k_cur = _unit_rows(k_ref, u, full, dil)
        v_cur = _unit_rows(v_ref, u, full, dil)
        kk[r, blk:2 * blk, :] = k_cur
        vv[r, blk:2 * blk, :] = v_cur
        for hp in range(ATTN_GW // LANES):
            sl = slice(hp * LANES, (hp + 1) * LANES)
            qp = _unit_rows(q_ref, u, sl, dil) * jnp.asarray(ATTN_HEAD_DIM ** -0.5, BF16)
            zero = jnp.zeros_like(qp)
            qm = jnp.concatenate([jnp.where(lo_half, qp, zero), jnp.where(lo_half, zero, qp)], axis=0)
            s = lax.dot_general(qm, kk[r, :, sl], contract_last, preferred_element_type=F32) + bias
            m = jnp.max(s, axis=-1, keepdims=True)
            p = jnp.exp(s - m)
            den = jnp.sum(p, axis=-1, keepdims=True)
            acc = jnp.dot(p.astype(BF16), vv[r, :, sl], preferred_element_type=F32)
            o2 = acc / den
            lse2 = jnp.broadcast_to(m + jnp.log(den), (2 * blk, LANES))
            o_new = jnp.where(lo_half, o2[0:blk], o2[blk:2 * blk])
            lse_new = jnp.where(lo_half, lse2[0:blk], lse2[blk:2 * blk])
            if not first:
                lse_old = lp_ref[hp, rows_out, :]
                o_old = op_ref[hp, rows_out, :]
                mx = jnp.maximum(lse_old, lse_new)
                w_old = jnp.exp(lse_old - mx)
                w_new = jnp.exp(lse_new - mx)
                tot = w_old + w_new
                o_new = (o_old * w_old + o_new * w_new) / tot
                lse_new = mx + jnp.log(tot)
            o_ref[hp, rows_out, :] = o_new
            if not last:
                l_ref[hp, rows_out, :] = lse_new
        kk[r, 0:blk, :] = k_cur
        vv[r, 0:blk, :] = v_cur


def _attn_group(qkv, prev, gi, batch, seq):
    dil = ATTN_GROUPS[gi][1]
    first, last = gi == 0, gi == N_GROUPS - 1
    t = batch * seq
    nb = seq // dil // ATTN_BLOCK
    units = ATTN_UNITS_PER_STEP if dil != 4 else 4
    n_pairs = ATTN_GW // LANES
    if dil == 1:
        qkv_v = qkv.reshape(batch, seq, QKV_W)
        grid = (batch, nb // units, 1)
        blk_shape = (None, units * ATTN_BLOCK, ATTN_GW)
        idx = lambda b, n, r, col: (b, n, col)
        o_spec = pl.BlockSpec((n_pairs, units * ATTN_BLOCK, LANES), lambda b, n, r: (0, b * (nb // units) + n, 0))
    else:
        chunk = ROW_TILE // dil
        qkv_v = qkv.reshape(batch, seq // ROW_TILE, dil, chunk, QKV_W)
        grid = (batch, nb, dil // units)
        if chunk >= ATTN_BLOCK:
            sub = chunk // ATTN_BLOCK
            blk_shape = (None, None, units, ATTN_BLOCK, ATTN_GW)
            idx = lambda b, n, r, col: (b, n // sub, r, n % sub, col)
        else:
            blk_shape = (None, ATTN_BLOCK // chunk, units, chunk, ATTN_GW)
            idx = lambda b, n, r, col: (b, n, r, 0, col)
        o_spec = pl.BlockSpec((n_pairs, ATTN_BLOCK * dil, LANES), lambda b, n, r: (0, b * nb + n, 0))
    in_specs = [pl.BlockSpec(blk_shape, functools.partial(lambda b, n, r, col: idx(b, n, r, col), col=gi * 3 + which))
                for which in range(3)]
    args = [qkv_v, qkv_v, qkv_v]
    if not first:
        in_specs += [o_spec, o_spec]
        args += list(prev)
    o_shape = jax.ShapeDtypeStruct((n_pairs, t, LANES), F32)
    outs = pl.pallas_call(
        functools.partial(_attn_kernel, first=first, last=last, dil=dil, units=units),
        grid=grid,
        in_specs=in_specs,
        out_specs=[o_spec] if last else [o_spec, o_spec],
        out_shape=[o_shape] if last else [o_shape, o_shape],
        scratch_shapes=[pltpu.VMEM((dil, 2 * ATTN_BLOCK, ATTN_GW), BF16),
                        pltpu.VMEM((dil, 2 * ATTN_BLOCK, ATTN_GW), BF16)],
        compiler_params=_cparams(3),
        name=f"dilated_attn_g{gi}",
    )(*args)
    return outs[0] if last else tuple(outs)


def _even_out_kernel(attn_ref, u_ref, halo_ref, pw_ref, ps_ref, w_ref, x_ref, gate_ref, o_ref, ue, a_scr, w_bf,
                     *, seq):
    tm = u_ref.shape[0]
    i = pl.program_id(0)

    @pl.when(i == 0)
    def _():
        w_bf[...] = w_ref[...].astype(BF16)

    row0 = (i * tm) % seq
    halo_ok = jnp.where(row0 > 0, 1.0, 0.0)
    ue[0:POOL_HALO, :] = halo_ref[...] * halo_ok
    ue[POOL_HALO:POOL_HALO + tm, :] = u_ref[...]
    for hp in range(ATTN_GW // LANES):
        a_scr[:, hp * LANES:(hp + 1) * LANES] = attn_ref[hp].astype(BF16)
    pos = row0 + lax.broadcasted_iota(jnp.int32, (tm, POOL_GW), 0)
    for gi, win in enumerate(POOL_WINDOWS):
        cols = slice(gi * POOL_GW, (gi + 1) * POOL_GW)
        tok = ue[POOL_HALO:POOL_HALO + tm, cols]
        acc = tok
        for back in range(1, win):
            acc = acc + ue[POOL_HALO - back:POOL_HALO - back + tm, cols]
        cnt = jnp.minimum(pos + 1, win).astype(F32)
        diff = acc / cnt - tok
        y = jnp.dot(diff.astype(BF16), pw_ref[gi].astype(BF16), preferred_element_type=F32) * ps_ref[:, cols]
        a_scr[:, ATTN_GW + gi * POOL_GW:ATTN_GW + (gi + 1) * POOL_GW] = y.astype(BF16)
    y = jnp.dot(a_scr[...], w_bf[...], preferred_element_type=F32)
    o_ref[...] = x_ref[...] + gate_ref[...] * y


def _even_out_proj(attn, u, pool_w, pool_scale, w_out, x, modr, gate_base, seq):
    t, d = x.shape
    tm = 512
    tpb = seq // tm
    k = w_out.shape[0]
    halo_blocks = tm // POOL_HALO
    n_pairs = attn.shape[0]
    return pl.pallas_call(
        functools.partial(_even_out_kernel, seq=seq),
        grid=(t // tm,),
        in_specs=[
            pl.BlockSpec((n_pairs, tm, LANES), lambda i: (0, i, 0)),
            pl.BlockSpec((tm, POOL_W), lambda i: (i, 0)),
            pl.BlockSpec((POOL_HALO, POOL_W), lambda i: (jnp.maximum(i * halo_blocks - 1, 0), 0)),
            pl.BlockSpec(pool_w.shape, lambda i: (0, 0, 0)),
            pl.BlockSpec((1, POOL_W), lambda i: (0, 0)),
            pl.BlockSpec((k, d), lambda i: (0, 0), pipeline_mode=pl.Buffered(1)),
            pl.BlockSpec((tm, d), lambda i: (i, 0)),
            _mod_spec(d, gate_base, tpb),
        ],
        out_specs=pl.BlockSpec((tm, d), lambda i: (i, 0)),
        out_shape=jax.ShapeDtypeStruct((t, d), F32),
        scratch_shapes=[pltpu.VMEM((tm + POOL_HALO, POOL_W), F32), pltpu.VMEM((tm, k), BF16),
                        pltpu.VMEM((k, d), BF16)],
        compiler_params=_cparams(1),
        name="even_out_proj",
    )(attn, u, u, pool_w, pool_scale, w_out, x, modr)


def _ffn_kernel(x_ref, g_ref, sh_ref, sc_ref, gate_ref, w1_ref, w2_ref, fin_ref, o_ref, h_scr, *, final_norm):
    j = pl.program_id(1)

    def hidden_tile():
        a = jnp.dot(h_scr[...], w1_ref[...].astype(BF16), preferred_element_type=F32)
        a = jnp.maximum(a, 0.0)
        return jnp.dot((a * a).astype(BF16), w2_ref[...].astype(BF16), preferred_element_type=F32)

    @pl.when(j == 0)
    def _():
        _norm_mod_to_scratch_unrolled(x_ref, g_ref, sh_ref, sc_ref, h_scr)
        o_ref[...] = hidden_tile()

    @pl.when(j > 0)
    def _():
        o_ref[...] += hidden_tile()

    @pl.when(j == pl.num_programs(1) - 1)
    def _():
        rows = 256
        gate = gate_ref[...]
        fin = fin_ref[...]

        def body(i, carry):
            r0 = pl.multiple_of(i * rows, rows)
            y = x_ref[pl.ds(r0, rows), :] + gate * o_ref[pl.ds(r0, rows), :]
            if final_norm:
                ms = jnp.mean(y * y, axis=-1, keepdims=True)
                y = y * lax.rsqrt(ms + NORM_EPS) * fin
            o_ref[pl.ds(r0, rows), :] = y
            return carry

        lax.fori_loop(0, x_ref.shape[0] // rows, body, 0)


def _ffn(x, g, modr, mod_base, w1_all, w2_all, layer, fin, seq, final_norm):
    t, d = x.shape
    hdim = w1_all.shape[2]
    tm, th = ROW_TILE, 512
    tpb = seq // tm
    batch = t // seq
    return pl.pallas_call(
        functools.partial(_ffn_kernel, final_norm=final_norm),
        grid=(t // tm, hdim // th),
        in_specs=[
            pl.BlockSpec((tm, d), lambda i, j: (i, 0), pipeline_mode=pl.Buffered(1)),
            pl.BlockSpec((1, d), lambda i, j: (0, 0)),
            _mod_spec(d, mod_base + 3 * batch, tpb),
            _mod_spec(d, mod_base + 4 * batch, tpb),
            _mod_spec(d, mod_base + 5 * batch, tpb),
            pl.BlockSpec((None, d, th), lambda i, j: (layer, 0, j)),
            pl.BlockSpec((None, th, d), lambda i, j: (layer, j, 0)),
            pl.BlockSpec((1, d), lambda i, j: (0, 0)),
        ],
        out_specs=pl.BlockSpec((tm, d), lambda i, j: (i, 0)),
        out_shape=jax.ShapeDtypeStruct((t, d), F32),
        scratch_shapes=[pltpu.VMEM((tm, d), BF16)],
        compiler_params=_cparams(2),
        name="ffn_final" if final_norm else "ffn",
    )(x, g, modr, modr, modr, w1_all, w2_all, fin)


def _ssm_in_kernel(x_ref, g_ref, sh_ref, sc_ref, wt_ref, wdt_ref, zxbc_ref, dt_ref, h_scr):
    contract_last = (((1,), (1,)), ((), ()))

    def project():
        zxbc_ref[...] = lax.dot_general(h_scr[...], wt_ref[...].astype(BF16), contract_last,
                                        preferred_element_type=F32).astype(BF16)

    @pl.when(pl.program_id(1) == 0)
    def _():
        _norm_mod_to_scratch_unrolled(x_ref, g_ref, sh_ref, sc_ref, h_scr)
        project()
        dt = lax.dot_general(h_scr[...], wdt_ref[...].astype(BF16), contract_last, preferred_element_type=F32)
        dt_ref[...] = jnp.concatenate([dt, jnp.zeros((dt.shape[0], LANES - dt.shape[1]), F32)], axis=1)

    @pl.when(pl.program_id(1) > 0)
    def _():
        project()


def _ssm_in_proj(x, g, modr, mod_base, w_in_t, n_main, seq):
    t, d = x.shape
    tm, tn = ROW_TILE, 1024
    tpb = seq // tm
    batch = t // seq
    n_dt = w_in_t.shape[0] - n_main
    return pl.pallas_call(
        _ssm_in_kernel,
        grid=(t // tm, n_main // tn),
        in_specs=[
            pl.BlockSpec((tm, d), lambda i, j: (i, 0)),
            pl.BlockSpec((1, d), lambda i, j: (0, 0)),
            _mod_spec(d, mod_base + 0 * batch, tpb),
            _mod_spec(d, mod_base + 1 * batch, tpb),
            pl.BlockSpec((tn, d), lambda i, j: (j, 0)),
            pl.BlockSpec((n_dt, d), lambda i, j: (n_main // n_dt, 0)),
        ],
        out_specs=[
            pl.BlockSpec((tm, tn), lambda i, j: (i, j)),
            pl.BlockSpec((tm, LANES), lambda i, j: (i, 0)),
        ],
        out_shape=[
            jax.ShapeDtypeStruct((t, n_main), BF16),
            jax.ShapeDtypeStruct((t, LANES), F32),
        ],
        scratch_shapes=[pltpu.VMEM((tm, d), BF16)],
        compiler_params=_cparams(2),
        name="ssm_in_proj",
    )(x, g, modr, modr, w_in_t, w_in_t)


def _split3_bf16(v):
    hi = v.astype(BF16)
    r1 = v - hi.astype(F32)
    mid = r1.astype(BF16)
    lo = (r1 - mid.astype(F32)).astype(BF16)
    return hi, mid, lo


def _ssd_kernel(z_ref, xr_ref, bcr_ref, dt_ref, cw_ref, cb_ref, dtb_ref, alog_ref, dskip_ref, ng_ref, e_ref, shift_ref,
                y_ref, state, tail, xs_f, xs_b, bc, ex, y_scr):
    q = SSM_CHUNK
    d_inner = xr_ref.shape[1]
    gw = d_inner // SSM_GROUPS
    gs = SSM_STATE
    bc_split = SSM_GROUPS * gs

    @pl.when(pl.program_id(1) == 0)
    def _():
        state[...] = jnp.zeros_like(state)
        tail[...] = jnp.zeros_like(tail)

    halo = tail.shape[0]
    cw_blk = 256
    for c0 in range(0, d_inner + 2 * bc_split, cw_blk):
        cols = slice(c0, c0 + cw_blk)
        if c0 < d_inner:
            cur = xr_ref[:, cols]
        else:
            cur = bcr_ref[:, c0 - d_inner:c0 - d_inner + cw_blk]
        ext = jnp.concatenate([tail[:, cols], cur], axis=0)
        sh = jnp.dot(shift_ref[...], ext, preferred_element_type=F32)
        acc = cb_ref[:, cols] + cw_ref[SSM_CONV - 1:SSM_CONV, cols] * cur.astype(F32)
        for k in range(SSM_CONV - 1):
            acc = acc + cw_ref[k:k + 1, cols] * sh[k * q:(k + 1) * q]
        tail[:, cols] = cur[q - halo:q]
        act = acc * jax.nn.sigmoid(acc)
        if c0 < d_inner:
            xs_f[:, cols] = act
            xs_b[:, cols] = act.astype(BF16)
        else:
            bc[:, c0 - d_inner:c0 - d_inner + cw_blk] = act.astype(BF16)

    v = dt_ref[...] + dtb_ref[...]
    dt = jnp.maximum(v, 0.0) + jnp.log(1.0 + jnp.exp(-jnp.abs(v)))
    a_neg = -jnp.exp(alog_ref[...])
    d_a = dt * a_neg
    row = lax.broadcasted_iota(jnp.int32, (q, q), 0)
    col = lax.broadcasted_iota(jnp.int32, (q, q), 1)
    causal = row >= col
    tril = jnp.where(causal, 1.0, 0.0).astype(BF16)
    a_cum = None
    for part in _split3_bf16(d_a):
        term = jnp.dot(tril, part, preferred_element_type=F32)
        a_cum = term if a_cum is None else a_cum + term
    a2 = a_cum * LOG2E
    b2_t = (a2 - jnp.log2(dt)).T
    a_end = a_cum[q - 1:q, :]
    ea = jnp.exp(a_cum)
    wgt = dt * jnp.exp(a_end - a_cum)
    ea_hi = ea.astype(BF16)
    ea_lo = (ea - ea_hi.astype(F32)).astype(BF16)
    top = jnp.concatenate([ea_hi, ea_lo], axis=1)
    bot = jnp.concatenate([wgt.astype(BF16), jnp.zeros((q, LANES), BF16)], axis=1)
    ex[...] = jnp.dot(jnp.concatenate([top, bot], axis=0), e_ref[...], preferred_element_type=F32)

    lane = lax.broadcasted_iota(jnp.int32, (q, LANES), 1)
    lo_half = lane < SSM_HEAD_DIM
    contract_last = (((1,), (1,)), ((), ()))
    contract_first = (((0,), (0,)), ((), ()))
    heads_per_group = gw // SSM_HEAD_DIM

    def group_body(g):
        gcols = slice(g * gw, (g + 1) * gw)
        b_g = bc[:, g * gs:(g + 1) * gs]
        c_g = bc[:, bc_split + g * gs:bc_split + (g + 1) * gs]
        cb = lax.dot_general(c_g, b_g, contract_last, preferred_element_type=F32)
        s_in = state[g]
        y_off = jnp.dot(c_g, s_in.astype(BF16), preferred_element_type=F32)
        for pr in range(heads_per_group // 2):
            pcols = slice(g * gw + pr * LANES, g * gw + (pr + 1) * LANES)
            x_pair = xs_b[:, pcols]
            h0 = g * heads_per_group + 2 * pr
            halves = []
            for h in (h0, h0 + 1):
                diff = a2[:, h:h + 1] - b2_t[h:h + 1, :]
                mat = (cb * jnp.exp2(jnp.where(causal, diff, NEG_BIG))).astype(BF16)
                halves.append(jnp.dot(mat, x_pair, preferred_element_type=F32))
            y_diag = jnp.where(lo_half, halves[0], halves[1])
            y_scr[:, pcols] = (y_diag + y_off[:, pr * LANES:(pr + 1) * LANES] * ex[0:q, pcols]
                               + dskip_ref[:, pcols] * xs_f[:, pcols])
        xw = (xs_f[:, gcols] * ex[q:2 * q, gcols]).astype(BF16)
        upd = lax.dot_general(b_g, xw, contract_first, preferred_element_type=F32)
        state[g] = s_in * ex[q - 1:q, gcols] + upd

        zg = z_ref[:, gcols].astype(F32)
        yz = y_scr[:, gcols] * (zg * jax.nn.sigmoid(zg))
        ms = jnp.mean(yz * yz, axis=-1, keepdims=True)
        y_ref[:, gcols] = (yz * lax.rsqrt(ms + NORM_EPS) * ng_ref[:, gcols]).astype(y_ref.dtype)

    for g in range(SSM_GROUPS):
        group_body(g)


def _ssd(zxbc, dt_raw, conv_w, conv_b, dt_bias_pad, a_log_pad, d_exp, norm_g, expand, batch, seq):
    t = zxbc.shape[0]
    d_inner = d_exp.shape[1]
    n_bc = zxbc.shape[1] - 2 * d_inner
    n_conv = d_inner + n_bc
    q = SSM_CHUNK
    nc = seq // q
    gw = d_inner // SSM_GROUPS
    row_map = lambda b, c: (b * nc + c, 0)
    const = lambda b, c: (0, 0)
    halo = CONV_TAIL_ROWS
    src = jnp.arange(q)[None, :, None] + (halo - (SSM_CONV - 1)) + jnp.arange(SSM_CONV - 1)[:, None, None]
    shift = (jnp.arange(halo + q)[None, None, :] == src).astype(BF16).reshape((SSM_CONV - 1) * q, halo + q)
    return pl.pallas_call(
        _ssd_kernel,
        grid=(batch, nc),
        in_specs=[
            pl.BlockSpec((q, d_inner), row_map),
            pl.BlockSpec((q, d_inner), lambda b, c: (b * nc + c, 1)),
            pl.BlockSpec((q, n_bc), lambda b, c: (b * nc + c, 2 * d_inner // n_bc)),
            pl.BlockSpec((q, LANES), row_map),
            pl.BlockSpec((SSM_CONV, n_conv), const),
            pl.BlockSpec((1, n_conv), const),
            pl.BlockSpec((1, LANES), const),
            pl.BlockSpec((1, LANES), const),
            pl.BlockSpec((1, d_inner), const),
            pl.BlockSpec((1, d_inner), const),
            pl.BlockSpec((2 * LANES, d_inner), const),
            pl.BlockSpec(shift.shape, const),
        ],
        out_specs=pl.BlockSpec((q, d_inner), row_map),
        out_shape=jax.ShapeDtypeStruct((t, d_inner), BF16),
        scratch_shapes=[
            pltpu.VMEM((SSM_GROUPS, SSM_STATE, gw), F32),
            pltpu.VMEM((halo, n_conv), BF16),
            pltpu.VMEM((q, d_inner), F32),
            pltpu.VMEM((q, d_inner), BF16),
            pltpu.VMEM((q, n_bc), BF16),
            pltpu.VMEM((2 * q, d_inner), F32),
            pltpu.VMEM((q, d_inner), F32),
        ],
        compiler_params=_cparams(2),
        name="ssd_scan",
    )(zxbc, zxbc, zxbc, dt_raw, conv_w, conv_b, dt_bias_pad, a_log_pad, d_exp, norm_g, expand, shift)


def _mm_res_kernel(a_ref, w_ref, x_ref, gate_ref, o_ref):
    y = jnp.dot(a_ref[...], w_ref[...].astype(BF16), preferred_element_type=F32)
    o_ref[...] = x_ref[...] + gate_ref[...] * y


def _matmul_residual(a, w, x, modr, gate_base, seq):
    t, k = a.shape
    d = x.shape[1]
    tm, tn = ROW_TILE, 512
    tpb = seq // tm
    gate_spec = pl.BlockSpec((None, 1, tn), lambda i, j: (gate_base + i // tpb, 0, j))
    return pl.pallas_call(
        _mm_res_kernel,
        grid=(t // tm, d // tn),
        in_specs=[
            pl.BlockSpec((tm, k), lambda i, j: (i, 0)),
            pl.BlockSpec((k, tn), lambda i, j: (0, j)),
            pl.BlockSpec((tm, tn), lambda i, j: (i, j)),
            gate_spec,
        ],
        out_specs=pl.BlockSpec((tm, tn), lambda i, j: (i, j)),
        out_shape=jax.ShapeDtypeStruct((t, d), F32),
        compiler_params=_cparams(2),
        name="ssm_out_proj",
    )(a, w, x, modr)


def kernel(x, c, ada_w, ada_b, norm_mix, norm_ffn, ffn_w1, ffn_w2, even_w_in, pool_w, pool_scale, even_w_out,
           ssm_w_in, ssm_conv_w, ssm_conv_b, ssm_dt_bias, ssm_a_log, ssm_d, ssm_norm, ssm_w_out, final_norm):
    batch, seq, d = x.shape
    depth = ada_w.shape[0]
    t = batch * seq
    xf = x.reshape(t, d)

    c_pad = jnp.pad(c, ((0, 8 - batch), (0, 0)))
    mod = _ada_mod(c_pad, ada_w, ada_b)[:, :batch]
    modr = mod.reshape(depth, batch, 6, d).transpose(0, 2, 1, 3).reshape(depth * 6 * batch, 1, d)

    d_inner = ssm_w_out.shape[1]
    n_heads = ssm_dt_bias.shape[1]
    head_of_channel = jnp.arange(d_inner) // SSM_HEAD_DIM
    expand1 = (jnp.arange(LANES)[:, None] == head_of_channel[None, :]).astype(BF16)
    expand = jnp.concatenate([expand1, expand1], axis=0)

    for i in range(depth):
        base = i * 6 * batch
        j = i // 2
        g_mix = norm_mix[i].reshape(1, d)
        g_ffn = norm_ffn[i].reshape(1, d)
        if i % 2 == 0:
            qkv, u = _even_in_proj(xf, g_mix, modr, base, even_w_in[j], seq)
            merged = None
            for gi in range(N_GROUPS):
                merged = _attn_group(qkv, merged, gi, batch, seq)
            xf = _even_out_proj(merged, u, pool_w[j], pool_scale[j].reshape(1, POOL_W), even_w_out[j], xf, modr,
                                base + 2 * batch, seq)
        else:
            w_in_t = jnp.swapaxes(ssm_w_in[j], 0, 1)
            n_main = w_in_t.shape[0] - n_heads
            pad_h = ((0, 0), (0, LANES - n_heads))
            zxbc, dt_raw = _ssm_in_proj(xf, g_mix, modr, base, w_in_t, n_main, seq)
            y = _ssd(zxbc, dt_raw, ssm_conv_w[j], ssm_conv_b[j].reshape(1, -1),
                     jnp.pad(ssm_dt_bias[j].reshape(1, -1), pad_h),
                     jnp.pad(ssm_a_log[j].reshape(1, -1), pad_h),
                     jnp.repeat(ssm_d[j], SSM_HEAD_DIM).reshape(1, d_inner), ssm_norm[j].reshape(1, d_inner),
                     expand, batch, seq)
            xf = _matmul_residual(y, ssm_w_out[j], xf, modr, base + 2 * batch, seq)
        xf = _ffn(xf, g_ffn, modr, base, ffn_w1, ffn_w2, i, final_norm.reshape(1, d), seq,
                  final_norm=(i == depth - 1))
    return xf.reshape(batch, seq, d)
```

```python
import functools

import jax
import jax.numpy as jnp
from jax import lax
from jax.experimental import pallas as pl
from jax.experimental.pallas import tpu as pltpu

F32 = jnp.float32
BF16 = jnp.bfloat16

NORM_EPS = 1e-6
NEG_BIG = -1e30
LOG2E = 1.4426950408889634

ATTN_GROUPS = ((128, 1), (512, 4), (2048, 16))
ATTN_HEADS = 8
ATTN_HEAD_DIM = 64
ATTN_BLOCK = 128
ATTN_UNITS_PER_STEP = 8
ATTN_GW = ATTN_HEADS * ATTN_HEAD_DIM
N_GROUPS = len(ATTN_GROUPS)
QKV_W = 3 * N_GROUPS * ATTN_GW
POOL_WINDOWS = (2, 4, 8, 16)
POOL_GW = 128
POOL_W = len(POOL_WINDOWS) * POOL_GW
POOL_HALO = 16

SSM_HEAD_DIM = 64
SSM_GROUPS = 8
SSM_STATE = 128
SSM_CONV = 4
SSM_CHUNK = 128
SSD_CHUNKS_PER_STEP = 2
CONV_TAIL_ROWS = 16

VMEM_LIMIT = 56 * 1024 * 1024
BIG_TILE_VMEM_LIMIT = 60 * 1024 * 1024
LANES = 128
ROW_TILE = 1024


def _cparams(n_axes, vmem_limit=VMEM_LIMIT):
    return pltpu.CompilerParams(dimension_semantics=("arbitrary",) * n_axes, vmem_limit_bytes=vmem_limit)


def _ada_kernel(c_ref, w_ref, b_ref, o_ref):
    c = c_ref[...]
    cond = (c * jax.nn.sigmoid(c)).astype(BF16)
    w = w_ref[...].astype(BF16)
    o_ref[...] = jnp.dot(cond, w, preferred_element_type=F32) + b_ref[...]


def _ada_mod(c_pad, ada_w, ada_b):
    depth, d, n = ada_w.shape
    rows = c_pad.shape[0]
    tn = 1024
    return pl.pallas_call(
        _ada_kernel,
        grid=(depth, n // tn),
        in_specs=[
            pl.BlockSpec((rows, d), lambda l, j: (0, 0)),
            pl.BlockSpec((None, d, tn), lambda l, j: (l, 0, j)),
            pl.BlockSpec((None, 1, tn), lambda l, j: (l, 0, j)),
        ],
        out_specs=pl.BlockSpec((None, rows, tn), lambda l, j: (l, 0, j)),
        out_shape=jax.ShapeDtypeStruct((depth, rows, n), F32),
        compiler_params=_cparams(2),
        name="ada_mod",
    )(c_pad, ada_w, ada_b.reshape(depth, 1, n))


def _norm_mod_rows(x_ref, g_ref, sh_ref, sc_ref, store, rows_per_step=256):
    tm = x_ref.shape[0]
    gain = g_ref[...] * (1.0 + sc_ref[...])
    shift = sh_ref[...]

    def body(i, carry):
        r0 = pl.multiple_of(i * rows_per_step, rows_per_step)
        x = x_ref[pl.ds(r0, rows_per_step), :]
        ms = jnp.mean(x * x, axis=-1, keepdims=True)
        store(r0, x * lax.rsqrt(ms + NORM_EPS) * gain + shift, rows_per_step)
        return carry

    lax.fori_loop(0, tm // rows_per_step, body, 0)


def _norm_mod_to_scratch_unrolled(x_ref, g_ref, sh_ref, sc_ref, h_scr, rows_per_step=64):
    gain = g_ref[...] * (1.0 + sc_ref[...])
    shift = sh_ref[...]
    for r0 in range(0, x_ref.shape[0], rows_per_step):
        x = x_ref[r0:r0 + rows_per_step, :]
        ms = jnp.mean(x * x, axis=-1, keepdims=True)
        h_scr[r0:r0 + rows_per_step, :] = (x * lax.rsqrt(ms + NORM_EPS) * gain + shift).astype(BF16)


def _mod_spec(d, idx_base, tiles_per_batch):
    return pl.BlockSpec((None, 1, d), lambda i, *_: (idx_base + i // tiles_per_batch, 0, 0))


def _even_in_kernel(x_ref, g_ref, sh_ref, sc_ref, w_ref, wu_ref, qkv_ref, u_ref, hf, h3):
    j = pl.program_id(1)
    tm = x_ref.shape[0]
    n_slabs = hf.shape[0]

    @pl.when(j == 0)
    def _():
        def store(r0, h, n):
            h3[0, pl.ds(r0, n), :] = h.astype(BF16)
            for c in range(n_slabs):
                hf[c, pl.ds(r0, n), :] = h[:, c * LANES:(c + 1) * LANES]

        _norm_mod_rows(x_ref, g_ref, sh_ref, sc_ref, store)
        u_ref[...] = jnp.dot(h3[0], wu_ref[...].astype(BF16), preferred_element_type=F32)
        qkv_ref[...] = jnp.dot(h3[0], w_ref[...].astype(BF16), preferred_element_type=F32).astype(BF16)
        for gi, (_, dil) in enumerate(ATTN_GROUPS):
            if dil == 1:
                continue
            rows = tm // dil
            for c in range(n_slabs):
                for r in range(dil):
                    piece = hf[c, pl.ds(r, rows, stride=dil), :]
                    h3[gi, r * rows:(r + 1) * rows, c * LANES:(c + 1) * LANES] = piece.astype(BF16)

    @pl.when(j > 0)
    def _():
        gi = j // N_GROUPS
        qkv_ref[...] = jnp.dot(h3[gi], w_ref[...].astype(BF16), preferred_element_type=F32).astype(BF16)


def _even_in_proj(x, g, modr, mod_base, w_in, seq):
    t, d = x.shape
    tm, tn = ROW_TILE, ATTN_GW
    tpb = seq // tm
    batch = t // seq
    n_u = w_in.shape[1] - QKV_W
    return pl.pallas_call(
        _even_in_kernel,
        grid=(t // tm, QKV_W // tn),
        in_specs=[
            pl.BlockSpec((tm, d), lambda i, j: (i, 0)),
            pl.BlockSpec((1, d), lambda i, j: (0, 0)),
            _mod_spec(d, mod_base + 0 * batch, tpb),
            _mod_spec(d, mod_base + 1 * batch, tpb),
            pl.BlockSpec((d, tn), lambda i, j: (0, (j % N_GROUPS) * N_GROUPS + j // N_GROUPS)),
            pl.BlockSpec((d, n_u), lambda i, j: (0, QKV_W // n_u), pipeline_mode=pl.Buffered(1)),
        ],
        out_specs=[
            pl.BlockSpec((tm, tn), lambda i, j: (i, j)),
            pl.BlockSpec((tm, n_u), lambda i, j: (i, 0)),
        ],
        out_shape=[
            jax.ShapeDtypeStruct((t, QKV_W), BF16),
            jax.ShapeDtypeStruct((t, n_u), F32),
        ],
        scratch_shapes=[pltpu.VMEM((d // LANES, tm, LANES), F32), pltpu.VMEM((N_GROUPS, tm, d), BF16)],
        compiler_params=_cparams(2, BIG_TILE_VMEM_LIMIT),
        name="even_in_proj",
    )(x, g, modr, modr, w_in, w_in)


def _unit_rows(ref, u, sl, dil):
    if dil == 1:
        return ref[u * ATTN_BLOCK:(u + 1) * ATTN_BLOCK, sl]
    if len(ref.shape) == 3:
        return ref[u, :, sl]
    return jnp.concatenate([ref[p, u, :, sl] for p in range(ref.shape[0])], axis=0)


def _attn_kernel(*refs, first, last, dil, units):
    if first:
        q_ref, k_ref, v_ref, o_ref, l_ref, kk, vv = refs
        op_ref = lp_ref = None
    elif last:
        q_ref, k_ref, v_ref, op_ref, lp_ref, o_ref, kk, vv = refs
        l_ref = None
    else:
        q_ref, k_ref, v_ref, op_ref, lp_ref, o_ref, l_ref, kk, vv = refs
    step_n = pl.program_id(1)
    step_r = pl.program_id(2)
    blk = ATTN_BLOCK
    full = slice(None)

    @pl.when(step_n == 0)
    def _():
        n_zero = units if dil > 1 else 1
        kk[pl.ds(step_r * n_zero, n_zero), 0:blk, :] = jnp.zeros((n_zero, blk, ATTN_GW), BF16)
        vv[pl.ds(step_r * n_zero, n_zero), 0:blk, :] = jnp.zeros((n_zero, blk, ATTN_GW), BF16)

    qi = lax.broadcasted_iota(jnp.int32, (2 * blk, 2 * blk), 0) & (blk - 1)
    kj = lax.broadcasted_iota(jnp.int32, (2 * blk, 2 * blk), 1)
    cur_bias = jnp.where(kj - blk <= qi, 0.0, NEG_BIG)
    prev_ok = kj >= qi
    lane = lax.broadcasted_iota(jnp.int32, (blk, LANES), 1)
    lo_half = lane < ATTN_HEAD_DIM
    contract_last = (((1,), (1,)), ((), ()))

    for u in range(units):
        if dil > 1:
            r = step_r * units + u
            has_prev = step_n > 0
            rows_out = pl.ds(r, blk, stride=dil)
        else:
            r = 0
            has_prev = (step_n > 0) if u == 0 else True
            rows_out = slice(u * blk, (u + 1) * blk)
        prev_pen = 0.0 if has_prev is True else jnp.where(has_prev, 0.0, NEG_BIG)
        bias = jnp.where(kj < blk, jnp.where(prev_ok, prev_pen, NEG_BIG), cur_bias)
        k_cur = _unit_rows(k_ref, u, full, dil)
        v_cur = _unit_rows(v_ref, u, full, dil)
        kk[r, blk:2 * blk, :] = k_cur
        vv[r, blk:2 * blk, :] = v_cur
        for hp in range(ATTN_GW // LANES):
            sl = slice(hp * LANES, (hp + 1) * LANES)
            qp = _unit_rows(q_ref, u, sl, dil) * jnp.asarray(ATTN_HEAD_DIM ** -0.5, BF16)
            zero = jnp.zeros_like(qp)
            qm = jnp.concatenate([jnp.where(lo_half, qp, zero), jnp.where(lo_half, zero, qp)], axis=0)
            s = lax.dot_general(qm, kk[r, :, sl], contract_last, preferred_element_type=F32) + bias
            m = jnp.max(s, axis=-1, keepdims=True)
            p = jnp.exp(s - m)
            den = jnp.sum(p, axis=-1, keepdims=True)
            acc = jnp.dot(p.astype(BF16), vv[r, :, sl], preferred_element_type=F32)
            o2 = acc / den
            lse2 = jnp.broadcast_to(m + jnp.log(den), (2 * blk, LANES))
            o_new = jnp.where(lo_half, o2[0:blk], o2[blk:2 * blk])
            lse_new = jnp.where(lo_half, lse2[0:blk], lse2[blk:2 * blk])
            if not first:
                lse_old = lp_ref[hp, rows_out, :]
                o_old = op_ref[hp, rows_out, :]
                mx = jnp.maximum(lse_old, lse_new)
                w_old = jnp.exp(lse_old - mx)
                w_new = jnp.exp(lse_new - mx)
                tot = w_old + w_new
                o_new = (o_old * w_old + o_new * w_new) / tot
                lse_new = mx + jnp.log(tot)
            o_ref[hp, rows_out, :] = o_new
            if not last:
                l_ref[hp, rows_out, :] = lse_new
        kk[r, 0:blk, :] = k_cur
        vv[r, 0:blk, :] = v_cur


def _attn_group(qkv, prev, gi, batch, seq):
    dil = ATTN_GROUPS[gi][1]
    first, last = gi == 0, gi == N_GROUPS - 1
    t = batch * seq
    nb = seq // dil // ATTN_BLOCK
    units = ATTN_UNITS_PER_STEP if dil != 4 else 4
    n_pairs = ATTN_GW // LANES
    if dil == 1:
        qkv_v = qkv.reshape(batch, seq, QKV_W)
        grid = (batch, nb // units, 1)
        blk_shape = (None, units * ATTN_BLOCK, ATTN_GW)
        idx = lambda b, n, r, col: (b, n, col)
        o_spec = pl.BlockSpec((n_pairs, units * ATTN_BLOCK, LANES), lambda b, n, r: (0, b * (nb // units) + n, 0))
    else:
        chunk = ROW_TILE // dil
        qkv_v = qkv.reshape(batch, seq // ROW_TILE, dil, chunk, QKV_W)
        grid = (batch, nb, dil // units)
        if chunk >= ATTN_BLOCK:
            sub = chunk // ATTN_BLOCK
            blk_shape = (None, None, units, ATTN_BLOCK, ATTN_GW)
            idx = lambda b, n, r, col: (b, n // sub, r, n % sub, col)
        else:
            blk_shape = (None, ATTN_BLOCK // chunk, units, chunk, ATTN_GW)
            idx = lambda b, n, r, col: (b, n, r, 0, col)
        o_spec = pl.BlockSpec((n_pairs, ATTN_BLOCK * dil, LANES), lambda b, n, r: (0, b * nb + n, 0))
    in_specs = [pl.BlockSpec(blk_shape, functools.partial(lambda b, n, r, col: idx(b, n, r, col), col=gi * 3 + which))
                for which in range(3)]
    args = [qkv_v, qkv_v, qkv_v]
    if not first:
        in_specs += [o_spec, o_spec]
        args += list(prev)
    o_shape = jax.ShapeDtypeStruct((n_pairs, t, LANES), F32)
    outs = pl.pallas_call(
        functools.partial(_attn_kernel, first=first, last=last, dil=dil, units=units),
        grid=grid,
        in_specs=in_specs,
        out_specs=[o_spec] if last else [o_spec, o_spec],
        out_shape=[o_shape] if last else [o_shape, o_shape],
        scratch_shapes=[pltpu.VMEM((dil, 2 * ATTN_BLOCK, ATTN_GW), BF16),
                        pltpu.VMEM((dil, 2 * ATTN_BLOCK, ATTN_GW), BF16)],
        compiler_params=_cparams(3),
        name=f"dilated_attn_g{gi}",
    )(*args)
    return outs[0] if last else tuple(outs)


def _even_out_kernel(attn_ref, u_ref, halo_ref, pw_ref, ps_ref, w_ref, x_ref, gate_ref, o_ref, ue, a_scr, w_bf,
                     *, seq):
    tm = u_ref.shape[0]
    i = pl.program_id(0)

    @pl.when(i == 0)
    def _():
        w_bf[...] = w_ref[...].astype(BF16)

    row0 = (i * tm) % seq
    halo_ok = jnp.where(row0 > 0, 1.0, 0.0)
    ue[0:POOL_HALO, :] = halo_ref[...] * halo_ok
    ue[POOL_HALO:POOL_HALO + tm, :] = u_ref[...]
    for hp in range(ATTN_GW // LANES):
        a_scr[:, hp * LANES:(hp + 1) * LANES] = attn_ref[hp].astype(BF16)
    pos = row0 + lax.broadcasted_iota(jnp.int32, (tm, POOL_GW), 0)
    for gi, win in enumerate(POOL_WINDOWS):
        cols = slice(gi * POOL_GW, (gi + 1) * POOL_GW)
        tok = ue[POOL_HALO:POOL_HALO + tm, cols]
        acc = tok
        for back in range(1, win):
            acc = acc + ue[POOL_HALO - back:POOL_HALO - back + tm, cols]
        cnt = jnp.minimum(pos + 1, win).astype(F32)
        diff = acc / cnt - tok
        y = jnp.dot(diff.astype(BF16), pw_ref[gi].astype(BF16), preferred_element_type=F32) * ps_ref[:, cols]
        a_scr[:, ATTN_GW + gi * POOL_GW:ATTN_GW + (gi + 1) * POOL_GW] = y.astype(BF16)
    y = jnp.dot(a_scr[...], w_bf[...], preferred_element_type=F32)
    o_ref[...] = x_ref[...] + gate_ref[...] * y


def _even_out_proj(attn, u, pool_w, pool_scale, w_out, x, modr, gate_base, seq):
    t, d = x.shape
    tm = 512
    tpb = seq // tm
    k = w_out.shape[0]
    halo_blocks = tm // POOL_HALO
    n_pairs = attn.shape[0]
    return pl.pallas_call(
        functools.partial(_even_out_kernel, seq=seq),
        grid=(t // tm,),
        in_specs=[
            pl.BlockSpec((n_pairs, tm, LANES), lambda i: (0, i, 0)),
            pl.BlockSpec((tm, POOL_W), lambda i: (i, 0)),
            pl.BlockSpec((POOL_HALO, POOL_W), lambda i: (jnp.maximum(i * halo_blocks - 1, 0), 0)),
            pl.BlockSpec(pool_w.shape, lambda i: (0, 0, 0)),
            pl.BlockSpec((1, POOL_W), lambda i: (0, 0)),
            pl.BlockSpec((k, d), lambda i: (0, 0), pipeline_mode=pl.Buffered(1)),
            pl.BlockSpec((tm, d), lambda i: (i, 0)),
            _mod_spec(d, gate_base, tpb),
        ],
        out_specs=pl.BlockSpec((tm, d), lambda i: (i, 0)),
        out_shape=jax.ShapeDtypeStruct((t, d), F32),
        scratch_shapes=[pltpu.VMEM((tm + POOL_HALO, POOL_W), F32), pltpu.VMEM((tm, k), BF16),
                        pltpu.VMEM((k, d), BF16)],
        compiler_params=_cparams(1),
        name="even_out_proj",
    )(attn, u, u, pool_w, pool_scale, w_out, x, modr)


def _ffn_kernel(x_ref, g_ref, sh_ref, sc_ref, gate_ref, w1_ref, w2_ref, fin_ref, o_ref, h_scr, *, final_norm):
    j = pl.program_id(1)

    def hidden_tile():
        a = jnp.dot(h_scr[...], w1_ref[...].astype(BF16), preferred_element_type=F32)
        a = jnp.maximum(a, 0.0)
        return jnp.dot((a * a).astype(BF16), w2_ref[...].astype(BF16), preferred_element_type=F32)

    @pl.when(j == 0)
    def _():
        _norm_mod_to_scratch_unrolled(x_ref, g_ref, sh_ref, sc_ref, h_scr)
        o_ref[...] = hidden_tile()

    @pl.when(j > 0)
    def _():
        o_ref[...] += hidden_tile()

    @pl.when(j == pl.num_programs(1) - 1)
    def _():
        rows = 256
        gate = gate_ref[...]
        fin = fin_ref[...]

        def body(i, carry):
            r0 = pl.multiple_of(i * rows, rows)
            y = x_ref[pl.ds(r0, rows), :] + gate * o_ref[pl.ds(r0, rows), :]
            if final_norm:
                ms = jnp.mean(y * y, axis=-1, keepdims=True)
                y = y * lax.rsqrt(ms + NORM_EPS) * fin
            o_ref[pl.ds(r0, rows), :] = y
            return carry

        lax.fori_loop(0, x_ref.shape[0] // rows, body, 0)


def _ffn(x, g, modr, mod_base, w1_all, w2_all, layer, fin, seq, final_norm):
    t, d = x.shape
    hdim = w1_all.shape[2]
    tm, th = ROW_TILE, 512
    tpb = seq // tm
    batch = t // seq
    return pl.pallas_call(
        functools.partial(_ffn_kernel, final_norm=final_norm),
        grid=(t // tm, hdim // th),
        in_specs=[
            pl.BlockSpec((tm, d), lambda i, j: (i, 0)),
            pl.BlockSpec((1, d), lambda i, j: (0, 0)),
            _mod_spec(d, mod_base + 3 * batch, tpb),
            _mod_spec(d, mod_base + 4 * batch, tpb),
            _mod_spec(d, mod_base + 5 * batch, tpb),
            pl.BlockSpec((None, d, th), lambda i, j: (layer, 0, j)),
            pl.BlockSpec((None, th, d), lambda i, j: (layer, j, 0)),
            pl.BlockSpec((1, d), lambda i, j: (0, 0)),
        ],
        out_specs=pl.BlockSpec((tm, d), lambda i, j: (i, 0)),
        out_shape=jax.ShapeDtypeStruct((t, d), F32),
        scratch_shapes=[pltpu.VMEM((tm, d), BF16)],
        compiler_params=_cparams(2, BIG_TILE_VMEM_LIMIT),
        name="ffn_final" if final_norm else "ffn",
    )(x, g, modr, modr, modr, w1_all, w2_all, fin)


def _ssm_in_kernel(x_ref, g_ref, sh_ref, sc_ref, wt_ref, wdt_ref, zxbc_ref, dt_ref, h_scr):
    contract_last = (((1,), (1,)), ((), ()))

    def project():
        zxbc_ref[...] = lax.dot_general(h_scr[...], wt_ref[...].astype(BF16), contract_last,
                                        preferred_element_type=F32).astype(BF16)

    @pl.when(pl.program_id(1) == 0)
    def _():
        _norm_mod_to_scratch_unrolled(x_ref, g_ref, sh_ref, sc_ref, h_scr)
        project()
        dt = lax.dot_general(h_scr[...], wdt_ref[...].astype(BF16), contract_last, preferred_element_type=F32)
        dt_ref[...] = jnp.concatenate([dt, jnp.zeros((dt.shape[0], LANES - dt.shape[1]), F32)], axis=1)

    @pl.when(pl.program_id(1) > 0)
    def _():
        project()


def _ssm_in_proj(x, g, modr, mod_base, w_in_t, n_main, seq):
    t, d = x.shape
    tm, tn = ROW_TILE, 1024
    tpb = seq // tm
    batch = t // seq
    n_dt = w_in_t.shape[0] - n_main
    return pl.pallas_call(
        _ssm_in_kernel,
        grid=(t // tm, n_main // tn),
        in_specs=[
            pl.BlockSpec((tm, d), lambda i, j: (i, 0)),
            pl.BlockSpec((1, d), lambda i, j: (0, 0)),
            _mod_spec(d, mod_base + 0 * batch, tpb),
            _mod_spec(d, mod_base + 1 * batch, tpb),
            pl.BlockSpec((tn, d), lambda i, j: (j, 0)),
            pl.BlockSpec((n_dt, d), lambda i, j: (n_main // n_dt, 0)),
        ],
        out_specs=[
            pl.BlockSpec((tm, tn), lambda i, j: (i, j)),
            pl.BlockSpec((tm, LANES), lambda i, j: (i, 0)),
        ],
        out_shape=[
            jax.ShapeDtypeStruct((t, n_main), BF16),
            jax.ShapeDtypeStruct((t, LANES), F32),
        ],
        scratch_shapes=[pltpu.VMEM((tm, d), BF16)],
        compiler_params=_cparams(2),
        name="ssm_in_proj",
    )(x, g, modr, modr, w_in_t, w_in_t)


def _split3_bf16(v):
    hi = v.astype(BF16)
    r1 = v - hi.astype(F32)
    mid = r1.astype(BF16)
    lo = (r1 - mid.astype(F32)).astype(BF16)
    return hi, mid, lo


def _ssd_kernel(z_ref, xr_ref, bcr_ref, dt_ref, cw_ref, cb_ref, dtb_ref, alog_ref, dskip_ref, ng_ref, e_ref, shift_ref,
                y_ref, state, tail, xs_f, xs_b, bc, ex, y_scr):
    q = SSM_CHUNK
    d_inner = xr_ref.shape[1]
    gw = d_inner // SSM_GROUPS
    gs = SSM_STATE
    bc_split = SSM_GROUPS * gs

    @pl.when(pl.program_id(1) == 0)
    def _():
        state[...] = jnp.zeros_like(state)
        tail[...] = jnp.zeros_like(tail)

    for sub in range(xs_f.shape[0]):
        _ssd_chunk(sub, z_ref, xr_ref, bcr_ref, dt_ref, cw_ref, cb_ref, dtb_ref, alog_ref, dskip_ref, ng_ref, e_ref,
                   shift_ref, y_ref, state, tail, xs_f.at[sub], xs_b.at[sub], bc.at[sub], ex.at[sub], y_scr.at[sub])
    last = xs_f.shape[0] * q
    halo = tail.shape[0]
    tail[:, 0:d_inner] = xr_ref[last - halo:last, :]
    tail[:, d_inner:] = bcr_ref[last - halo:last, :]


def _ssd_chunk(sub, z_ref, xr_ref, bcr_ref, dt_ref, cw_ref, cb_ref, dtb_ref, alog_ref, dskip_ref, ng_ref, e_ref,
               shift_ref, y_ref, state, tail, xs_f, xs_b, bc, ex, y_scr):
    q = SSM_CHUNK
    d_inner = xr_ref.shape[1]
    gw = d_inner // SSM_GROUPS
    gs = SSM_STATE
    bc_split = SSM_GROUPS * gs
    rows = slice(sub * q, (sub + 1) * q)

    halo = tail.shape[0]
    prev_rows = slice(sub * q - halo, sub * q)
    cw_blk = 256
    for c0 in range(0, d_inner + 2 * bc_split, cw_blk):
        cols = slice(c0, c0 + cw_blk)
        src, scols = (xr_ref, cols) if c0 < d_inner else (bcr_ref, slice(c0 - d_inner, c0 - d_inner + cw_blk))
        cur = src[rows, scols]
        before = tail[:, cols] if sub == 0 else src[prev_rows, scols]
        ext = jnp.concatenate([before, cur], axis=0)
        sh = jnp.dot(shift_ref[...], ext, preferred_element_type=F32)
        acc = cb_ref[:, cols] + cw_ref[SSM_CONV - 1:SSM_CONV, cols] * cur.astype(F32)
        for k in range(SSM_CONV - 1):
            acc = acc + cw_ref[k:k + 1, cols] * sh[k * q:(k + 1) * q]
        act = acc * jax.nn.sigmoid(acc)
        if c0 < d_inner:
            xs_f[:, cols] = act
            xs_b[:, cols] = act.astype(BF16)
        else:
            bc[:, c0 - d_inner:c0 - d_inner + cw_blk] = act.astype(BF16)

    v = dt_ref[rows, :] + dtb_ref[...]
    dt = jnp.maximum(v, 0.0) + jnp.log(1.0 + jnp.exp(-jnp.abs(v)))
    a_neg = -jnp.exp(alog_ref[...])
    d_a = dt * a_neg
    row = lax.broadcasted_iota(jnp.int32, (q, q), 0)
    col = lax.broadcasted_iota(jnp.int32, (q, q), 1)
    causal = row >= col
    tril = jnp.where(causal, 1.0, 0.0).astype(BF16)
    a_cum = None
    for part in _split3_bf16(d_a):
        term = jnp.dot(tril, part, preferred_element_type=F32)
        a_cum = term if a_cum is None else a_cum + term
    a2 = a_cum * LOG2E
    b2_t = (a2 - jnp.log2(dt)).T
    a_end = a_cum[q - 1:q, :]
    ea = jnp.exp(a_cum)
    wgt = dt * jnp.exp(a_end - a_cum)
    ea_hi = ea.astype(BF16)
    ea_lo = (ea - ea_hi.astype(F32)).astype(BF16)
    top = jnp.concatenate([ea_hi, ea_lo], axis=1)
    bot = jnp.concatenate([wgt.astype(BF16), jnp.zeros((q, LANES), BF16)], axis=1)
    ex[...] = jnp.dot(jnp.concatenate([top, bot], axis=0), e_ref[...], preferred_element_type=F32)

    lane = lax.broadcasted_iota(jnp.int32, (q, LANES), 1)
    lo_half = lane < SSM_HEAD_DIM
    contract_last = (((1,), (1,)), ((), ()))
    contract_first = (((0,), (0,)), ((), ()))
    heads_per_group = gw // SSM_HEAD_DIM

    def group_body(g):
        gcols = slice(g * gw, (g + 1) * gw)
        b_g = bc[:, g * gs:(g + 1) * gs]
        c_g = bc[:, bc_split + g * gs:bc_split + (g + 1) * gs]
        cb = lax.dot_general(c_g, b_g, contract_last, preferred_element_type=F32)
        s_in = state[g]
        y_off = jnp.dot(c_g, s_in.astype(BF16), preferred_element_type=F32)
        for pr in range(heads_per_group // 2):
            pcols = slice(g * gw + pr * LANES, g * gw + (pr + 1) * LANES)
            x_pair = xs_b[:, pcols]
            h0 = g * heads_per_group + 2 * pr
            halves = []
            for h in (h0, h0 + 1):
                diff = a2[:, h:h + 1] - b2_t[h:h + 1, :]
                mat = (cb * jnp.exp2(jnp.where(causal, diff, NEG_BIG))).astype(BF16)
                halves.append(jnp.dot(mat, x_pair, preferred_element_type=F32))
            y_diag = jnp.where(lo_half, halves[0], halves[1])
            y_scr[:, pcols] = (y_diag + y_off[:, pr * LANES:(pr + 1) * LANES] * ex[0:q, pcols]
                               + dskip_ref[:, pcols] * xs_f[:, pcols])
        xw = (xs_f[:, gcols] * ex[q:2 * q, gcols]).astype(BF16)
        upd = lax.dot_general(b_g, xw, contract_first, preferred_element_type=F32)
        state[g] = s_in * ex[q - 1:q, gcols] + upd

        zg = z_ref[rows, gcols].astype(F32)
        yz = y_scr[:, gcols] * (zg * jax.nn.sigmoid(zg))
        ms = jnp.mean(yz * yz, axis=-1, keepdims=True)
        y_ref[rows, gcols] = (yz * lax.rsqrt(ms + NORM_EPS) * ng_ref[:, gcols]).astype(y_ref.dtype)

    for g in range(SSM_GROUPS):
        group_body(g)


def _ssd(zxbc, dt_raw, conv_w, conv_b, dt_bias_pad, a_log_pad, d_exp, norm_g, expand, batch, seq):
    t = zxbc.shape[0]
    d_inner = d_exp.shape[1]
    n_bc = zxbc.shape[1] - 2 * d_inner
    n_conv = d_inner + n_bc
    n_sub = SSD_CHUNKS_PER_STEP
    q = SSM_CHUNK
    rows = n_sub * q
    nc = seq // rows
    gw = d_inner // SSM_GROUPS
    row_map = lambda b, c: (b * nc + c, 0)
    const = lambda b, c: (0, 0)
    halo = CONV_TAIL_ROWS
    src = jnp.arange(q)[None, :, None] + (halo - (SSM_CONV - 1)) + jnp.arange(SSM_CONV - 1)[:, None, None]
    shift = (jnp.arange(halo + q)[None, None, :] == src).astype(BF16).reshape((SSM_CONV - 1) * q, halo + q)
    return pl.pallas_call(
        _ssd_kernel,
        grid=(batch, nc),
        in_specs=[
            pl.BlockSpec((rows, d_inner), row_map),
            pl.BlockSpec((rows, d_inner), lambda b, c: (b * nc + c, 1)),
            pl.BlockSpec((rows, n_bc), lambda b, c: (b * nc + c, 2 * d_inner // n_bc)),
            pl.BlockSpec((rows, LANES), row_map),
            pl.BlockSpec((SSM_CONV, n_conv), const),
            pl.BlockSpec((1, n_conv), const),
            pl.BlockSpec((1, LANES), const),
            pl.BlockSpec((1, LANES), const),
            pl.BlockSpec((1, d_inner), const),
            pl.BlockSpec((1, d_inner), const),
            pl.BlockSpec((2 * LANES, d_inner), const),
            pl.BlockSpec(shift.shape, const),
        ],
        out_specs=pl.BlockSpec((rows, d_inner), row_map),
        out_shape=jax.ShapeDtypeStruct((t, d_inner), BF16),
        scratch_shapes=[
            pltpu.VMEM((SSM_GROUPS, SSM_STATE, gw), F32),
            pltpu.VMEM((halo, n_conv), BF16),
            pltpu.VMEM((n_sub, q, d_inner), F32),
            pltpu.VMEM((n_sub, q, d_inner), BF16),
            pltpu.VMEM((n_sub, q, n_bc), BF16),
            pltpu.VMEM((n_sub, 2 * q, d_inner), F32),
            pltpu.VMEM((n_sub, q, d_inner), F32),
        ],
        compiler_params=_cparams(2),
        name="ssd_scan",
    )(zxbc, zxbc, zxbc, dt_raw, conv_w, conv_b, dt_bias_pad, a_log_pad, d_exp, norm_g, expand, shift)


def _mm_res_kernel(a_ref, w_ref, x_ref, gate_ref, o_ref):
    y = jnp.dot(a_ref[...], w_ref[...].astype(BF16), preferred_element_type=F32)
    o_ref[...] = x_ref[...] + gate_ref[...] * y


def _matmul_residual(a, w, x, modr, gate_base, seq):
    t, k = a.shape
    d = x.shape[1]
    tm, tn = ROW_TILE, 512
    tpb = seq // tm
    gate_spec = pl.BlockSpec((None, 1, tn), lambda i, j: (gate_base + i // tpb, 0, j))
    return pl.pallas_call(
        _mm_res_kernel,
        grid=(t // tm, d // tn),
        in_specs=[
            pl.BlockSpec((tm, k), lambda i, j: (i, 0)),
            pl.BlockSpec((k, tn), lambda i, j: (0, j)),
            pl.BlockSpec((tm, tn), lambda i, j: (i, j)),
            gate_spec,
        ],
        out_specs=pl.BlockSpec((tm, tn), lambda i, j: (i, j)),
        out_shape=jax.ShapeDtypeStruct((t, d), F32),
        compiler_params=_cparams(2),
        name="ssm_out_proj",
    )(a, w, x, modr)


def kernel(x, c, ada_w, ada_b, norm_mix, norm_ffn, ffn_w1, ffn_w2, even_w_in, pool_w, pool_scale, even_w_out,
           ssm_w_in, ssm_conv_w, ssm_conv_b, ssm_dt_bias, ssm_a_log, ssm_d, ssm_norm, ssm_w_out, final_norm):
    batch, seq, d = x.shape
    depth = ada_w.shape[0]
    t = batch * seq
    xf = x.reshape(t, d)

    c_pad = jnp.pad(c, ((0, 8 - batch), (0, 0)))
    mod = _ada_mod(c_pad, ada_w, ada_b)[:, :batch]
    modr = mod.reshape(depth, batch, 6, d).transpose(0, 2, 1, 3).reshape(depth * 6 * batch, 1, d)

    d_inner = ssm_w_out.shape[1]
    n_heads = ssm_dt_bias.shape[1]
    head_of_channel = jnp.arange(d_inner) // SSM_HEAD_DIM
    expand1 = (jnp.arange(LANES)[:, None] == head_of_channel[None, :]).astype(BF16)
    expand = jnp.concatenate([expand1, expand1], axis=0)

    for i in range(depth):
        base = i * 6 * batch
        j = i // 2
        g_mix = norm_mix[i].reshape(1, d)
        g_ffn = norm_ffn[i].reshape(1, d)
        if i % 2 == 0:
            qkv, u = _even_in_proj(xf, g_mix, modr, base, even_w_in[j], seq)
            merged = None
            for gi in range(N_GROUPS):
                merged = _attn_group(qkv, merged, gi, batch, seq)
            xf = _even_out_proj(merged, u, pool_w[j], pool_scale[j].reshape(1, POOL_W), even_w_out[j], xf, modr,
                                base + 2 * batch, seq)
        else:
            w_in_t = jnp.swapaxes(ssm_w_in[j], 0, 1)
            n_main = w_in_t.shape[0] - n_heads
            pad_h = ((0, 0), (0, LANES - n_heads))
            zxbc, dt_raw = _ssm_in_proj(xf, g_mix, modr, base, w_in_t, n_main, seq)
            y = _ssd(zxbc, dt_raw, ssm_conv_w[j], ssm_conv_b[j].reshape(1, -1),
                     jnp.pad(ssm_dt_bias[j].reshape(1, -1), pad_h),
                     jnp.pad(ssm_a_log[j].reshape(1, -1), pad_h),
                     jnp.repeat(ssm_d[j], SSM_HEAD_DIM).reshape(1, d_inner), ssm_norm[j].reshape(1, d_inner),
                     expand, batch, seq)
            xf = _matmul_residual(y, ssm_w_out[j], xf, modr, base + 2 * batch, seq)
        xf = _ffn(xf, g_ffn, modr, base, ffn_w1, ffn_w2, i, final_norm.reshape(1, d), seq,
                  final_norm=(i == depth - 1))
    return xf.reshape(batch, seq, d)
```

```python
import functools

import jax
import jax.numpy as jnp
from jax import lax
from jax.experimental import pallas as pl
from jax.experimental.pallas import tpu as pltpu

F32 = jnp.float32
BF16 = jnp.bfloat16

NORM_EPS = 1e-6
NEG_BIG = -1e30
LOG2E = 1.4426950408889634

ATTN_GROUPS = ((128, 1), (512, 4), (2048, 16))
ATTN_HEADS = 8
ATTN_HEAD_DIM = 64
ATTN_BLOCK = 128
ATTN_UNITS_PER_STEP = 8
ATTN_GW = ATTN_HEADS * ATTN_HEAD_DIM
N_GROUPS = len(ATTN_GROUPS)
QKV_W = 3 * N_GROUPS * ATTN_GW
POOL_WINDOWS = (2, 4, 8, 16)
POOL_GW = 128
POOL_W = len(POOL_WINDOWS) * POOL_GW
POOL_HALO = 16

SSM_HEAD_DIM = 64
SSM_GROUPS = 8
SSM_STATE = 128
SSM_CONV = 4
SSM_CHUNK = 128
SSD_CHUNKS_PER_STEP = 2
CONV_TAIL_ROWS = 16

VMEM_LIMIT = 56 * 1024 * 1024
BIG_TILE_VMEM_LIMIT = 60 * 1024 * 1024
LANES = 128
ROW_TILE = 1024


def _cparams(n_axes, vmem_limit=VMEM_LIMIT):
    return pltpu.CompilerParams(dimension_semantics=("arbitrary",) * n_axes, vmem_limit_bytes=vmem_limit)


def _ada_kernel(c_ref, w_ref, b_ref, o_ref):
    c = c_ref[...]
    cond = (c * jax.nn.sigmoid(c)).astype(BF16)
    w = w_ref[...].astype(BF16)
    o_ref[...] = jnp.dot(cond, w, preferred_element_type=F32) + b_ref[...]


def _ada_mod(c_pad, ada_w, ada_b):
    depth, d, n = ada_w.shape
    rows = c_pad.shape[0]
    tn = 1024
    return pl.pallas_call(
        _ada_kernel,
        grid=(depth, n // tn),
        in_specs=[
            pl.BlockSpec((rows, d), lambda l, j: (0, 0)),
            pl.BlockSpec((None, d, tn), lambda l, j: (l, 0, j)),
            pl.BlockSpec((None, 1, tn), lambda l, j: (l, 0, j)),
        ],
        out_specs=pl.BlockSpec((None, rows, tn), lambda l, j: (l, 0, j)),
        out_shape=jax.ShapeDtypeStruct((depth, rows, n), F32),
        compiler_params=_cparams(2),
        name="ada_mod",
    )(c_pad, ada_w, ada_b.reshape(depth, 1, n))


def _norm_mod_rows(x_ref, g_ref, sh_ref, sc_ref, store, rows_per_step=256):
    tm = x_ref.shape[0]
    gain = g_ref[...] * (1.0 + sc_ref[...])
    shift = sh_ref[...]

    def body(i, carry):
        r0 = pl.multiple_of(i * rows_per_step, rows_per_step)
        x = x_ref[pl.ds(r0, rows_per_step), :]
        ms = jnp.mean(x * x, axis=-1, keepdims=True)
        store(r0, x * lax.rsqrt(ms + NORM_EPS) * gain + shift, rows_per_step)
        return carry

    lax.fori_loop(0, tm // rows_per_step, body, 0)


def _norm_mod_to_scratch_unrolled(x_ref, g_ref, sh_ref, sc_ref, h_scr, rows_per_step=64):
    gain = g_ref[...] * (1.0 + sc_ref[...])
    shift = sh_ref[...]
    for r0 in range(0, x_ref.shape[0], rows_per_step):
        x = x_ref[r0:r0 + rows_per_step, :]
        ms = jnp.mean(x * x, axis=-1, keepdims=True)
        h_scr[r0:r0 + rows_per_step, :] = (x * lax.rsqrt(ms + NORM_EPS) * gain + shift).astype(BF16)


def _x_tile_spec(tm, d, n_row_tiles):
    return pl.BlockSpec((tm, d), lambda i, j: (jnp.minimum(i + jnp.minimum(j, 1), n_row_tiles - 1), 0))


def _mod_spec(d, idx_base, tiles_per_batch):
    return pl.BlockSpec((None, 1, d), lambda i, *_: (idx_base + i // tiles_per_batch, 0, 0))


def _even_in_kernel(x_ref, g_ref, sh_ref, sc_ref, w_ref, wu_ref, qkv_ref, u_ref, hf, h3):
    j = pl.program_id(1)
    tm = x_ref.shape[0]
    n_slabs = hf.shape[0]

    @pl.when(j == 0)
    def _():
        def store(r0, h, n):
            h3[0, pl.ds(r0, n), :] = h.astype(BF16)
            for c in range(n_slabs):
                hf[c, pl.ds(r0, n), :] = h[:, c * LANES:(c + 1) * LANES]

        _norm_mod_rows(x_ref, g_ref, sh_ref, sc_ref, store)
        u_ref[...] = jnp.dot(h3[0], wu_ref[...].astype(BF16), preferred_element_type=F32)
        qkv_ref[...] = jnp.dot(h3[0], w_ref[...].astype(BF16), preferred_element_type=F32).astype(BF16)
        for gi, (_, dil) in enumerate(ATTN_GROUPS):
            if dil == 1:
                continue
            rows = tm // dil
            for c in range(n_slabs):
                for r in range(dil):
                    piece = hf[c, pl.ds(r, rows, stride=dil), :]
                    h3[gi, r * rows:(r + 1) * rows, c * LANES:(c + 1) * LANES] = piece.astype(BF16)

    @pl.when(j > 0)
    def _():
        gi = j // N_GROUPS
        qkv_ref[...] = jnp.dot(h3[gi], w_ref[...].astype(BF16), preferred_element_type=F32).astype(BF16)


def _even_in_proj(x, g, modr, mod_base, w_in, seq):
    t, d = x.shape
    tm, tn = ROW_TILE, ATTN_GW
    tpb = seq // tm
    batch = t // seq
    n_u = w_in.shape[1] - QKV_W
    return pl.pallas_call(
        _even_in_kernel,
        grid=(t // tm, QKV_W // tn),
        in_specs=[
            _x_tile_spec(tm, d, t // tm),
            pl.BlockSpec((1, d), lambda i, j: (0, 0)),
            _mod_spec(d, mod_base + 0 * batch, tpb),
            _mod_spec(d, mod_base + 1 * batch, tpb),
            pl.BlockSpec((d, tn), lambda i, j: (0, (j % N_GROUPS) * N_GROUPS + j // N_GROUPS)),
            pl.BlockSpec((d, n_u), lambda i, j: (0, QKV_W // n_u), pipeline_mode=pl.Buffered(1)),
        ],
        out_specs=[
            pl.BlockSpec((tm, tn), lambda i, j: (i, j)),
            pl.BlockSpec((tm, n_u), lambda i, j: (i, 0)),
        ],
        out_shape=[
            jax.ShapeDtypeStruct((t, QKV_W), BF16),
            jax.ShapeDtypeStruct((t, n_u), F32),
        ],
        scratch_shapes=[pltpu.VMEM((d // LANES, tm, LANES), F32), pltpu.VMEM((N_GROUPS, tm, d), BF16)],
        compiler_params=_cparams(2, BIG_TILE_VMEM_LIMIT),
        name="even_in_proj",
    )(x, g, modr, modr, w_in, w_in)


def _unit_rows(ref, u, sl, dil):
    if dil == 1:
        return ref[u * ATTN_BLOCK:(u + 1) * ATTN_BLOCK, sl]
    if len(ref.shape) == 3:
        return ref[u, :, sl]
    return jnp.concatenate([ref[p, u, :, sl] for p in range(ref.shape[0])], axis=0)


def _attn_kernel(*refs, first, last, dil, units):
    if first:
        q_ref, k_ref, v_ref, o_ref, l_ref, kk, vv = refs
        op_ref = lp_ref = None
    elif last:
        q_ref, k_ref, v_ref, op_ref, lp_ref, o_ref, kk, vv = refs
        l_ref = None
    else:
        q_ref, k_ref, v_ref, op_ref, lp_ref, o_ref, l_ref, kk, vv = refs
    step_n = pl.program_id(1)
    step_r = pl.program_id(2)
    blk = ATTN_BLOCK
    full = slice(None)

    @pl.when(step_n == 0)
    def _():
        n_zero = units if dil > 1 else 1
        kk[pl.ds(step_r * n_zero, n_zero), 0:blk, :] = jnp.zeros((n_zero, blk, ATTN_GW), BF16)
        vv[pl.ds(step_r * n_zero, n_zero), 0:blk, :] = jnp.zeros((n_zero, blk, ATTN_GW), BF16)

    qi = lax.broadcasted_iota(jnp.int32, (2 * blk, 2 * blk), 0) & (blk - 1)
    kj = lax.broadcasted_iota(jnp.int32, (2 * blk, 2 * blk), 1)
    cur_bias = jnp.where(kj - blk <= qi, 0.0, NEG_BIG)
    prev_ok = kj >= qi
    lane = lax.broadcasted_iota(jnp.int32, (blk, LANES), 1)
    lo_half = lane < ATTN_HEAD_DIM
    contract_last = (((1,), (1,)), ((), ()))

    for u in range(units):
        if dil > 1:
            r = step_r * units + u
            has_prev = step_n > 0
            rows_out = pl.ds(r, blk, stride=dil)
        else:
            r = 0
            has_prev = (step_n > 0) if u == 0 else True
            rows_out = slice(u * blk, (u + 1) * blk)
        prev_pen = 0.0 if has_prev is True else jnp.where(has_prev, 0.0, NEG_BIG)
        bias = jnp.where(kj < blk, jnp.where(prev_ok, prev_pen, NEG_BIG), cur_bias)
        k_cur = _unit_rows(k_ref, u, full, dil)
        v_cur = _unit_rows(v_ref, u, full, dil)
        kk[r, blk:2 * blk, :] = k_cur
        vv[r, blk:2 * blk, :] = v_cur
        for hp in range(ATTN_GW // LANES):
            sl = slice(hp * LANES, (hp + 1) * LANES)
            qp = _unit_rows(q_ref, u, sl, dil) * jnp.asarray(ATTN_HEAD_DIM ** -0.5, BF16)
            zero = jnp.zeros_like(qp)
            qm = jnp.concatenate([jnp.where(lo_half, qp, zero), jnp.where(lo_half, zero, qp)], axis=0)
            s = lax.dot_general(qm, kk[r, :, sl], contract_last, preferred_element_type=F32) + bias
            m = jnp.max(s, axis=-1, keepdims=True)
            p = jnp.exp(s - m)
            den = jnp.sum(p, axis=-1, keepdims=True)
            acc = jnp.dot(p.astype(BF16), vv[r, :, sl], preferred_element_type=F32)
            o2 = acc / den
            lse2 = jnp.broadcast_to(m + jnp.log(den), (2 * blk, LANES))
            o_new = jnp.where(lo_half, o2[0:blk], o2[blk:2 * blk])
            lse_new = jnp.where(lo_half, lse2[0:blk], lse2[blk:2 * blk])
            if not first:
                lse_old = lp_ref[hp, rows_out, :]
                o_old = op_ref[hp, rows_out, :]
                mx = jnp.maximum(lse_old, lse_new)
                w_old = jnp.exp(lse_old - mx)
                w_new = jnp.exp(lse_new - mx)
                tot = w_old + w_new
                o_new = (o_old * w_old + o_new * w_new) / tot
                lse_new = mx + jnp.log(tot)
            o_ref[hp, rows_out, :] = o_new
            if not last:
                l_ref[hp, rows_out, :] = lse_new
        kk[r, 0:blk, :] = k_cur
        vv[r, 0:blk, :] = v_cur


def _attn_group(qkv, prev, gi, batch, seq):
    dil = ATTN_GROUPS[gi][1]
    first, last = gi == 0, gi == N_GROUPS - 1
    t = batch * seq
    nb = seq // dil // ATTN_BLOCK
    units = ATTN_UNITS_PER_STEP if dil != 4 else 4
    n_pairs = ATTN_GW // LANES
    if dil == 1:
        qkv_v = qkv.reshape(batch, seq, QKV_W)
        grid = (batch, nb // units, 1)
        blk_shape = (None, units * ATTN_BLOCK, ATTN_GW)
        idx = lambda b, n, r, col: (b, n, col)
        o_spec = pl.BlockSpec((n_pairs, units * ATTN_BLOCK, LANES), lambda b, n, r: (0, b * (nb // units) + n, 0))
    else:
        chunk = ROW_TILE // dil
        qkv_v = qkv.reshape(batch, seq // ROW_TILE, dil, chunk, QKV_W)
        grid = (batch, nb, dil // units)
        if chunk >= ATTN_BLOCK:
            sub = chunk // ATTN_BLOCK
            blk_shape = (None, None, units, ATTN_BLOCK, ATTN_GW)
            idx = lambda b, n, r, col: (b, n // sub, r, n % sub, col)
        else:
            blk_shape = (None, ATTN_BLOCK // chunk, units, chunk, ATTN_GW)
            idx = lambda b, n, r, col: (b, n, r, 0, col)
        o_spec = pl.BlockSpec((n_pairs, ATTN_BLOCK * dil, LANES), lambda b, n, r: (0, b * nb + n, 0))
    in_specs = [pl.BlockSpec(blk_shape, functools.partial(lambda b, n, r, col: idx(b, n, r, col), col=gi * 3 + which))
                for which in range(3)]
    args = [qkv_v, qkv_v, qkv_v]
    if not first:
        in_specs += [o_spec, o_spec]
        args += list(prev)
    o_shape = jax.ShapeDtypeStruct((n_pairs, t, LANES), F32)
    outs = pl.pallas_call(
        functools.partial(_attn_kernel, first=first, last=last, dil=dil, units=units),
        grid=grid,
        in_specs=in_specs,
        out_specs=[o_spec] if last else [o_spec, o_spec],
        out_shape=[o_shape] if last else [o_shape, o_shape],
        scratch_shapes=[pltpu.VMEM((dil, 2 * ATTN_BLOCK, ATTN_GW), BF16),
                        pltpu.VMEM((dil, 2 * ATTN_BLOCK, ATTN_GW), BF16)],
        compiler_params=_cparams(3),
        name=f"dilated_attn_g{gi}",
    )(*args)
    return outs[0] if last else tuple(outs)


def _even_out_kernel(attn_ref, u_ref, halo_ref, pw_ref, ps_ref, w_ref, x_ref, gate_ref, o_ref, ue, a_scr, w_bf,
                     *, seq):
    tm = u_ref.shape[0]
    i = pl.program_id(0)

    @pl.when(i == 0)
    def _():
        w_bf[...] = w_ref[...].astype(BF16)

    row0 = (i * tm) % seq
    halo_ok = jnp.where(row0 > 0, 1.0, 0.0)
    ue[0:POOL_HALO, :] = halo_ref[...] * halo_ok
    ue[POOL_HALO:POOL_HALO + tm, :] = u_ref[...]
    for hp in range(ATTN_GW // LANES):
        a_scr[:, hp * LANES:(hp + 1) * LANES] = attn_ref[hp].astype(BF16)
    pos = row0 + lax.broadcasted_iota(jnp.int32, (tm, POOL_GW), 0)
    for gi, win in enumerate(POOL_WINDOWS):
        cols = slice(gi * POOL_GW, (gi + 1) * POOL_GW)
        tok = ue[POOL_HALO:POOL_HALO + tm, cols]
        acc = tok
        for back in range(1, win):
            acc = acc + ue[POOL_HALO - back:POOL_HALO - back + tm, cols]
        cnt = jnp.minimum(pos + 1, win).astype(F32)
        diff = acc / cnt - tok
        y = jnp.dot(diff.astype(BF16), pw_ref[gi].astype(BF16), preferred_element_type=F32) * ps_ref[:, cols]
        a_scr[:, ATTN_GW + gi * POOL_GW:ATTN_GW + (gi + 1) * POOL_GW] = y.astype(BF16)
    y = jnp.dot(a_scr[...], w_bf[...], preferred_element_type=F32)
    o_ref[...] = x_ref[...] + gate_ref[...] * y


def _even_out_proj(attn, u, pool_w, pool_scale, w_out, x, modr, gate_base, seq):
    t, d = x.shape
    tm = 512
    tpb = seq // tm
    k = w_out.shape[0]
    halo_blocks = tm // POOL_HALO
    n_pairs = attn.shape[0]
    return pl.pallas_call(
        functools.partial(_even_out_kernel, seq=seq),
        grid=(t // tm,),
        in_specs=[
            pl.BlockSpec((n_pairs, tm, LANES), lambda i: (0, i, 0)),
            pl.BlockSpec((tm, POOL_W), lambda i: (i, 0)),
            pl.BlockSpec((POOL_HALO, POOL_W), lambda i: (jnp.maximum(i * halo_blocks - 1, 0), 0)),
            pl.BlockSpec(pool_w.shape, lambda i: (0, 0, 0)),
            pl.BlockSpec((1, POOL_W), lambda i: (0, 0)),
            pl.BlockSpec((k, d), lambda i: (0, 0), pipeline_mode=pl.Buffered(1)),
            pl.BlockSpec((tm, d), lambda i: (i, 0)),
            _mod_spec(d, gate_base, tpb),
        ],
        out_specs=pl.BlockSpec((tm, d), lambda i: (i, 0)),
        out_shape=jax.ShapeDtypeStruct((t, d), F32),
        scratch_shapes=[pltpu.VMEM((tm + POOL_HALO, POOL_W), F32), pltpu.VMEM((tm, k), BF16),
                        pltpu.VMEM((k, d), BF16)],
        compiler_params=_cparams(1),
        name="even_out_proj",
    )(attn, u, u, pool_w, pool_scale, w_out, x, modr)


def _ffn_kernel(x_ref, g_ref, sh_ref, sc_ref, gate_ref, w1_ref, w2_ref, fin_ref, o_ref, h_scr, *, final_norm):
    j = pl.program_id(1)

    def hidden_tile():
        a = jnp.dot(h_scr[...], w1_ref[...].astype(BF16), preferred_element_type=F32)
        a = jnp.maximum(a, 0.0)
        return jnp.dot((a * a).astype(BF16), w2_ref[...].astype(BF16), preferred_element_type=F32)

    @pl.when(j == 0)
    def _():
        _norm_mod_to_scratch_unrolled(x_ref, g_ref, sh_ref, sc_ref, h_scr)
        o_ref[...] = x_ref[...] + gate_ref[...] * hidden_tile()

    @pl.when(j > 0)
    def _():
        o_ref[...] += gate_ref[...] * hidden_tile()

    if final_norm:
        @pl.when(j == pl.num_programs(1) - 1)
        def _():
            rows = 256
            fin = fin_ref[...]

            def body(i, carry):
                r0 = pl.multiple_of(i * rows, rows)
                y = o_ref[pl.ds(r0, rows), :]
                ms = jnp.mean(y * y, axis=-1, keepdims=True)
                o_ref[pl.ds(r0, rows), :] = y * lax.rsqrt(ms + NORM_EPS) * fin
                return carry

            lax.fori_loop(0, o_ref.shape[0] // rows, body, 0)


def _ffn(x, g, modr, mod_base, w1_all, w2_all, layer, fin, seq, final_norm):
    t, d = x.shape
    hdim = w1_all.shape[2]
    tm, th = ROW_TILE, 512
    tpb = seq // tm
    batch = t // seq
    return pl.pallas_call(
        functools.partial(_ffn_kernel, final_norm=final_norm),
        grid=(t // tm, hdim // th),
        in_specs=[
            _x_tile_spec(tm, d, t // tm),
            pl.BlockSpec((1, d), lambda i, j: (0, 0)),
            _mod_spec(d, mod_base + 3 * batch, tpb),
            _mod_spec(d, mod_base + 4 * batch, tpb),
            _mod_spec(d, mod_base + 5 * batch, tpb),
            pl.BlockSpec((None, d, th), lambda i, j: (layer, 0, j)),
            pl.BlockSpec((None, th, d), lambda i, j: (layer, j, 0)),
            pl.BlockSpec((1, d), lambda i, j: (0, 0)),
        ],
        out_specs=pl.BlockSpec((tm, d), lambda i, j: (i, 0)),
        out_shape=jax.ShapeDtypeStruct((t, d), F32),
        scratch_shapes=[pltpu.VMEM((tm, d), BF16)],
        compiler_params=_cparams(2, BIG_TILE_VMEM_LIMIT),
        name="ffn_final" if final_norm else "ffn",
    )(x, g, modr, modr, modr, w1_all, w2_all, fin)


def _ssm_in_kernel(x_ref, g_ref, sh_ref, sc_ref, wt_ref, wdt_ref, zxbc_ref, dt_ref, h_scr):
    contract_last = (((1,), (1,)), ((), ()))

    def project():
        zxbc_ref[...] = lax.dot_general(h_scr[...], wt_ref[...].astype(BF16), contract_last,
                                        preferred_element_type=F32).astype(BF16)

    @pl.when(pl.program_id(1) == 0)
    def _():
        _norm_mod_to_scratch_unrolled(x_ref, g_ref, sh_ref, sc_ref, h_scr)
        project()
        dt = lax.dot_general(h_scr[...], wdt_ref[...].astype(BF16), contract_last, preferred_element_type=F32)
        dt_ref[...] = jnp.concatenate([dt, jnp.zeros((dt.shape[0], LANES - dt.shape[1]), F32)], axis=1)

    @pl.when(pl.program_id(1) > 0)
    def _():
        project()


def _ssm_in_proj(x, g, modr, mod_base, w_in_t, n_main, seq):
    t, d = x.shape
    tm, tn = ROW_TILE, 1024
    tpb = seq // tm
    batch = t // seq
    n_dt = w_in_t.shape[0] - n_main
    return pl.pallas_call(
        _ssm_in_kernel,
        grid=(t // tm, n_main // tn),
        in_specs=[
            _x_tile_spec(tm, d, t // tm),
            pl.BlockSpec((1, d), lambda i, j: (0, 0)),
            _mod_spec(d, mod_base + 0 * batch, tpb),
            _mod_spec(d, mod_base + 1 * batch, tpb),
            pl.BlockSpec((tn, d), lambda i, j: (j, 0)),
            pl.BlockSpec((n_dt, d), lambda i, j: (n_main // n_dt, 0)),
        ],
        out_specs=[
            pl.BlockSpec((tm, tn), lambda i, j: (i, j)),
            pl.BlockSpec((tm, LANES), lambda i, j: (i, 0)),
        ],
        out_shape=[
            jax.ShapeDtypeStruct((t, n_main), BF16),
            jax.ShapeDtypeStruct((t, LANES), F32),
        ],
        scratch_shapes=[pltpu.VMEM((tm, d), BF16)],
        compiler_params=_cparams(2),
        name="ssm_in_proj",
    )(x, g, modr, modr, w_in_t, w_in_t)


def _split3_bf16(v):
    hi = v.astype(BF16)
    r1 = v - hi.astype(F32)
    mid = r1.astype(BF16)
    lo = (r1 - mid.astype(F32)).astype(BF16)
    return hi, mid, lo


def _ssd_kernel(z_ref, xr_ref, bcr_ref, dt_ref, cw_ref, cb_ref, dtb_ref, alog_ref, dskip_ref, ng_ref, e_ref, shift_ref,
                y_ref, state, tail, xs_f, xs_b, bc, ex, y_scr):
    q = SSM_CHUNK
    d_inner = xr_ref.shape[1]
    gw = d_inner // SSM_GROUPS
    gs = SSM_STATE
    bc_split = SSM_GROUPS * gs

    @pl.when(pl.program_id(1) == 0)
    def _():
        state[...] = jnp.zeros_like(state)
        tail[...] = jnp.zeros_like(tail)

    for sub in range(xs_f.shape[0]):
        _ssd_chunk(sub, z_ref, xr_ref, bcr_ref, dt_ref, cw_ref, cb_ref, dtb_ref, alog_ref, dskip_ref, ng_ref, e_ref,
                   shift_ref, y_ref, state, tail, xs_f.at[sub], xs_b.at[sub], bc.at[sub], ex.at[sub], y_scr.at[sub])
    last = xs_f.shape[0] * q
    halo = tail.shape[0]
    tail[:, 0:d_inner] = xr_ref[last - halo:last, :]
    tail[:, d_inner:] = bcr_ref[last - halo:last, :]


def _ssd_chunk(sub, z_ref, xr_ref, bcr_ref, dt_ref, cw_ref, cb_ref, dtb_ref, alog_ref, dskip_ref, ng_ref, e_ref,
               shift_ref, y_ref, state, tail, xs_f, xs_b, bc, ex, y_scr):
    q = SSM_CHUNK
    d_inner = xr_ref.shape[1]
    gw = d_inner // SSM_GROUPS
    gs = SSM_STATE
    bc_split = SSM_GROUPS * gs
    rows = slice(sub * q, (sub + 1) * q)

    halo = tail.shape[0]
    prev_rows = slice(sub * q - halo, sub * q)
    cw_blk = 256
    for c0 in range(0, d_inner + 2 * bc_split, cw_blk):
        cols = slice(c0, c0 + cw_blk)
        src, scols = (xr_ref, cols) if c0 < d_inner else (bcr_ref, slice(c0 - d_inner, c0 - d_inner + cw_blk))
        cur = src[rows, scols]
        before = tail[:, cols] if sub == 0 else src[prev_rows, scols]
        ext = jnp.concatenate([before, cur], axis=0)
        sh = jnp.dot(shift_ref[...], ext, preferred_element_type=F32)
        acc = cb_ref[:, cols] + cw_ref[SSM_CONV - 1:SSM_CONV, cols] * cur.astype(F32)
        for k in range(SSM_CONV - 1):
            acc = acc + cw_ref[k:k + 1, cols] * sh[k * q:(k + 1) * q]
        act = acc * jax.nn.sigmoid(acc)
        if c0 < d_inner:
            xs_f[:, cols] = act
            xs_b[:, cols] = act.astype(BF16)
        else:
            bc[:, c0 - d_inner:c0 - d_inner + cw_blk] = act.astype(BF16)

    v = dt_ref[rows, :] + dtb_ref[...]
    dt = jnp.maximum(v, 0.0) + jnp.log(1.0 + jnp.exp(-jnp.abs(v)))
    a_neg = -jnp.exp(alog_ref[...])
    d_a = dt * a_neg
    row = lax.broadcasted_iota(jnp.int32, (q, q), 0)
    col = lax.broadcasted_iota(jnp.int32, (q, q), 1)
    causal = row >= col
    tril = jnp.where(causal, 1.0, 0.0).astype(BF16)
    a_cum = None
    for part in _split3_bf16(d_a):
        term = jnp.dot(tril, part, preferred_element_type=F32)
        a_cum = term if a_cum is None else a_cum + term
    a2 = a_cum * LOG2E
    b2_t = (a2 - jnp.log2(dt)).T
    a_end = a_cum[q - 1:q, :]
    ea = jnp.exp(a_cum)
    wgt = dt * jnp.exp(a_end - a_cum)
    ea_hi = ea.astype(BF16)
    ea_lo = (ea - ea_hi.astype(F32)).astype(BF16)
    top = jnp.concatenate([ea_hi, ea_lo], axis=1)
    bot = jnp.concatenate([wgt.astype(BF16), jnp.zeros((q, LANES), BF16)], axis=1)
    ex[...] = jnp.dot(jnp.concatenate([top, bot], axis=0), e_ref[...], preferred_element_type=F32)

    lane = lax.broadcasted_iota(jnp.int32, (q, LANES), 1)
    lo_half = lane < SSM_HEAD_DIM
    contract_last = (((1,), (1,)), ((), ()))
    contract_first = (((0,), (0,)), ((), ()))
    heads_per_group = gw // SSM_HEAD_DIM

    def group_body(g):
        gcols = slice(g * gw, (g + 1) * gw)
        b_g = bc[:, g * gs:(g + 1) * gs]
        c_g = bc[:, bc_split + g * gs:bc_split + (g + 1) * gs]
        cb = lax.dot_general(c_g, b_g, contract_last, preferred_element_type=F32)
        s_in = state[g]
        y_off = jnp.dot(c_g, s_in.astype(BF16), preferred_element_type=F32)
        for pr in range(heads_per_group // 2):
            pcols = slice(g * gw + pr * LANES, g * gw + (pr + 1) * LANES)
            x_pair = xs_b[:, pcols]
            h0 = g * heads_per_group + 2 * pr
            halves = []
            for h in (h0, h0 + 1):
                diff = a2[:, h:h + 1] - b2_t[h:h + 1, :]
                mat = (cb * jnp.exp2(jnp.where(causal, diff, NEG_BIG))).astype(BF16)
                halves.append(jnp.dot(mat, x_pair, preferred_element_type=F32))
            y_diag = jnp.where(lo_half, halves[0], halves[1])
            y_scr[:, pcols] = (y_diag + y_off[:, pr * LANES:(pr + 1) * LANES] * ex[0:q, pcols]
                               + dskip_ref[:, pcols] * xs_f[:, pcols])
        xw = (xs_f[:, gcols] * ex[q:2 * q, gcols]).astype(BF16)
        upd = lax.dot_general(b_g, xw, contract_first, preferred_element_type=F32)
        state[g] = s_in * ex[q - 1:q, gcols] + upd

        zg = z_ref[rows, gcols].astype(F32)
        yz = y_scr[:, gcols] * (zg * jax.nn.sigmoid(zg))
        ms = jnp.mean(yz * yz, axis=-1, keepdims=True)
        y_ref[rows, gcols] = (yz * lax.rsqrt(ms + NORM_EPS) * ng_ref[:, gcols]).astype(y_ref.dtype)

    for g in range(SSM_GROUPS):
        group_body(g)


def _ssd(zxbc, dt_raw, conv_w, conv_b, dt_bias_pad, a_log_pad, d_exp, norm_g, expand, batch, seq):
    t = zxbc.shape[0]
    d_inner = d_exp.shape[1]
    n_bc = zxbc.shape[1] - 2 * d_inner
    n_conv = d_inner + n_bc
    n_sub = SSD_CHUNKS_PER_STEP
    q = SSM_CHUNK
    rows = n_sub * q
    nc = seq // rows
    gw = d_inner // SSM_GROUPS
    row_map = lambda b, c: (b * nc + c, 0)
    const = lambda b, c: (0, 0)
    halo = CONV_TAIL_ROWS
    src = jnp.arange(q)[None, :, None] + (halo - (SSM_CONV - 1)) + jnp.arange(SSM_CONV - 1)[:, None, None]
    shift = (jnp.arange(halo + q)[None, None, :] == src).astype(BF16).reshape((SSM_CONV - 1) * q, halo + q)
    return pl.pallas_call(
        _ssd_kernel,
        grid=(batch, nc),
        in_specs=[
            pl.BlockSpec((rows, d_inner), row_map),
            pl.BlockSpec((rows, d_inner), lambda b, c: (b * nc + c, 1)),
            pl.BlockSpec((rows, n_bc), lambda b, c: (b * nc + c, 2 * d_inner // n_bc)),
            pl.BlockSpec((rows, LANES), row_map),
            pl.BlockSpec((SSM_CONV, n_conv), const),
            pl.BlockSpec((1, n_conv), const),
            pl.BlockSpec((1, LANES), const),
            pl.BlockSpec((1, LANES), const),
            pl.BlockSpec((1, d_inner), const),
            pl.BlockSpec((1, d_inner), const),
            pl.BlockSpec((2 * LANES, d_inner), const),
            pl.BlockSpec(shift.shape, const),
        ],
        out_specs=pl.BlockSpec((rows, d_inner), row_map),
        out_shape=jax.ShapeDtypeStruct((t, d_inner), BF16),
        scratch_shapes=[
            pltpu.VMEM((SSM_GROUPS, SSM_STATE, gw), F32),
            pltpu.VMEM((halo, n_conv), BF16),
            pltpu.VMEM((n_sub, q, d_inner), F32),
            pltpu.VMEM((n_sub, q, d_inner), BF16),
            pltpu.VMEM((n_sub, q, n_bc), BF16),
            pltpu.VMEM((n_sub, 2 * q, d_inner), F32),
            pltpu.VMEM((n_sub, q, d_inner), F32),
        ],
        compiler_params=_cparams(2),
        name="ssd_scan",
    )(zxbc, zxbc, zxbc, dt_raw, conv_w, conv_b, dt_bias_pad, a_log_pad, d_exp, norm_g, expand, shift)


def _mm_res_kernel(a_ref, w_ref, x_ref, gate_ref, o_ref):
    y = jnp.dot(a_ref[...], w_ref[...].astype(BF16), preferred_element_type=F32)
    o_ref[...] = x_ref[...] + gate_ref[...] * y


def _matmul_residual(a, w, x, modr, gate_base, seq):
    t, k = a.shape
    d = x.shape[1]
    tm, tn = ROW_TILE, 512
    tpb = seq // tm
    gate_spec = pl.BlockSpec((None, 1, tn), lambda i, j: (gate_base + i // tpb, 0, j))
    return pl.pallas_call(
        _mm_res_kernel,
        grid=(t // tm, d // tn),
        in_specs=[
            pl.BlockSpec((tm, k), lambda i, j: (i, 0)),
            pl.BlockSpec((k, tn), lambda i, j: (0, j)),
            pl.BlockSpec((tm, tn), lambda i, j: (i, j)),
            gate_spec,
        ],
        out_specs=pl.BlockSpec((tm, tn), lambda i, j: (i, j)),
        out_shape=jax.ShapeDtypeStruct((t, d), F32),
        compiler_params=_cparams(2),
        name="ssm_out_proj",
    )(a, w, x, modr)


def kernel(x, c, ada_w, ada_b, norm_mix, norm_ffn, ffn_w1, ffn_w2, even_w_in, pool_w, pool_scale, even_w_out,
           ssm_w_in, ssm_conv_w, ssm_conv_b, ssm_dt_bias, ssm_a_log, ssm_d, ssm_norm, ssm_w_out, final_norm):
    batch, seq, d = x.shape
    depth = ada_w.shape[0]
    t = batch * seq
    xf = x.reshape(t, d)

    c_pad = jnp.pad(c, ((0, 8 - batch), (0, 0)))
    mod = _ada_mod(c_pad, ada_w, ada_b)[:, :batch]
    modr = mod.reshape(depth, batch, 6, d).transpose(0, 2, 1, 3).reshape(depth * 6 * batch, 1, d)

    d_inner = ssm_w_out.shape[1]
    n_heads = ssm_dt_bias.shape[1]
    head_of_channel = jnp.arange(d_inner) // SSM_HEAD_DIM
    expand1 = (jnp.arange(LANES)[:, None] == head_of_channel[None, :]).astype(BF16)
    expand = jnp.concatenate([expand1, expand1], axis=0)

    for i in range(depth):
        base = i * 6 * batch
        j = i // 2
        g_mix = norm_mix[i].reshape(1, d)
        g_ffn = norm_ffn[i].reshape(1, d)
        if i % 2 == 0:
            qkv, u = _even_in_proj(xf, g_mix, modr, base, even_w_in[j], seq)
            merged = None
            for gi in range(N_GROUPS):
                merged = _attn_group(qkv, merged, gi, batch, seq)
            xf = _even_out_proj(merged, u, pool_w[j], pool_scale[j].reshape(1, POOL_W), even_w_out[j], xf, modr,
                                base + 2 * batch, seq)
        else:
            w_in_t = jnp.swapaxes(ssm_w_in[j], 0, 1)
            n_main = w_in_t.shape[0] - n_heads
            pad_h = ((0, 0), (0, LANES - n_heads))
            zxbc, dt_raw = _ssm_in_proj(xf, g_mix, modr, base, w_in_t, n_main, seq)
            y = _ssd(zxbc, dt_raw, ssm_conv_w[j], ssm_conv_b[j].reshape(1, -1),
                     jnp.pad(ssm_dt_bias[j].reshape(1, -1), pad_h),
                     jnp.pad(ssm_a_log[j].reshape(1, -1), pad_h),
                     jnp.repeat(ssm_d[j], SSM_HEAD_DIM).reshape(1, d_inner), ssm_norm[j].reshape(1, d_inner),
                     expand, batch, seq)
            xf = _matmul_residual(y, ssm_w_out[j], xf, modr, base + 2 * batch, seq)
        xf = _ffn(xf, g_ffn, modr, base, ffn_w1, ffn_w2, i, final_norm.reshape(1, d), seq,
                  final_norm=(i == depth - 1))
    return xf.reshape(batch, seq, d)
```

```python
import functools

import jax
import jax.numpy as jnp
from jax import lax
from jax.experimental import pallas as pl
from jax.experimental.pallas import tpu as pltpu

F32 = jnp.float32
BF16 = jnp.bfloat16

NORM_EPS = 1e-6
NEG_BIG = -1e30
LOG2E = 1.4426950408889634

ATTN_GROUPS = ((128, 1), (512, 4), (2048, 16))
ATTN_HEADS = 8
ATTN_HEAD_DIM = 64
ATTN_BLOCK = 128
ATTN_UNITS_PER_STEP = 8
ATTN_GW = ATTN_HEADS * ATTN_HEAD_DIM
N_GROUPS = len(ATTN_GROUPS)
QKV_W = 3 * N_GROUPS * ATTN_GW
POOL_WINDOWS = (2, 4, 8, 16)
POOL_GW = 128
POOL_W = len(POOL_WINDOWS) * POOL_GW
POOL_HALO = 16

SSM_HEAD_DIM = 64
SSM_GROUPS = 8
SSM_STATE = 128
SSM_CONV = 4
SSM_CHUNK = 128
SSD_CHUNKS_PER_STEP = 2
CONV_TAIL_ROWS = 16

VMEM_LIMIT = 56 * 1024 * 1024
BIG_TILE_VMEM_LIMIT = 60 * 1024 * 1024
LANES = 128
ROW_TILE = 1024


def _cparams(n_axes, vmem_limit=VMEM_LIMIT):
    return pltpu.CompilerParams(dimension_semantics=("arbitrary",) * n_axes, vmem_limit_bytes=vmem_limit)


def _ada_kernel(c_ref, w_ref, b_ref, o_ref):
    c = c_ref[...]
    cond = (c * jax.nn.sigmoid(c)).astype(BF16)
    w = w_ref[...].astype(BF16)
    o_ref[...] = jnp.dot(cond, w, preferred_element_type=F32) + b_ref[...]


def _ada_mod(c_pad, ada_w, ada_b):
    depth, d, n = ada_w.shape
    rows = c_pad.shape[0]
    tn = 1024
    return pl.pallas_call(
        _ada_kernel,
        grid=(depth, n // tn),
        in_specs=[
            pl.BlockSpec((rows, d), lambda l, j: (0, 0)),
            pl.BlockSpec((None, d, tn), lambda l, j: (l, 0, j)),
            pl.BlockSpec((None, 1, tn), lambda l, j: (l, 0, j)),
        ],
        out_specs=pl.BlockSpec((None, rows, tn), lambda l, j: (l, 0, j)),
        out_shape=jax.ShapeDtypeStruct((depth, rows, n), F32),
        compiler_params=_cparams(2),
        name="ada_mod",
    )(c_pad, ada_w, ada_b.reshape(depth, 1, n))


def _norm_mod_rows(x_ref, g_ref, sh_ref, sc_ref, store, rows_per_step=256):
    tm = x_ref.shape[0]
    gain = g_ref[...] * (1.0 + sc_ref[...])
    shift = sh_ref[...]

    def body(i, carry):
        r0 = pl.multiple_of(i * rows_per_step, rows_per_step)
        x = x_ref[pl.ds(r0, rows_per_step), :]
        ms = jnp.mean(x * x, axis=-1, keepdims=True)
        store(r0, x * lax.rsqrt(ms + NORM_EPS) * gain + shift, rows_per_step)
        return carry

    lax.fori_loop(0, tm // rows_per_step, body, 0)


def _norm_mod_to_scratch_unrolled(x_ref, g_ref, sh_ref, sc_ref, h_scr, rows_per_step=64):
    gain = g_ref[...] * (1.0 + sc_ref[...])
    shift = sh_ref[...]
    for r0 in range(0, x_ref.shape[0], rows_per_step):
        x = x_ref[r0:r0 + rows_per_step, :]
        ms = jnp.mean(x * x, axis=-1, keepdims=True)
        h_scr[r0:r0 + rows_per_step, :] = (x * lax.rsqrt(ms + NORM_EPS) * gain + shift).astype(BF16)


def _x_tile_spec(tm, d, n_row_tiles, advance_at=1):
    return pl.BlockSpec((tm, d), lambda i, j: (jnp.minimum(i + jnp.minimum(j // advance_at, 1), n_row_tiles - 1), 0))


def _mod_spec(d, idx_base, tiles_per_batch):
    return pl.BlockSpec((None, 1, d), lambda i, *_: (idx_base + i // tiles_per_batch, 0, 0))


def _even_in_kernel(x_ref, g_ref, sh_ref, sc_ref, w_ref, wu_ref, qkv_ref, u_ref, hf, h3):
    j = pl.program_id(1)
    tm = x_ref.shape[0]
    n_slabs = hf.shape[0]

    @pl.when(j == 0)
    def _():
        def store(r0, h, n):
            h3[0, pl.ds(r0, n), :] = h.astype(BF16)
            for c in range(n_slabs):
                hf[c, pl.ds(r0, n), :] = h[:, c * LANES:(c + 1) * LANES]

        _norm_mod_rows(x_ref, g_ref, sh_ref, sc_ref, store)
        u_ref[...] = jnp.dot(h3[0], wu_ref[...].astype(BF16), preferred_element_type=F32)
        qkv_ref[...] = jnp.dot(h3[0], w_ref[...].astype(BF16), preferred_element_type=F32).astype(BF16)
        for gi, (_, dil) in enumerate(ATTN_GROUPS):
            if dil == 1:
                continue
            rows = tm // dil
            for c in range(n_slabs):
                for r in range(dil):
                    piece = hf[c, pl.ds(r, rows, stride=dil), :]
                    h3[gi, r * rows:(r + 1) * rows, c * LANES:(c + 1) * LANES] = piece.astype(BF16)

    @pl.when(j > 0)
    def _():
        gi = j // N_GROUPS
        qkv_ref[...] = jnp.dot(h3[gi], w_ref[...].astype(BF16), preferred_element_type=F32).astype(BF16)


def _even_in_proj(x, g, modr, mod_base, w_in, seq):
    t, d = x.shape
    tm, tn = ROW_TILE, ATTN_GW
    tpb = seq // tm
    batch = t // seq
    n_u = w_in.shape[1] - QKV_W
    return pl.pallas_call(
        _even_in_kernel,
        grid=(t // tm, QKV_W // tn),
        in_specs=[
            _x_tile_spec(tm, d, t // tm),
            pl.BlockSpec((1, d), lambda i, j: (0, 0)),
            _mod_spec(d, mod_base + 0 * batch, tpb),
            _mod_spec(d, mod_base + 1 * batch, tpb),
            pl.BlockSpec((d, tn), lambda i, j: (0, (j % N_GROUPS) * N_GROUPS + j // N_GROUPS)),
            pl.BlockSpec((d, n_u), lambda i, j: (0, QKV_W // n_u), pipeline_mode=pl.Buffered(1)),
        ],
        out_specs=[
            pl.BlockSpec((tm, tn), lambda i, j: (i, j)),
            pl.BlockSpec((tm, n_u), lambda i, j: (i, 0)),
        ],
        out_shape=[
            jax.ShapeDtypeStruct((t, QKV_W), BF16),
            jax.ShapeDtypeStruct((t, n_u), F32),
        ],
        scratch_shapes=[pltpu.VMEM((d // LANES, tm, LANES), F32), pltpu.VMEM((N_GROUPS, tm, d), BF16)],
        compiler_params=_cparams(2, BIG_TILE_VMEM_LIMIT),
        name="even_in_proj",
    )(x, g, modr, modr, w_in, w_in)


def _unit_rows(ref, u, sl, dil):
    if dil == 1:
        return ref[u * ATTN_BLOCK:(u + 1) * ATTN_BLOCK, sl]
    if len(ref.shape) == 3:
        return ref[u, :, sl]
    return jnp.concatenate([ref[p, u, :, sl] for p in range(ref.shape[0])], axis=0)


def _attn_kernel(*refs, first, last, dil, units):
    if first:
        q_ref, k_ref, v_ref, o_ref, l_ref, kk, vv = refs
        op_ref = lp_ref = None
    elif last:
        q_ref, k_ref, v_ref, op_ref, lp_ref, o_ref, kk, vv = refs
        l_ref = None
    else:
        q_ref, k_ref, v_ref, op_ref, lp_ref, o_ref, l_ref, kk, vv = refs
    step_n = pl.program_id(1)
    step_r = pl.program_id(2)
    blk = ATTN_BLOCK
    full = slice(None)

    @pl.when(step_n == 0)
    def _():
        n_zero = units if dil > 1 else 1
        kk[pl.ds(step_r * n_zero, n_zero), 0:blk, :] = jnp.zeros((n_zero, blk, ATTN_GW), BF16)
        vv[pl.ds(step_r * n_zero, n_zero), 0:blk, :] = jnp.zeros((n_zero, blk, ATTN_GW), BF16)

    qi = lax.broadcasted_iota(jnp.int32, (2 * blk, 2 * blk), 0) & (blk - 1)
    kj = lax.broadcasted_iota(jnp.int32, (2 * blk, 2 * blk), 1)
    cur_bias = jnp.where(kj - blk <= qi, 0.0, NEG_BIG)
    prev_ok = kj >= qi
    lane = lax.broadcasted_iota(jnp.int32, (blk, LANES), 1)
    lo_half = lane < ATTN_HEAD_DIM
    contract_last = (((1,), (1,)), ((), ()))

    for u in range(units):
        if dil > 1:
            r = step_r * units + u
            has_prev = step_n > 0
            rows_out = pl.ds(r, blk, stride=dil)
        else:
            r = 0
            has_prev = (step_n > 0) if u == 0 else True
            rows_out = slice(u * blk, (u + 1) * blk)
        prev_pen = 0.0 if has_prev is True else jnp.where(has_prev, 0.0, NEG_BIG)
        bias = jnp.where(kj < blk, jnp.where(prev_ok, prev_pen, NEG_BIG), cur_bias)
        k_cur = _unit_rows(k_ref, u, full, dil)
        v_cur = _unit_rows(v_ref, u, full, dil)
        kk[r, blk:2 * blk, :] = k_cur
        vv[r, blk:2 * blk, :] = v_cur
        for hp in range(ATTN_GW // LANES):
            sl = slice(hp * LANES, (hp + 1) * LANES)
            qp = _unit_rows(q_ref, u, sl, dil) * jnp.asarray(ATTN_HEAD_DIM ** -0.5, BF16)
            zero = jnp.zeros_like(qp)
            qm = jnp.concatenate([jnp.where(lo_half, qp, zero), jnp.where(lo_half, zero, qp)], axis=0)
            s = lax.dot_general(qm, kk[r, :, sl], contract_last, preferred_element_type=F32) + bias
            m = jnp.max(s, axis=-1, keepdims=True)
            p = jnp.exp(s - m)
            den = jnp.sum(p, axis=-1, keepdims=True)
            acc = jnp.dot(p.astype(BF16), vv[r, :, sl], preferred_element_type=F32)
            o2 = acc / den
            lse2 = jnp.broadcast_to(m + jnp.log(den), (2 * blk, LANES))
            o_new = jnp.where(lo_half, o2[0:blk], o2[blk:2 * blk])
            lse_new = jnp.where(lo_half, lse2[0:blk], lse2[blk:2 * blk])
            if not first:
                lse_old = lp_ref[hp, rows_out, :]
                o_old = op_ref[hp, rows_out, :]
                mx = jnp.maximum(lse_old, lse_new)
                w_old = jnp.exp(lse_old - mx)
                w_new = jnp.exp(lse_new - mx)
                tot = w_old + w_new
                o_new = (o_old * w_old + o_new * w_new) / tot
                lse_new = mx + jnp.log(tot)
            o_ref[hp, rows_out, :] = o_new
            if not last:
                l_ref[hp, rows_out, :] = lse_new
        kk[r, 0:blk, :] = k_cur
        vv[r, 0:blk, :] = v_cur


def _attn_group(qkv, prev, gi, batch, seq):
    dil = ATTN_GROUPS[gi][1]
    first, last = gi == 0, gi == N_GROUPS - 1
    t = batch * seq
    nb = seq // dil // ATTN_BLOCK
    units = ATTN_UNITS_PER_STEP if dil != 4 else 4
    n_pairs = ATTN_GW // LANES
    if dil == 1:
        qkv_v = qkv.reshape(batch, seq, QKV_W)
        grid = (batch, nb // units, 1)
        blk_shape = (None, units * ATTN_BLOCK, ATTN_GW)
        idx = lambda b, n, r, col: (b, n, col)
        o_spec = pl.BlockSpec((n_pairs, units * ATTN_BLOCK, LANES), lambda b, n, r: (0, b * (nb // units) + n, 0))
    else:
        chunk = ROW_TILE // dil
        qkv_v = qkv.reshape(batch, seq // ROW_TILE, dil, chunk, QKV_W)
        grid = (batch, nb, dil // units)
        if chunk >= ATTN_BLOCK:
            sub = chunk // ATTN_BLOCK
            blk_shape = (None, None, units, ATTN_BLOCK, ATTN_GW)
            idx = lambda b, n, r, col: (b, n // sub, r, n % sub, col)
        else:
            blk_shape = (None, ATTN_BLOCK // chunk, units, chunk, ATTN_GW)
            idx = lambda b, n, r, col: (b, n, r, 0, col)
        o_spec = pl.BlockSpec((n_pairs, ATTN_BLOCK * dil, LANES), lambda b, n, r: (0, b * nb + n, 0))
    in_specs = [pl.BlockSpec(blk_shape, functools.partial(lambda b, n, r, col: idx(b, n, r, col), col=gi * 3 + which))
                for which in range(3)]
    args = [qkv_v, qkv_v, qkv_v]
    if not first:
        in_specs += [o_spec, o_spec]
        args += list(prev)
    o_shape = jax.ShapeDtypeStruct((n_pairs, t, LANES), F32)
    outs = pl.pallas_call(
        functools.partial(_attn_kernel, first=first, last=last, dil=dil, units=units),
        grid=grid,
        in_specs=in_specs,
        out_specs=[o_spec] if last else [o_spec, o_spec],
        out_shape=[o_shape] if last else [o_shape, o_shape],
        scratch_shapes=[pltpu.VMEM((dil, 2 * ATTN_BLOCK, ATTN_GW), BF16),
                        pltpu.VMEM((dil, 2 * ATTN_BLOCK, ATTN_GW), BF16)],
        compiler_params=_cparams(3),
        name=f"dilated_attn_g{gi}",
    )(*args)
    return outs[0] if last else tuple(outs)


def _even_out_kernel(attn_ref, u_ref, halo_ref, pw_ref, ps_ref, w_ref, x_ref, gate_ref, o_ref, ue, a_scr, w_bf,
                     *, seq):
    tm = u_ref.shape[0]
    i = pl.program_id(0)

    @pl.when(i == 0)
    def _():
        w_bf[...] = w_ref[...].astype(BF16)

    row0 = (i * tm) % seq
    halo_ok = jnp.where(row0 > 0, 1.0, 0.0)
    ue[0:POOL_HALO, :] = halo_ref[...] * halo_ok
    ue[POOL_HALO:POOL_HALO + tm, :] = u_ref[...]
    for hp in range(ATTN_GW // LANES):
        a_scr[:, hp * LANES:(hp + 1) * LANES] = attn_ref[hp].astype(BF16)
    pos = row0 + lax.broadcasted_iota(jnp.int32, (tm, POOL_GW), 0)
    for gi, win in enumerate(POOL_WINDOWS):
        cols = slice(gi * POOL_GW, (gi + 1) * POOL_GW)
        tok = ue[POOL_HALO:POOL_HALO + tm, cols]
        acc = tok
        for back in range(1, win):
            acc = acc + ue[POOL_HALO - back:POOL_HALO - back + tm, cols]
        cnt = jnp.minimum(pos + 1, win).astype(F32)
        diff = acc / cnt - tok
        y = jnp.dot(diff.astype(BF16), pw_ref[gi].astype(BF16), preferred_element_type=F32) * ps_ref[:, cols]
        a_scr[:, ATTN_GW + gi * POOL_GW:ATTN_GW + (gi + 1) * POOL_GW] = y.astype(BF16)
    y = jnp.dot(a_scr[...], w_bf[...], preferred_element_type=F32)
    o_ref[...] = x_ref[...] + gate_ref[...] * y


def _even_out_proj(attn, u, pool_w, pool_scale, w_out, x, modr, gate_base, seq):
    t, d = x.shape
    tm = 512
    tpb = seq // tm
    k = w_out.shape[0]
    halo_blocks = tm // POOL_HALO
    n_pairs = attn.shape[0]
    return pl.pallas_call(
        functools.partial(_even_out_kernel, seq=seq),
        grid=(t // tm,),
        in_specs=[
            pl.BlockSpec((n_pairs, tm, LANES), lambda i: (0, i, 0)),
            pl.BlockSpec((tm, POOL_W), lambda i: (i, 0)),
            pl.BlockSpec((POOL_HALO, POOL_W), lambda i: (jnp.maximum(i * halo_blocks - 1, 0), 0)),
            pl.BlockSpec(pool_w.shape, lambda i: (0, 0, 0)),
            pl.BlockSpec((1, POOL_W), lambda i: (0, 0)),
            pl.BlockSpec((k, d), lambda i: (0, 0), pipeline_mode=pl.Buffered(1)),
            pl.BlockSpec((tm, d), lambda i: (i, 0)),
            _mod_spec(d, gate_base, tpb),
        ],
        out_specs=pl.BlockSpec((tm, d), lambda i: (i, 0)),
        out_shape=jax.ShapeDtypeStruct((t, d), F32),
        scratch_shapes=[pltpu.VMEM((tm + POOL_HALO, POOL_W), F32), pltpu.VMEM((tm, k), BF16),
                        pltpu.VMEM((k, d), BF16)],
        compiler_params=_cparams(1),
        name="even_out_proj",
    )(attn, u, u, pool_w, pool_scale, w_out, x, modr)


def _ffn_kernel(x_ref, g_ref, sh_ref, sc_ref, gate_ref, w1_ref, w2_ref, fin_ref, o_ref, h_scr, *, final_norm):
    j = pl.program_id(1)

    def hidden_tile():
        a = jnp.dot(h_scr[...], w1_ref[...].astype(BF16), preferred_element_type=F32)
        a = jnp.maximum(a, 0.0)
        return jnp.dot((a * a).astype(BF16), w2_ref[...].astype(BF16), preferred_element_type=F32)

    @pl.when(j == 0)
    def _():
        _norm_mod_to_scratch_unrolled(x_ref, g_ref, sh_ref, sc_ref, h_scr)
        o_ref[...] = x_ref[...] + gate_ref[...] * hidden_tile()

    @pl.when(j > 0)
    def _():
        o_ref[...] += gate_ref[...] * hidden_tile()

    if final_norm:
        @pl.when(j == pl.num_programs(1) - 1)
        def _():
            rows = 256
            fin = fin_ref[...]

            def body(i, carry):
                r0 = pl.multiple_of(i * rows, rows)
                y = o_ref[pl.ds(r0, rows), :]
                ms = jnp.mean(y * y, axis=-1, keepdims=True)
                o_ref[pl.ds(r0, rows), :] = y * lax.rsqrt(ms + NORM_EPS) * fin
                return carry

            lax.fori_loop(0, o_ref.shape[0] // rows, body, 0)


def _ffn(x, g, modr, mod_base, w1_all, w2_all, layer, fin, seq, final_norm):
    t, d = x.shape
    hdim = w1_all.shape[2]
    tm, th = ROW_TILE, 512
    tpb = seq // tm
    batch = t // seq
    return pl.pallas_call(
        functools.partial(_ffn_kernel, final_norm=final_norm),
        grid=(t // tm, hdim // th),
        in_specs=[
            _x_tile_spec(tm, d, t // tm, advance_at=2),
            pl.BlockSpec((1, d), lambda i, j: (0, 0)),
            _mod_spec(d, mod_base + 3 * batch, tpb),
            _mod_spec(d, mod_base + 4 * batch, tpb),
            _mod_spec(d, mod_base + 5 * batch, tpb),
            pl.BlockSpec((None, d, th), lambda i, j: (layer, 0, j)),
            pl.BlockSpec((None, th, d), lambda i, j: (layer, j, 0)),
            pl.BlockSpec((1, d), lambda i, j: (0, 0)),
        ],
        out_specs=pl.BlockSpec((tm, d), lambda i, j: (i, 0)),
        out_shape=jax.ShapeDtypeStruct((t, d), F32),
        scratch_shapes=[pltpu.VMEM((tm, d), BF16)],
        compiler_params=_cparams(2, BIG_TILE_VMEM_LIMIT),
        name="ffn_final" if final_norm else "ffn",
    )(x, g, modr, modr, modr, w1_all, w2_all, fin)


def _ssm_in_kernel(x_ref, g_ref, sh_ref, sc_ref, wt_ref, wdt_ref, zxbc_ref, dt_ref, h_scr):
    contract_last = (((1,), (1,)), ((), ()))

    def project():
        zxbc_ref[...] = lax.dot_general(h_scr[...], wt_ref[...].astype(BF16), contract_last,
                                        preferred_element_type=F32).astype(BF16)

    @pl.when(pl.program_id(1) == 0)
    def _():
        _norm_mod_to_scratch_unrolled(x_ref, g_ref, sh_ref, sc_ref, h_scr)
        project()
        dt = lax.dot_general(h_scr[...], wdt_ref[...].astype(BF16), contract_last, preferred_element_type=F32)
        dt_ref[...] = jnp.concatenate([dt, jnp.zeros((dt.shape[0], LANES - dt.shape[1]), F32)], axis=1)

    @pl.when(pl.program_id(1) > 0)
    def _():
        project()


def _ssm_in_proj(x, g, modr, mod_base, w_in_t, n_main, seq):
    t, d = x.shape
    tm, tn = ROW_TILE, 1024
    tpb = seq // tm
    batch = t // seq
    n_dt = w_in_t.shape[0] - n_main
    return pl.pallas_call(
        _ssm_in_kernel,
        grid=(t // tm, n_main // tn),
        in_specs=[
            _x_tile_spec(tm, d, t // tm),
            pl.BlockSpec((1, d), lambda i, j: (0, 0)),
            _mod_spec(d, mod_base + 0 * batch, tpb),
            _mod_spec(d, mod_base + 1 * batch, tpb),
            pl.BlockSpec((tn, d), lambda i, j: (j, 0)),
            pl.BlockSpec((n_dt, d), lambda i, j: (n_main // n_dt, 0)),
        ],
        out_specs=[
            pl.BlockSpec((tm, tn), lambda i, j: (i, j)),
            pl.BlockSpec((tm, LANES), lambda i, j: (i, 0)),
        ],
        out_shape=[
            jax.ShapeDtypeStruct((t, n_main), BF16),
            jax.ShapeDtypeStruct((t, LANES), F32),
        ],
        scratch_shapes=[pltpu.VMEM((tm, d), BF16)],
        compiler_params=_cparams(2),
        name="ssm_in_proj",
    )(x, g, modr, modr, w_in_t, w_in_t)


def _split3_bf16(v):
    hi = v.astype(BF16)
    r1 = v - hi.astype(F32)
    mid = r1.astype(BF16)
    lo = (r1 - mid.astype(F32)).astype(BF16)
    return hi, mid, lo


def _ssd_kernel(z_ref, xr_ref, bcr_ref, dt_ref, cw_ref, cb_ref, dtb_ref, alog_ref, dskip_ref, ng_ref, e_ref, shift_ref,
                y_ref, state, tail, xs_f, xs_b, bc, ex, y_scr):
    q = SSM_CHUNK
    d_inner = xr_ref.shape[1]
    gw = d_inner // SSM_GROUPS
    gs = SSM_STATE
    bc_split = SSM_GROUPS * gs

    @pl.when(pl.program_id(1) == 0)
    def _():
        state[...] = jnp.zeros_like(state)
        tail[...] = jnp.zeros_like(tail)

    for sub in range(xs_f.shape[0]):
        _ssd_chunk(sub, z_ref, xr_ref, bcr_ref, dt_ref, cw_ref, cb_ref, dtb_ref, alog_ref, dskip_ref, ng_ref, e_ref,
                   shift_ref, y_ref, state, tail, xs_f.at[sub], xs_b.at[sub], bc.at[sub], ex.at[sub], y_scr.at[sub])
    last = xs_f.shape[0] * q
    halo = tail.shape[0]
    tail[:, 0:d_inner] = xr_ref[last - halo:last, :]
    tail[:, d_inner:] = bcr_ref[last - halo:last, :]


def _ssd_chunk(sub, z_ref, xr_ref, bcr_ref, dt_ref, cw_ref, cb_ref, dtb_ref, alog_ref, dskip_ref, ng_ref, e_ref,
               shift_ref, y_ref, state, tail, xs_f, xs_b, bc, ex, y_scr):
    q = SSM_CHUNK
    d_inner = xr_ref.shape[1]
    gw = d_inner // SSM_GROUPS
    gs = SSM_STATE
    bc_split = SSM_GROUPS * gs
    rows = slice(sub * q, (sub + 1) * q)

    halo = tail.shape[0]
    prev_rows = slice(sub * q - halo, sub * q)
    cw_blk = 256
    for c0 in range(0, d_inner + 2 * bc_split, cw_blk):
        cols = slice(c0, c0 + cw_blk)
        src, scols = (xr_ref, cols) if c0 < d_inner else (bcr_ref, slice(c0 - d_inner, c0 - d_inner + cw_blk))
        cur = src[rows, scols]
        before = tail[:, cols] if sub == 0 else src[prev_rows, scols]
        ext = jnp.concatenate([before, cur], axis=0)
        sh = jnp.dot(shift_ref[...], ext, preferred_element_type=F32)
        acc = cb_ref[:, cols] + cw_ref[SSM_CONV - 1:SSM_CONV, cols] * cur.astype(F32)
        for k in range(SSM_CONV - 1):
            acc = acc + cw_ref[k:k + 1, cols] * sh[k * q:(k + 1) * q]
        act = acc * jax.nn.sigmoid(acc)
        if c0 < d_inner:
            xs_f[:, cols] = act
            xs_b[:, cols] = act.astype(BF16)
        else:
            bc[:, c0 - d_inner:c0 - d_inner + cw_blk] = act.astype(BF16)

    v = dt_ref[rows, :] + dtb_ref[...]
    dt = jnp.maximum(v, 0.0) + jnp.log(1.0 + jnp.exp(-jnp.abs(v)))
    a_neg = -jnp.exp(alog_ref[...])
    d_a = dt * a_neg
    row = lax.broadcasted_iota(jnp.int32, (q, q), 0)
    col = lax.broadcasted_iota(jnp.int32, (q, q), 1)
    causal = row >= col
    tril = jnp.where(causal, 1.0, 0.0).astype(BF16)
    a_cum = None
    for part in _split3_bf16(d_a):
        term = jnp.dot(tril, part, preferred_element_type=F32)
        a_cum = term if a_cum is None else a_cum + term
    a2 = a_cum * LOG2E
    b2_t = (a2 - jnp.log2(dt)).T
    a_end = a_cum[q - 1:q, :]
    ea = jnp.exp(a_cum)
    wgt = dt * jnp.exp(a_end - a_cum)
    ea_hi = ea.astype(BF16)
    ea_lo = (ea - ea_hi.astype(F32)).astype(BF16)
    top = jnp.concatenate([ea_hi, ea_lo], axis=1)
    bot = jnp.concatenate([wgt.astype(BF16), jnp.zeros((q, LANES), BF16)], axis=1)
    ex[...] = jnp.dot(jnp.concatenate([top, bot], axis=0), e_ref[...], preferred_element_type=F32)

    lane = lax.broadcasted_iota(jnp.int32, (q, LANES), 1)
    lo_half = lane < SSM_HEAD_DIM
    contract_last = (((1,), (1,)), ((), ()))
    contract_first = (((0,), (0,)), ((), ()))
    heads_per_group = gw // SSM_HEAD_DIM

    def group_body(g):
        gcols = slice(g * gw, (g + 1) * gw)
        b_g = bc[:, g * gs:(g + 1) * gs]
        c_g = bc[:, bc_split + g * gs:bc_split + (g + 1) * gs]
        cb = lax.dot_general(c_g, b_g, contract_last, preferred_element_type=F32)
        s_in = state[g]
        y_off = jnp.dot(c_g, s_in.astype(BF16), preferred_element_type=F32)
        for pr in range(heads_per_group // 2):
            pcols = slice(g * gw + pr * LANES, g * gw + (pr + 1) * LANES)
            x_pair = xs_b[:, pcols]
            h0 = g * heads_per_group + 2 * pr
            halves = []
            for h in (h0, h0 + 1):
                diff = a2[:, h:h + 1] - b2_t[h:h + 1, :]
                mat = (cb * jnp.exp2(jnp.where(causal, diff, NEG_BIG))).astype(BF16)
                halves.append(jnp.dot(mat, x_pair, preferred_element_type=F32))
            y_diag = jnp.where(lo_half, halves[0], halves[1])
            y_scr[:, pcols] = (y_diag + y_off[:, pr * LANES:(pr + 1) * LANES] * ex[0:q, pcols]
                               + dskip_ref[:, pcols] * xs_f[:, pcols])
        xw = (xs_f[:, gcols] * ex[q:2 * q, gcols]).astype(BF16)
        upd = lax.dot_general(b_g, xw, contract_first, preferred_element_type=F32)
        state[g] = s_in * ex[q - 1:q, gcols] + upd

        zg = z_ref[rows, gcols].astype(F32)
        yz = y_scr[:, gcols] * (zg * jax.nn.sigmoid(zg))
        ms = jnp.mean(yz * yz, axis=-1, keepdims=True)
        y_ref[rows, gcols] = (yz * lax.rsqrt(ms + NORM_EPS) * ng_ref[:, gcols]).astype(y_ref.dtype)

    for g in range(SSM_GROUPS):
        group_body(g)


def _ssd(zxbc, dt_raw, conv_w, conv_b, dt_bias_pad, a_log_pad, d_exp, norm_g, expand, batch, seq):
    t = zxbc.shape[0]
    d_inner = d_exp.shape[1]
    n_bc = zxbc.shape[1] - 2 * d_inner
    n_conv = d_inner + n_bc
    n_sub = SSD_CHUNKS_PER_STEP
    q = SSM_CHUNK
    rows = n_sub * q
    nc = seq // rows
    gw = d_inner // SSM_GROUPS
    row_map = lambda b, c: (b * nc + c, 0)
    const = lambda b, c: (0, 0)
    halo = CONV_TAIL_ROWS
    src = jnp.arange(q)[None, :, None] + (halo - (SSM_CONV - 1)) + jnp.arange(SSM_CONV - 1)[:, None, None]
    shift = (jnp.arange(halo + q)[None, None, :] == src).astype(BF16).reshape((SSM_CONV - 1) * q, halo + q)
    return pl.pallas_call(
        _ssd_kernel,
        grid=(batch, nc),
        in_specs=[
            pl.BlockSpec((rows, d_inner), row_map),
            pl.BlockSpec((rows, d_inner), lambda b, c: (b * nc + c, 1)),
            pl.BlockSpec((rows, n_bc), lambda b, c: (b * nc + c, 2 * d_inner // n_bc)),
            pl.BlockSpec((rows, LANES), row_map),
            pl.BlockSpec((SSM_CONV, n_conv), const),
            pl.BlockSpec((1, n_conv), const),
            pl.BlockSpec((1, LANES), const),
            pl.BlockSpec((1, LANES), const),
            pl.BlockSpec((1, d_inner), const),
            pl.BlockSpec((1, d_inner), const),
            pl.BlockSpec((2 * LANES, d_inner), const),
            pl.BlockSpec(shift.shape, const),
        ],
        out_specs=pl.BlockSpec((rows, d_inner), row_map),
        out_shape=jax.ShapeDtypeStruct((t, d_inner), BF16),
        scratch_shapes=[
            pltpu.VMEM((SSM_GROUPS, SSM_STATE, gw), F32),
            pltpu.VMEM((halo, n_conv), BF16),
            pltpu.VMEM((n_sub, q, d_inner), F32),
            pltpu.VMEM((n_sub, q, d_inner), BF16),
            pltpu.VMEM((n_sub, q, n_bc), BF16),
            pltpu.VMEM((n_sub, 2 * q, d_inner), F32),
            pltpu.VMEM((n_sub, q, d_inner), F32),
        ],
        compiler_params=_cparams(2),
        name="ssd_scan",
    )(zxbc, zxbc, zxbc, dt_raw, conv_w, conv_b, dt_bias_pad, a_log_pad, d_exp, norm_g, expand, shift)


def _mm_res_kernel(a_ref, w_ref, x_ref, gate_ref, o_ref):
    y = jnp.dot(a_ref[...], w_ref[...].astype(BF16), preferred_element_type=F32)
    o_ref[...] = x_ref[...] + gate_ref[...] * y


def _matmul_residual(a, w, x, modr, gate_base, seq):
    t, k = a.shape
    d = x.shape[1]
    tm, tn = ROW_TILE, 512
    tpb = seq // tm
    gate_spec = pl.BlockSpec((None, 1, tn), lambda i, j: (gate_base + i // tpb, 0, j))
    return pl.pallas_call(
        _mm_res_kernel,
        grid=(t // tm, d // tn),
        in_specs=[
            pl.BlockSpec((tm, k), lambda i, j: (i, 0)),
            pl.BlockSpec((k, tn), lambda i, j: (0, j)),
            pl.BlockSpec((tm, tn), lambda i, j: (i, j)),
            gate_spec,
        ],
        out_specs=pl.BlockSpec((tm, tn), lambda i, j: (i, j)),
        out_shape=jax.ShapeDtypeStruct((t, d), F32),
        compiler_params=_cparams(2),
        name="ssm_out_proj",
    )(a, w, x, modr)


def kernel(x, c, ada_w, ada_b, norm_mix, norm_ffn, ffn_w1, ffn_w2, even_w_in, pool_w, pool_scale, even_w_out,
           ssm_w_in, ssm_conv_w, ssm_conv_b, ssm_dt_bias, ssm_a_log, ssm_d, ssm_norm, ssm_w_out, final_norm):
    batch, seq, d = x.shape
    depth = ada_w.shape[0]
    t = batch * seq
    xf = x.reshape(t, d)

    c_pad = jnp.pad(c, ((0, 8 - batch), (0, 0)))
    mod = _ada_mod(c_pad, ada_w, ada_b)[:, :batch]
    modr = mod.reshape(depth, batch, 6, d).transpose(0, 2, 1, 3).reshape(depth * 6 * batch, 1, d)

    d_inner = ssm_w_out.shape[1]
    n_heads = ssm_dt_bias.shape[1]
    head_of_channel = jnp.arange(d_inner) // SSM_HEAD_DIM
    expand1 = (jnp.arange(LANES)[:, None] == head_of_channel[None, :]).astype(BF16)
    expand = jnp.concatenate([expand1, expand1], axis=0)

    for i in range(depth):
        base = i * 6 * batch
        j = i // 2
        g_mix = norm_mix[i].reshape(1, d)
        g_ffn = norm_ffn[i].reshape(1, d)
        if i % 2 == 0:
            qkv, u = _even_in_proj(xf, g_mix, modr, base, even_w_in[j], seq)
            merged = None
            for gi in range(N_GROUPS):
                merged = _attn_group(qkv, merged, gi, batch, seq)
            xf = _even_out_proj(merged, u, pool_w[j], pool_scale[j].reshape(1, POOL_W), even_w_out[j], xf, modr,
                                base + 2 * batch, seq)
        else:
            w_in_t = jnp.swapaxes(ssm_w_in[j], 0, 1)
            n_main = w_in_t.shape[0] - n_heads
            pad_h = ((0, 0), (0, LANES - n_heads))
            zxbc, dt_raw = _ssm_in_proj(xf, g_mix, modr, base, w_in_t, n_main, seq)
            y = _ssd(zxbc, dt_raw, ssm_conv_w[j], ssm_conv_b[j].reshape(1, -1),
                     jnp.pad(ssm_dt_bias[j].reshape(1, -1), pad_h),
                     jnp.pad(ssm_a_log[j].reshape(1, -1), pad_h),
                     jnp.repeat(ssm_d[j], SSM_HEAD_DIM).reshape(1, d_inner), ssm_norm[j].reshape(1, d_inner),
                     expand, batch, seq)
            xf = _matmul_residual(y, ssm_w_out[j], xf, modr, base + 2 * batch, seq)
        xf = _ffn(xf, g_ffn, modr, base, ffn_w1, ffn_w2, i, final_norm.reshape(1, d), seq,
                  final_norm=(i == depth - 1))
    return xf.reshape(batch, seq, d)
```

```python
import functools

import jax
import jax.numpy as jnp
from jax import lax
from jax.experimental import pallas as pl
from jax.experimental.pallas import tpu as pltpu

F32 = jnp.float32
BF16 = jnp.bfloat16

NORM_EPS = 1e-6
NEG_BIG = -1e30
LOG2E = 1.4426950408889634

ATTN_GROUPS = ((128, 1), (512, 4), (2048, 16))
ATTN_HEADS = 8
ATTN_HEAD_DIM = 64
ATTN_BLOCK = 128
ATTN_UNITS_PER_STEP = 16
ATTN_GW = ATTN_HEADS * ATTN_HEAD_DIM
N_GROUPS = len(ATTN_GROUPS)
QKV_W = 3 * N_GROUPS * ATTN_GW
POOL_WINDOWS = (2, 4, 8, 16)
POOL_GW = 128
POOL_W = len(POOL_WINDOWS) * POOL_GW
POOL_HALO = 16

SSM_HEAD_DIM = 64
SSM_GROUPS = 8
SSM_STATE = 128
SSM_CONV = 4
SSM_CHUNK = 128
SSD_CHUNKS_PER_STEP = 2
CONV_TAIL_ROWS = 16

VMEM_LIMIT = 56 * 1024 * 1024
BIG_TILE_VMEM_LIMIT = 60 * 1024 * 1024
LANES = 128
ROW_TILE = 1024


def _cparams(n_axes, vmem_limit=VMEM_LIMIT):
    return pltpu.CompilerParams(dimension_semantics=("arbitrary",) * n_axes, vmem_limit_bytes=vmem_limit)


def _ada_kernel(c_ref, w_ref, b_ref, o_ref):
    c = c_ref[...]
    cond = (c * jax.nn.sigmoid(c)).astype(BF16)
    w = w_ref[...].astype(BF16)
    o_ref[...] = jnp.dot(cond, w, preferred_element_type=F32) + b_ref[...]


def _ada_mod(c_pad, ada_w, ada_b):
    depth, d, n = ada_w.shape
    rows = c_pad.shape[0]
    tn = 1024
    return pl.pallas_call(
        _ada_kernel,
        grid=(depth, n // tn),
        in_specs=[
            pl.BlockSpec((rows, d), lambda l, j: (0, 0)),
            pl.BlockSpec((None, d, tn), lambda l, j: (l, 0, j)),
            pl.BlockSpec((None, 1, tn), lambda l, j: (l, 0, j)),
        ],
        out_specs=pl.BlockSpec((None, rows, tn), lambda l, j: (l, 0, j)),
        out_shape=jax.ShapeDtypeStruct((depth, rows, n), F32),
        compiler_params=_cparams(2),
        name="ada_mod",
    )(c_pad, ada_w, ada_b.reshape(depth, 1, n))


def _norm_mod_rows(x_ref, g_ref, sh_ref, sc_ref, store, rows_per_step=256):
    tm = x_ref.shape[0]
    gain = g_ref[...] * (1.0 + sc_ref[...])
    shift = sh_ref[...]

    def body(i, carry):
        r0 = pl.multiple_of(i * rows_per_step, rows_per_step)
        x = x_ref[pl.ds(r0, rows_per_step), :]
        ms = jnp.mean(x * x, axis=-1, keepdims=True)
        store(r0, x * lax.rsqrt(ms + NORM_EPS) * gain + shift, rows_per_step)
        return carry

    lax.fori_loop(0, tm // rows_per_step, body, 0)


def _norm_mod_to_scratch_unrolled(x_ref, g_ref, sh_ref, sc_ref, h_scr, rows_per_step=64):
    gain = g_ref[...] * (1.0 + sc_ref[...])
    shift = sh_ref[...]
    for r0 in range(0, x_ref.shape[0], rows_per_step):
        x = x_ref[r0:r0 + rows_per_step, :]
        ms = jnp.mean(x * x, axis=-1, keepdims=True)
        h_scr[r0:r0 + rows_per_step, :] = (x * lax.rsqrt(ms + NORM_EPS) * gain + shift).astype(BF16)


def _x_tile_spec(tm, d, n_row_tiles, advance_at=1):
    return pl.BlockSpec((tm, d), lambda i, j: (jnp.minimum(i + jnp.minimum(j // advance_at, 1), n_row_tiles - 1), 0))


def _mod_spec(d, idx_base, tiles_per_batch):
    return pl.BlockSpec((None, 1, d), lambda i, *_: (idx_base + i // tiles_per_batch, 0, 0))


def _even_in_kernel(x_ref, g_ref, sh_ref, sc_ref, w_ref, wu_ref, qkv_ref, u_ref, hf, h3):
    j = pl.program_id(1)
    tm = x_ref.shape[0]
    n_slabs = hf.shape[0]

    @pl.when(j == 0)
    def _():
        def store(r0, h, n):
            h3[0, pl.ds(r0, n), :] = h.astype(BF16)
            for c in range(n_slabs):
                hf[c, pl.ds(r0, n), :] = h[:, c * LANES:(c + 1) * LANES]

        _norm_mod_rows(x_ref, g_ref, sh_ref, sc_ref, store)
        u_ref[...] = jnp.dot(h3[0], wu_ref[...].astype(BF16), preferred_element_type=F32)
        qkv_ref[...] = jnp.dot(h3[0], w_ref[...].astype(BF16), preferred_element_type=F32).astype(BF16)
        for gi, (_, dil) in enumerate(ATTN_GROUPS):
            if dil == 1:
                continue
            rows = tm // dil
            for c in range(n_slabs):
                for r in range(dil):
                    piece = hf[c, pl.ds(r, rows, stride=dil), :]
                    h3[gi, r * rows:(r + 1) * rows, c * LANES:(c + 1) * LANES] = piece.astype(BF16)

    @pl.when(j > 0)
    def _():
        gi = j // N_GROUPS
        qkv_ref[...] = jnp.dot(h3[gi], w_ref[...].astype(BF16), preferred_element_type=F32).astype(BF16)


def _even_in_proj(x, g, modr, mod_base, w_in, seq):
    t, d = x.shape
    tm, tn = ROW_TILE, ATTN_GW
    tpb = seq // tm
    batch = t // seq
    n_u = w_in.shape[1] - QKV_W
    return pl.pallas_call(
        _even_in_kernel,
        grid=(t // tm, QKV_W // tn),
        in_specs=[
            _x_tile_spec(tm, d, t // tm),
            pl.BlockSpec((1, d), lambda i, j: (0, 0)),
            _mod_spec(d, mod_base + 0 * batch, tpb),
            _mod_spec(d, mod_base + 1 * batch, tpb),
            pl.BlockSpec((d, tn), lambda i, j: (0, (j % N_GROUPS) * N_GROUPS + j // N_GROUPS)),
            pl.BlockSpec((d, n_u), lambda i, j: (0, QKV_W // n_u), pipeline_mode=pl.Buffered(1)),
        ],
        out_specs=[
            pl.BlockSpec((tm, tn), lambda i, j: (i, j)),
            pl.BlockSpec((tm, n_u), lambda i, j: (i, 0)),
        ],
        out_shape=[
            jax.ShapeDtypeStruct((t, QKV_W), BF16),
            jax.ShapeDtypeStruct((t, n_u), F32),
        ],
        scratch_shapes=[pltpu.VMEM((d // LANES, tm, LANES), F32), pltpu.VMEM((N_GROUPS, tm, d), BF16)],
        compiler_params=_cparams(2, BIG_TILE_VMEM_LIMIT),
        name="even_in_proj",
    )(x, g, modr, modr, w_in, w_in)


def _unit_rows(ref, u, sl, dil):
    if dil == 1:
        return ref[u * ATTN_BLOCK:(u + 1) * ATTN_BLOCK, sl]
    if len(ref.shape) == 3:
        return ref[u, :, sl]
    return jnp.concatenate([ref[p, u, :, sl] for p in range(ref.shape[0])], axis=0)


def _attn_kernel(*refs, first, last, dil, units):
    if first:
        q_ref, k_ref, v_ref, o_ref, l_ref, kk, vv = refs
        op_ref = lp_ref = None
    elif last:
        q_ref, k_ref, v_ref, op_ref, lp_ref, o_ref, kk, vv = refs
        l_ref = None
    else:
        q_ref, k_ref, v_ref, op_ref, lp_ref, o_ref, l_ref, kk, vv = refs
    step_n = pl.program_id(1)
    step_r = pl.program_id(2)
    blk = ATTN_BLOCK
    full = slice(None)

    @pl.when(step_n == 0)
    def _():
        n_zero = units if dil > 1 else 1
        kk[pl.ds(step_r * n_zero, n_zero), 0:blk, :] = jnp.zeros((n_zero, blk, ATTN_GW), BF16)
        vv[pl.ds(step_r * n_zero, n_zero), 0:blk, :] = jnp.zeros((n_zero, blk, ATTN_GW), BF16)

    qi = lax.broadcasted_iota(jnp.int32, (2 * blk, 2 * blk), 0) & (blk - 1)
    kj = lax.broadcasted_iota(jnp.int32, (2 * blk, 2 * blk), 1)
    cur_bias = jnp.where(kj - blk <= qi, 0.0, NEG_BIG)
    prev_ok = kj >= qi
    lane = lax.broadcasted_iota(jnp.int32, (blk, LANES), 1)
    lo_half = lane < ATTN_HEAD_DIM
    contract_last = (((1,), (1,)), ((), ()))

    for u in range(units):
        if dil > 1:
            r = step_r * units + u
            has_prev = step_n > 0
            rows_out = pl.ds(r, blk, stride=dil)
        else:
            r = 0
            has_prev = (step_n > 0) if u == 0 else True
            rows_out = slice(u * blk, (u + 1) * blk)
        prev_pen = 0.0 if has_prev is True else jnp.where(has_prev, 0.0, NEG_BIG)
        bias = jnp.where(kj < blk, jnp.where(prev_ok, prev_pen, NEG_BIG), cur_bias)
        k_cur = _unit_rows(k_ref, u, full, dil)
        v_cur = _unit_rows(v_ref, u, full, dil)
        kk[r, blk:2 * blk, :] = k_cur
        vv[r, blk:2 * blk, :] = v_cur
        for hp in range(ATTN_GW // LANES):
            sl = slice(hp * LANES, (hp + 1) * LANES)
            qp = _unit_rows(q_ref, u, sl, dil) * jnp.asarray(ATTN_HEAD_DIM ** -0.5, BF16)
            zero = jnp.zeros_like(qp)
            qm = jnp.concatenate([jnp.where(lo_half, qp, zero), jnp.where(lo_half, zero, qp)], axis=0)
            s = lax.dot_general(qm, kk[r, :, sl], contract_last, preferred_element_type=F32) + bias
            m = jnp.max(s, axis=-1, keepdims=True)
            p = jnp.exp(s - m)
            den = jnp.sum(p, axis=-1, keepdims=True)
            acc = jnp.dot(p.astype(BF16), vv[r, :, sl], preferred_element_type=F32)
            o2 = acc / den
            lse2 = jnp.broadcast_to(m + jnp.log(den), (2 * blk, LANES))
            o_new = jnp.where(lo_half, o2[0:blk], o2[blk:2 * blk])
            lse_new = jnp.where(lo_half, lse2[0:blk], lse2[blk:2 * blk])
            if not first:
                lse_old = lp_ref[hp, rows_out, :]
                o_old = op_ref[hp, rows_out, :]
                mx = jnp.maximum(lse_old, lse_new)
                w_old = jnp.exp(lse_old - mx)
                w_new = jnp.exp(lse_new - mx)
                tot = w_old + w_new
                o_new = (o_old * w_old + o_new * w_new) / tot
                lse_new = mx + jnp.log(tot)
            o_ref[hp, rows_out, :] = o_new
            if not last:
                l_ref[hp, rows_out, :] = lse_new
        kk[r, 0:blk, :] = k_cur
        vv[r, 0:blk, :] = v_cur


def _attn_group(qkv, prev, gi, batch, seq):
    dil = ATTN_GROUPS[gi][1]
    first, last = gi == 0, gi == N_GROUPS - 1
    t = batch * seq
    nb = seq // dil // ATTN_BLOCK
    units = ATTN_UNITS_PER_STEP if dil != 4 else 4
    n_pairs = ATTN_GW // LANES
    if dil == 1:
        qkv_v = qkv.reshape(batch, seq, QKV_W)
        grid = (batch, nb // units, 1)
        blk_shape = (None, units * ATTN_BLOCK, ATTN_GW)
        idx = lambda b, n, r, col: (b, n, col)
        o_spec = pl.BlockSpec((n_pairs, units * ATTN_BLOCK, LANES), lambda b, n, r: (0, b * (nb // units) + n, 0))
    else:
        chunk = ROW_TILE // dil
        qkv_v = qkv.reshape(batch, seq // ROW_TILE, dil, chunk, QKV_W)
        grid = (batch, nb, dil // units)
        if chunk >= ATTN_BLOCK:
            sub = chunk // ATTN_BLOCK
            blk_shape = (None, None, units, ATTN_BLOCK, ATTN_GW)
            idx = lambda b, n, r, col: (b, n // sub, r, n % sub, col)
        else:
            blk_shape = (None, ATTN_BLOCK // chunk, units, chunk, ATTN_GW)
            idx = lambda b, n, r, col: (b, n, r, 0, col)
        o_spec = pl.BlockSpec((n_pairs, ATTN_BLOCK * dil, LANES), lambda b, n, r: (0, b * nb + n, 0))
    in_specs = [pl.BlockSpec(blk_shape, functools.partial(lambda b, n, r, col: idx(b, n, r, col), col=gi * 3 + which))
                for which in range(3)]
    args = [qkv_v, qkv_v, qkv_v]
    if not first:
        in_specs += [o_spec, o_spec]
        args += list(prev)
    o_shape = jax.ShapeDtypeStruct((n_pairs, t, LANES), F32)
    outs = pl.pallas_call(
        functools.partial(_attn_kernel, first=first, last=last, dil=dil, units=units),
        grid=grid,
        in_specs=in_specs,
        out_specs=[o_spec] if last else [o_spec, o_spec],
        out_shape=[o_shape] if last else [o_shape, o_shape],
        scratch_shapes=[pltpu.VMEM((dil, 2 * ATTN_BLOCK, ATTN_GW), BF16),
                        pltpu.VMEM((dil, 2 * ATTN_BLOCK, ATTN_GW), BF16)],
        compiler_params=_cparams(3),
        name=f"dilated_attn_g{gi}",
    )(*args)
    return outs[0] if last else tuple(outs)


def _even_out_kernel(attn_ref, u_ref, halo_ref, pw_ref, ps_ref, w_ref, x_ref, gate_ref, o_ref, ue, a_scr, w_bf,
                     *, seq):
    tm = u_ref.shape[0]
    i = pl.program_id(0)

    @pl.when(i == 0)
    def _():
        w_bf[...] = w_ref[...].astype(BF16)

    row0 = (i * tm) % seq
    halo_ok = jnp.where(row0 > 0, 1.0, 0.0)
    ue[0:POOL_HALO, :] = halo_ref[...] * halo_ok
    ue[POOL_HALO:POOL_HALO + tm, :] = u_ref[...]
    for hp in range(ATTN_GW // LANES):
        a_scr[:, hp * LANES:(hp + 1) * LANES] = attn_ref[hp].astype(BF16)
    pos = row0 + lax.broadcasted_iota(jnp.int32, (tm, POOL_GW), 0)
    for gi, win in enumerate(POOL_WINDOWS):
        cols = slice(gi * POOL_GW, (gi + 1) * POOL_GW)
        tok = ue[POOL_HALO:POOL_HALO + tm, cols]
        acc = tok
        for back in range(1, win):
            acc = acc + ue[POOL_HALO - back:POOL_HALO - back + tm, cols]
        cnt = jnp.minimum(pos + 1, win).astype(F32)
        diff = acc / cnt - tok
        y = jnp.dot(diff.astype(BF16), pw_ref[gi].astype(BF16), preferred_element_type=F32) * ps_ref[:, cols]
        a_scr[:, ATTN_GW + gi * POOL_GW:ATTN_GW + (gi + 1) * POOL_GW] = y.astype(BF16)
    y = jnp.dot(a_scr[...], w_bf[...], preferred_element_type=F32)
    o_ref[...] = x_ref[...] + gate_ref[...] * y


def _even_out_proj(attn, u, pool_w, pool_scale, w_out, x, modr, gate_base, seq):
    t, d = x.shape
    tm = 512
    tpb = seq // tm
    k = w_out.shape[0]
    halo_blocks = tm // POOL_HALO
    n_pairs = attn.shape[0]
    return pl.pallas_call(
        functools.partial(_even_out_kernel, seq=seq),
        grid=(t // tm,),
        in_specs=[
            pl.BlockSpec((n_pairs, tm, LANES), lambda i: (0, i, 0)),
            pl.BlockSpec((tm, POOL_W), lambda i: (i, 0)),
            pl.BlockSpec((POOL_HALO, POOL_W), lambda i: (jnp.maximum(i * halo_blocks - 1, 0), 0)),
            pl.BlockSpec(pool_w.shape, lambda i: (0, 0, 0)),
            pl.BlockSpec((1, POOL_W), lambda i: (0, 0)),
            pl.BlockSpec((k, d), lambda i: (0, 0), pipeline_mode=pl.Buffered(1)),
            pl.BlockSpec((tm, d), lambda i: (i, 0)),
            _mod_spec(d, gate_base, tpb),
        ],
        out_specs=pl.BlockSpec((tm, d), lambda i: (i, 0)),
        out_shape=jax.ShapeDtypeStruct((t, d), F32),
        scratch_shapes=[pltpu.VMEM((tm + POOL_HALO, POOL_W), F32), pltpu.VMEM((tm, k), BF16),
                        pltpu.VMEM((k, d), BF16)],
        compiler_params=_cparams(1),
        name="even_out_proj",
    )(attn, u, u, pool_w, pool_scale, w_out, x, modr)


def _ffn_kernel(x_ref, g_ref, sh_ref, sc_ref, gate_ref, w1_ref, w2_ref, fin_ref, o_ref, h_scr, *, final_norm):
    j = pl.program_id(1)

    def hidden_tile():
        a = jnp.dot(h_scr[...], w1_ref[...].astype(BF16), preferred_element_type=F32)
        a = jnp.maximum(a, 0.0)
        return jnp.dot((a * a).astype(BF16), w2_ref[...].astype(BF16), preferred_element_type=F32)

    @pl.when(j == 0)
    def _():
        _norm_mod_to_scratch_unrolled(x_ref, g_ref, sh_ref, sc_ref, h_scr)
        o_ref[...] = x_ref[...] + gate_ref[...] * hidden_tile()

    @pl.when(j > 0)
    def _():
        o_ref[...] += gate_ref[...] * hidden_tile()

    if final_norm:
        @pl.when(j == pl.num_programs(1) - 1)
        def _():
            rows = 256
            fin = fin_ref[...]

            def body(i, carry):
                r0 = pl.multiple_of(i * rows, rows)
                y = o_ref[pl.ds(r0, rows), :]
                ms = jnp.mean(y * y, axis=-1, keepdims=True)
                o_ref[pl.ds(r0, rows), :] = y * lax.rsqrt(ms + NORM_EPS) * fin
                return carry

            lax.fori_loop(0, o_ref.shape[0] // rows, body, 0)


def _ffn(x, g, modr, mod_base, w1_all, w2_all, layer, fin, seq, final_norm):
    t, d = x.shape
    hdim = w1_all.shape[2]
    tm, th = ROW_TILE, 512
    tpb = seq // tm
    batch = t // seq
    return pl.pallas_call(
        functools.partial(_ffn_kernel, final_norm=final_norm),
        grid=(t // tm, hdim // th),
        in_specs=[
            _x_tile_spec(tm, d, t // tm, advance_at=2),
            pl.BlockSpec((1, d), lambda i, j: (0, 0)),
            _mod_spec(d, mod_base + 3 * batch, tpb),
            _mod_spec(d, mod_base + 4 * batch, tpb),
            _mod_spec(d, mod_base + 5 * batch, tpb),
            pl.BlockSpec((None, d, th), lambda i, j: (layer, 0, j)),
            pl.BlockSpec((None, th, d), lambda i, j: (layer, j, 0)),
            pl.BlockSpec((1, d), lambda i, j: (0, 0)),
        ],
        out_specs=pl.BlockSpec((tm, d), lambda i, j: (i, 0)),
        out_shape=jax.ShapeDtypeStruct((t, d), F32),
        scratch_shapes=[pltpu.VMEM((tm, d), BF16)],
        compiler_params=_cparams(2, BIG_TILE_VMEM_LIMIT),
        name="ffn_final" if final_norm else "ffn",
    )(x, g, modr, modr, modr, w1_all, w2_all, fin)


def _ssm_in_kernel(x_ref, g_ref, sh_ref, sc_ref, wt_ref, wdt_ref, zxbc_ref, dt_ref, h_scr):
    contract_last = (((1,), (1,)), ((), ()))

    def project():
        zxbc_ref[...] = lax.dot_general(h_scr[...], wt_ref[...].astype(BF16), contract_last,
                                        preferred_element_type=F32).astype(BF16)

    @pl.when(pl.program_id(1) == 0)
    def _():
        _norm_mod_to_scratch_unrolled(x_ref, g_ref, sh_ref, sc_ref, h_scr)
        project()
        dt = lax.dot_general(h_scr[...], wdt_ref[...].astype(BF16), contract_last, preferred_element_type=F32)
        dt_ref[...] = jnp.concatenate([dt, jnp.zeros((dt.shape[0], LANES - dt.shape[1]), F32)], axis=1)

    @pl.when(pl.program_id(1) > 0)
    def _():
        project()


def _ssm_in_proj(x, g, modr, mod_base, w_in_t, n_main, seq):
    t, d = x.shape
    tm, tn = ROW_TILE, 1024
    tpb = seq // tm
    batch = t // seq
    n_dt = w_in_t.shape[0] - n_main
    return pl.pallas_call(
        _ssm_in_kernel,
        grid=(t // tm, n_main // tn),
        in_specs=[
            _x_tile_spec(tm, d, t // tm),
            pl.BlockSpec((1, d), lambda i, j: (0, 0)),
            _mod_spec(d, mod_base + 0 * batch, tpb),
            _mod_spec(d, mod_base + 1 * batch, tpb),
            pl.BlockSpec((tn, d), lambda i, j: (j, 0)),
            pl.BlockSpec((n_dt, d), lambda i, j: (n_main // n_dt, 0)),
        ],
        out_specs=[
            pl.BlockSpec((tm, tn), lambda i, j: (i, j)),
            pl.BlockSpec((tm, LANES), lambda i, j: (i, 0)),
        ],
        out_shape=[
            jax.ShapeDtypeStruct((t, n_main), BF16),
            jax.ShapeDtypeStruct((t, LANES), F32),
        ],
        scratch_shapes=[pltpu.VMEM((tm, d), BF16)],
        compiler_params=_cparams(2),
        name="ssm_in_proj",
    )(x, g, modr, modr, w_in_t, w_in_t)


def _split3_bf16(v):
    hi = v.astype(BF16)
    r1 = v - hi.astype(F32)
    mid = r1.astype(BF16)
    lo = (r1 - mid.astype(F32)).astype(BF16)
    return hi, mid, lo


def _ssd_kernel(z_ref, xr_ref, bcr_ref, dt_ref, cw_ref, cb_ref, dtb_ref, alog_ref, dskip_ref, ng_ref, e_ref, shift_ref,
                y_ref, state, tail, xs_f, xs_b, bc, ex, y_scr):
    q = SSM_CHUNK
    d_inner = xr_ref.shape[1]
    gw = d_inner // SSM_GROUPS
    gs = SSM_STATE
    bc_split = SSM_GROUPS * gs

    @pl.when(pl.program_id(1) == 0)
    def _():
        state[...] = jnp.zeros_like(state)
        tail[...] = jnp.zeros_like(tail)

    for sub in range(xs_f.shape[0]):
        _ssd_chunk(sub, z_ref, xr_ref, bcr_ref, dt_ref, cw_ref, cb_ref, dtb_ref, alog_ref, dskip_ref, ng_ref, e_ref,
                   shift_ref, y_ref, state, tail, xs_f.at[sub], xs_b.at[sub], bc.at[sub], ex.at[sub], y_scr.at[sub])
    last = xs_f.shape[0] * q
    halo = tail.shape[0]
    tail[:, 0:d_inner] = xr_ref[last - halo:last, :]
    tail[:, d_inner:] = bcr_ref[last - halo:last, :]


def _ssd_chunk(sub, z_ref, xr_ref, bcr_ref, dt_ref, cw_ref, cb_ref, dtb_ref, alog_ref, dskip_ref, ng_ref, e_ref,
               shift_ref, y_ref, state, tail, xs_f, xs_b, bc, ex, y_scr):
    q = SSM_CHUNK
    d_inner = xr_ref.shape[1]
    gw = d_inner // SSM_GROUPS
    gs = SSM_STATE
    bc_split = SSM_GROUPS * gs
    rows = slice(sub * q, (sub + 1) * q)

    halo = tail.shape[0]
    prev_rows = slice(sub * q - halo, sub * q)
    cw_blk = 256
    for c0 in range(0, d_inner + 2 * bc_split, cw_blk):
        cols = slice(c0, c0 + cw_blk)
        src, scols = (xr_ref, cols) if c0 < d_inner else (bcr_ref, slice(c0 - d_inner, c0 - d_inner + cw_blk))
        cur = src[rows, scols]
        before = tail[:, cols] if sub == 0 else src[prev_rows, scols]
        ext = jnp.concatenate([before, cur], axis=0)
        sh = jnp.dot(shift_ref[...], ext, preferred_element_type=F32)
        acc = cb_ref[:, cols] + cw_ref[SSM_CONV - 1:SSM_CONV, cols] * cur.astype(F32)
        for k in range(SSM_CONV - 1):
            acc = acc + cw_ref[k:k + 1, cols] * sh[k * q:(k + 1) * q]
        act = acc * jax.nn.sigmoid(acc)
        if c0 < d_inner:
            xs_f[:, cols] = act
            xs_b[:, cols] = act.astype(BF16)
        else:
            bc[:, c0 - d_inner:c0 - d_inner + cw_blk] = act.astype(BF16)

    v = dt_ref[rows, :] + dtb_ref[...]
    dt = jnp.maximum(v, 0.0) + jnp.log(1.0 + jnp.exp(-jnp.abs(v)))
    a_neg = -jnp.exp(alog_ref[...])
    d_a = dt * a_neg
    row = lax.broadcasted_iota(jnp.int32, (q, q), 0)
    col = lax.broadcasted_iota(jnp.int32, (q, q), 1)
    causal = row >= col
    tril = jnp.where(causal, 1.0, 0.0).astype(BF16)
    a_cum = None
    for part in _split3_bf16(d_a):
        term = jnp.dot(tril, part, preferred_element_type=F32)
        a_cum = term if a_cum is None else a_cum + term
    a2 = a_cum * LOG2E
    b2_t = (a2 - jnp.log2(dt)).T
    a_end = a_cum[q - 1:q, :]
    ea = jnp.exp(a_cum)
    wgt = dt * jnp.exp(a_end - a_cum)
    ea_hi = ea.astype(BF16)
    ea_lo = (ea - ea_hi.astype(F32)).astype(BF16)
    top = jnp.concatenate([ea_hi, ea_lo], axis=1)
    bot = jnp.concatenate([wgt.astype(BF16), jnp.zeros((q, LANES), BF16)], axis=1)
    ex[...] = jnp.dot(jnp.concatenate([top, bot], axis=0), e_ref[...], preferred_element_type=F32)

    lane = lax.broadcasted_iota(jnp.int32, (q, LANES), 1)
    lo_half = lane < SSM_HEAD_DIM
    contract_last = (((1,), (1,)), ((), ()))
    contract_first = (((0,), (0,)), ((), ()))
    heads_per_group = gw // SSM_HEAD_DIM

    def group_body(g):
        gcols = slice(g * gw, (g + 1) * gw)
        b_g = bc[:, g * gs:(g + 1) * gs]
        c_g = bc[:, bc_split + g * gs:bc_split + (g + 1) * gs]
        cb = lax.dot_general(c_g, b_g, contract_last, preferred_element_type=F32)
        s_in = state[g]
        y_off = jnp.dot(c_g, s_in.astype(BF16), preferred_element_type=F32)
        for pr in range(heads_per_group // 2):
            pcols = slice(g * gw + pr * LANES, g * gw + (pr + 1) * LANES)
            x_pair = xs_b[:, pcols]
            h0 = g * heads_per_group + 2 * pr
            halves = []
            for h in (h0, h0 + 1):
                diff = a2[:, h:h + 1] - b2_t[h:h + 1, :]
                mat = (cb * jnp.exp2(jnp.where(causal, diff, NEG_BIG))).astype(BF16)
                halves.append(jnp.dot(mat, x_pair, preferred_element_type=F32))
            y_diag = jnp.where(lo_half, halves[0], halves[1])
            y_scr[:, pcols] = (y_diag + y_off[:, pr * LANES:(pr + 1) * LANES] * ex[0:q, pcols]
                               + dskip_ref[:, pcols] * xs_f[:, pcols])
        xw = (xs_f[:, gcols] * ex[q:2 * q, gcols]).astype(BF16)
        upd = lax.dot_general(b_g, xw, contract_first, preferred_element_type=F32)
        state[g] = s_in * ex[q - 1:q, gcols] + upd

        zg = z_ref[rows, gcols].astype(F32)
        yz = y_scr[:, gcols] * (zg * jax.nn.sigmoid(zg))
        ms = jnp.mean(yz * yz, axis=-1, keepdims=True)
        y_ref[rows, gcols] = (yz * lax.rsqrt(ms + NORM_EPS) * ng_ref[:, gcols]).astype(y_ref.dtype)

    for g in range(SSM_GROUPS):
        group_body(g)


def _ssd(zxbc, dt_raw, conv_w, conv_b, dt_bias_pad, a_log_pad, d_exp, norm_g, expand, batch, seq):
    t = zxbc.shape[0]
    d_inner = d_exp.shape[1]
    n_bc = zxbc.shape[1] - 2 * d_inner
    n_conv = d_inner + n_bc
    n_sub = SSD_CHUNKS_PER_STEP
    q = SSM_CHUNK
    rows = n_sub * q
    nc = seq // rows
    gw = d_inner // SSM_GROUPS
    row_map = lambda b, c: (b * nc + c, 0)
    const = lambda b, c: (0, 0)
    halo = CONV_TAIL_ROWS
    src = jnp.arange(q)[None, :, None] + (halo - (SSM_CONV - 1)) + jnp.arange(SSM_CONV - 1)[:, None, None]
    shift = (jnp.arange(halo + q)[None, None, :] == src).astype(BF16).reshape((SSM_CONV - 1) * q, halo + q)
    return pl.pallas_call(
        _ssd_kernel,
        grid=(batch, nc),
        in_specs=[
            pl.BlockSpec((rows, d_inner), row_map),
            pl.BlockSpec((rows, d_inner), lambda b, c: (b * nc + c, 1)),
            pl.BlockSpec((rows, n_bc), lambda b, c: (b * nc + c, 2 * d_inner // n_bc)),
            pl.BlockSpec((rows, LANES), row_map),
            pl.BlockSpec((SSM_CONV, n_conv), const),
            pl.BlockSpec((1, n_conv), const),
            pl.BlockSpec((1, LANES), const),
            pl.BlockSpec((1, LANES), const),
            pl.BlockSpec((1, d_inner), const),
            pl.BlockSpec((1, d_inner), const),
            pl.BlockSpec((2 * LANES, d_inner), const),
            pl.BlockSpec(shift.shape, const),
        ],
        out_specs=pl.BlockSpec((rows, d_inner), row_map),
        out_shape=jax.ShapeDtypeStruct((t, d_inner), BF16),
        scratch_shapes=[
            pltpu.VMEM((SSM_GROUPS, SSM_STATE, gw), F32),
            pltpu.VMEM((halo, n_conv), BF16),
            pltpu.VMEM((n_sub, q, d_inner), F32),
            pltpu.VMEM((n_sub, q, d_inner), BF16),
            pltpu.VMEM((n_sub, q, n_bc), BF16),
            pltpu.VMEM((n_sub, 2 * q, d_inner), F32),
            pltpu.VMEM((n_sub, q, d_inner), F32),
        ],
        compiler_params=_cparams(2),
        name="ssd_scan",
    )(zxbc, zxbc, zxbc, dt_raw, conv_w, conv_b, dt_bias_pad, a_log_pad, d_exp, norm_g, expand, shift)


def _mm_res_kernel(a_ref, w_ref, x_ref, gate_ref, o_ref):
    y = jnp.dot(a_ref[...], w_ref[...].astype(BF16), preferred_element_type=F32)
    o_ref[...] = x_ref[...] + gate_ref[...] * y


def _matmul_residual(a, w, x, modr, gate_base, seq):
    t, k = a.shape
    d = x.shape[1]
    tm, tn = ROW_TILE, 512
    tpb = seq // tm
    gate_spec = pl.BlockSpec((None, 1, tn), lambda i, j: (gate_base + i // tpb, 0, j))
    return pl.pallas_call(
        _mm_res_kernel,
        grid=(t // tm, d // tn),
        in_specs=[
            pl.BlockSpec((tm, k), lambda i, j: (i, 0)),
            pl.BlockSpec((k, tn), lambda i, j: (0, j)),
            pl.BlockSpec((tm, tn), lambda i, j: (i, j)),
            gate_spec,
        ],
        out_specs=pl.BlockSpec((tm, tn), lambda i, j: (i, j)),
        out_shape=jax.ShapeDtypeStruct((t, d), F32),
        compiler_params=_cparams(2),
        name="ssm_out_proj",
    )(a, w, x, modr)


def kernel(x, c, ada_w, ada_b, norm_mix, norm_ffn, ffn_w1, ffn_w2, even_w_in, pool_w, pool_scale, even_w_out,
           ssm_w_in, ssm_conv_w, ssm_conv_b, ssm_dt_bias, ssm_a_log, ssm_d, ssm_norm, ssm_w_out, final_norm):
    batch, seq, d = x.shape
    depth = ada_w.shape[0]
    t = batch * seq
    xf = x.reshape(t, d)

    c_pad = jnp.pad(c, ((0, 8 - batch), (0, 0)))
    mod = _ada_mod(c_pad, ada_w, ada_b)[:, :batch]
    modr = mod.reshape(depth, batch, 6, d).transpose(0, 2, 1, 3).reshape(depth * 6 * batch, 1, d)

    d_inner = ssm_w_out.shape[1]
    n_heads = ssm_dt_bias.shape[1]
    head_of_channel = jnp.arange(d_inner) // SSM_HEAD_DIM
    expand1 = (jnp.arange(LANES)[:, None] == head_of_channel[None, :]).astype(BF16)
    expand = jnp.concatenate([expand1, expand1], axis=0)

    for i in range(depth):
        base = i * 6 * batch
        j = i // 2
        g_mix = norm_mix[i].reshape(1, d)
        g_ffn = norm_ffn[i].reshape(1, d)
        if i % 2 == 0:
            qkv, u = _even_in_proj(xf, g_mix, modr, base, even_w_in[j], seq)
            merged = None
            for gi in range(N_GROUPS):
                merged = _attn_group(qkv, merged, gi, batch, seq)
            xf = _even_out_proj(merged, u, pool_w[j], pool_scale[j].reshape(1, POOL_W), even_w_out[j], xf, modr,
                                base + 2 * batch, seq)
        else:
            w_in_t = jnp.swapaxes(ssm_w_in[j], 0, 1)
            n_main = w_in_t.shape[0] - n_heads
            pad_h = ((0, 0), (0, LANES - n_heads))
            zxbc, dt_raw = _ssm_in_proj(xf, g_mix, modr, base, w_in_t, n_main, seq)
            y = _ssd(zxbc, dt_raw, ssm_conv_w[j], ssm_conv_b[j].reshape(1, -1),
                     jnp.pad(ssm_dt_bias[j].reshape(1, -1), pad_h),
                     jnp.pad(ssm_a_log[j].reshape(1, -1), pad_h),
                     jnp.repeat(ssm_d[j], SSM_HEAD_DIM).reshape(1, d_inner), ssm_norm[j].reshape(1, d_inner),
                     expand, batch, seq)
            xf = _matmul_residual(y, ssm_w_out[j], xf, modr, base + 2 * batch, seq)
        xf = _ffn(xf, g_ffn, modr, base, ffn_w1, ffn_w2, i, final_norm.reshape(1, d), seq,
                  final_norm=(i == depth - 1))
    return xf.reshape(batch, seq, d)
```

```python
import functools

import jax
import jax.numpy as jnp
from jax import lax
from jax.experimental import pallas as pl
from jax.experimental.pallas import tpu as pltpu

F32 = jnp.float32
BF16 = jnp.bfloat16

NORM_EPS = 1e-6
NEG_BIG = -1e30
LOG2E = 1.4426950408889634

ATTN_GROUPS = ((128, 1), (512, 4), (2048, 16))
ATTN_HEADS = 8
ATTN_HEAD_DIM = 64
ATTN_BLOCK = 128
ATTN_UNITS_PER_STEP = 8
ATTN_GW = ATTN_HEADS * ATTN_HEAD_DIM
N_GROUPS = len(ATTN_GROUPS)
QKV_W = 3 * N_GROUPS * ATTN_GW
POOL_WINDOWS = (2, 4, 8, 16)
POOL_GW = 128
POOL_W = len(POOL_WINDOWS) * POOL_GW
POOL_HALO = 16

SSM_HEAD_DIM = 64
SSM_GROUPS = 8
SSM_STATE = 128
SSM_CONV = 4
SSM_CHUNK = 128
SSD_CHUNKS_PER_STEP = 2
CONV_TAIL_ROWS = 16

VMEM_LIMIT = 56 * 1024 * 1024
BIG_TILE_VMEM_LIMIT = 60 * 1024 * 1024
LANES = 128
ROW_TILE = 1024


def _cparams(n_axes, vmem_limit=VMEM_LIMIT):
    return pltpu.CompilerParams(dimension_semantics=("arbitrary",) * n_axes, vmem_limit_bytes=vmem_limit)


def _ada_kernel(c_ref, w_ref, b_ref, o_ref):
    c = c_ref[...]
    cond = (c * jax.nn.sigmoid(c)).astype(BF16)
    w = w_ref[...].astype(BF16)
    o_ref[...] = jnp.dot(cond, w, preferred_element_type=F32) + b_ref[...]


def _ada_mod(c_pad, ada_w, ada_b):
    depth, d, n = ada_w.shape
    rows = c_pad.shape[0]
    tn = 1024
    return pl.pallas_call(
        _ada_kernel,
        grid=(depth, n // tn),
        in_specs=[
            pl.BlockSpec((rows, d), lambda l, j: (0, 0)),
            pl.BlockSpec((None, d, tn), lambda l, j: (l, 0, j)),
            pl.BlockSpec((None, 1, tn), lambda l, j: (l, 0, j)),
        ],
        out_specs=pl.BlockSpec((None, rows, tn), lambda l, j: (l, 0, j)),
        out_shape=jax.ShapeDtypeStruct((depth, rows, n), F32),
        compiler_params=_cparams(2),
        name="ada_mod",
    )(c_pad, ada_w, ada_b.reshape(depth, 1, n))


def _norm_mod_rows(x_ref, g_ref, sh_ref, sc_ref, store, rows_per_step=256):
    tm = x_ref.shape[0]
    gain = g_ref[...] * (1.0 + sc_ref[...])
    shift = sh_ref[...]

    def body(i, carry):
        r0 = pl.multiple_of(i * rows_per_step, rows_per_step)
        x = x_ref[pl.ds(r0, rows_per_step), :]
        ms = jnp.mean(x * x, axis=-1, keepdims=True)
        store(r0, x * lax.rsqrt(ms + NORM_EPS) * gain + shift, rows_per_step)
        return carry

    lax.fori_loop(0, tm // rows_per_step, body, 0)


def _norm_mod_to_scratch_unrolled(x_ref, g_ref, sh_ref, sc_ref, h_scr, rows_per_step=64):
    gain = g_ref[...] * (1.0 + sc_ref[...])
    shift = sh_ref[...]
    x_parts = x_ref if isinstance(x_ref, (tuple, list)) else (x_ref,)
    for r0 in range(0, x_parts[0].shape[0], rows_per_step):
        x = jnp.concatenate([part[r0:r0 + rows_per_step, :] for part in x_parts], axis=1)
        ms = jnp.mean(x * x, axis=-1, keepdims=True)
        h_scr[r0:r0 + rows_per_step, :] = (x * lax.rsqrt(ms + NORM_EPS) * gain + shift).astype(BF16)


def _x_tile_spec(tm, d, n_row_tiles, advance_at=1, col_block=0):
    return pl.BlockSpec(
        (tm, d), lambda i, j: (jnp.minimum(i + jnp.minimum(j // advance_at, 1), n_row_tiles - 1), col_block))


def _mod_spec(d, idx_base, tiles_per_batch):
    return pl.BlockSpec((None, 1, d), lambda i, *_: (idx_base + i // tiles_per_batch, 0, 0))


def _even_in_kernel(x_ref, g_ref, sh_ref, sc_ref, w_ref, wu_ref, qkv_ref, u_ref, hf, h3):
    j = pl.program_id(1)
    tm = x_ref.shape[0]
    n_slabs = hf.shape[0]

    @pl.when(j == 0)
    def _():
        def store(r0, h, n):
            h3[0, pl.ds(r0, n), :] = h.astype(BF16)
            for c in range(n_slabs):
                hf[c, pl.ds(r0, n), :] = h[:, c * LANES:(c + 1) * LANES]

        _norm_mod_rows(x_ref, g_ref, sh_ref, sc_ref, store)
        u_ref[...] = jnp.dot(h3[0], wu_ref[...].astype(BF16), preferred_element_type=F32)
        qkv_ref[...] = jnp.dot(h3[0], w_ref[...].astype(BF16), preferred_element_type=F32).astype(BF16)
        for gi, (_, dil) in enumerate(ATTN_GROUPS):
            if dil == 1:
                continue
            rows = tm // dil
            for c in range(n_slabs):
                for r in range(dil):
                    piece = hf[c, pl.ds(r, rows, stride=dil), :]
                    h3[gi, r * rows:(r + 1) * rows, c * LANES:(c + 1) * LANES] = piece.astype(BF16)

    @pl.when(j > 0)
    def _():
        gi = j // N_GROUPS
        qkv_ref[...] = jnp.dot(h3[gi], w_ref[...].astype(BF16), preferred_element_type=F32).astype(BF16)


def _even_in_proj(x, g, modr, mod_base, w_in, seq):
    t, d = x.shape
    tm, tn = ROW_TILE, ATTN_GW
    tpb = seq // tm
    batch = t // seq
    n_u = w_in.shape[1] - QKV_W
    return pl.pallas_call(
        _even_in_kernel,
        grid=(t // tm, QKV_W // tn),
        in_specs=[
            _x_tile_spec(tm, d, t // tm),
            pl.BlockSpec((1, d), lambda i, j: (0, 0)),
            _mod_spec(d, mod_base + 0 * batch, tpb),
            _mod_spec(d, mod_base + 1 * batch, tpb),
            pl.BlockSpec((d, tn), lambda i, j: (0, (j % N_GROUPS) * N_GROUPS + j // N_GROUPS)),
            pl.BlockSpec((d, n_u), lambda i, j: (0, QKV_W // n_u), pipeline_mode=pl.Buffered(1)),
        ],
        out_specs=[
            pl.BlockSpec((tm, tn), lambda i, j: (i, j)),
            pl.BlockSpec((tm, n_u), lambda i, j: (i, 0)),
        ],
        out_shape=[
            jax.ShapeDtypeStruct((t, QKV_W), BF16),
            jax.ShapeDtypeStruct((t, n_u), F32),
        ],
        scratch_shapes=[pltpu.VMEM((d // LANES, tm, LANES), F32), pltpu.VMEM((N_GROUPS, tm, d), BF16)],
        compiler_params=_cparams(2, BIG_TILE_VMEM_LIMIT),
        name="even_in_proj",
    )(x, g, modr, modr, w_in, w_in)


def _unit_rows(ref, u, sl, dil):
    if dil == 1:
        return ref[u * ATTN_BLOCK:(u + 1) * ATTN_BLOCK, sl]
    if len(ref.shape) == 3:
        return ref[u, :, sl]
    return jnp.concatenate([ref[p, u, :, sl] for p in range(ref.shape[0])], axis=0)


def _attn_kernel(*refs, first, last, dil, units):
    if first:
        q_ref, k_ref, v_ref, o_ref, l_ref, kk, vv = refs
        op_ref = lp_ref = None
    elif last:
        q_ref, k_ref, v_ref, op_ref, lp_ref, o_ref, kk, vv = refs
        l_ref = None
    else:
        q_ref, k_ref, v_ref, op_ref, lp_ref, o_ref, l_ref, kk, vv = refs
    step_n = pl.program_id(1)
    step_r = pl.program_id(2)
    blk = ATTN_BLOCK
    full = slice(None)

    @pl.when(step_n == 0)
    def _():
        n_zero = units if dil > 1 else 1
        kk[pl.ds(step_r * n_zero, n_zero), 0:blk, :] = jnp.zeros((n_zero, blk, ATTN_GW), BF16)
        vv[pl.ds(step_r * n_zero, n_zero), 0:blk, :] = jnp.zeros((n_zero, blk, ATTN_GW), BF16)

    qi = lax.broadcasted_iota(jnp.int32, (2 * blk, 2 * blk), 0) & (blk - 1)
    kj = lax.broadcasted_iota(jnp.int32, (2 * blk, 2 * blk), 1)
    cur_bias = jnp.where(kj - blk <= qi, 0.0, NEG_BIG)
    prev_ok = kj >= qi
    lane = lax.broadcasted_iota(jnp.int32, (blk, LANES), 1)
    lo_half = lane < ATTN_HEAD_DIM
    contract_last = (((1,), (1,)), ((), ()))

    for u in range(units):
        if dil > 1:
            r = step_r * units + u
            has_prev = step_n > 0
            rows_out = pl.ds(r, blk, stride=dil)
        else:
            r = 0
            has_prev = (step_n > 0) if u == 0 else True
            rows_out = slice(u * blk, (u + 1) * blk)
        prev_pen = 0.0 if has_prev is True else jnp.where(has_prev, 0.0, NEG_BIG)
        bias = jnp.where(kj < blk, jnp.where(prev_ok, prev_pen, NEG_BIG), cur_bias)
        k_cur = _unit_rows(k_ref, u, full, dil)
        v_cur = _unit_rows(v_ref, u, full, dil)
        kk[r, blk:2 * blk, :] = k_cur
        vv[r, blk:2 * blk, :] = v_cur
        for hp in range(ATTN_GW // LANES):
            sl = slice(hp * LANES, (hp + 1) * LANES)
            qp = _unit_rows(q_ref, u, sl, dil) * jnp.asarray(ATTN_HEAD_DIM ** -0.5, BF16)
            zero = jnp.zeros_like(qp)
            qm = jnp.concatenate([jnp.where(lo_half, qp, zero), jnp.where(lo_half, zero, qp)], axis=0)
            s = lax.dot_general(qm, kk[r, :, sl], contract_last, preferred_element_type=F32) + bias
            m = jnp.max(s, axis=-1, keepdims=True)
            p = jnp.exp(s - m)
            den = jnp.sum(p, axis=-1, keepdims=True)
            acc = jnp.dot(p.astype(BF16), vv[r, :, sl], preferred_element_type=F32)
            o2 = acc / den
            lse2 = jnp.broadcast_to(m + jnp.log(den), (2 * blk, LANES))
            o_new = jnp.where(lo_half, o2[0:blk], o2[blk:2 * blk])
            lse_new = jnp.where(lo_half, lse2[0:blk], lse2[blk:2 * blk])
            if not first:
                lse_old = lp_ref[hp, rows_out, :]
                o_old = op_ref[hp, rows_out, :]
                mx = jnp.maximum(lse_old, lse_new)
                w_old = jnp.exp(lse_old - mx)
                w_new = jnp.exp(lse_new - mx)
                tot = w_old + w_new
                o_new = (o_old * w_old + o_new * w_new) / tot
                lse_new = mx + jnp.log(tot)
            o_ref[hp, rows_out, :] = o_new
            if not last:
                l_ref[hp, rows_out, :] = lse_new
        kk[r, 0:blk, :] = k_cur
        vv[r, 0:blk, :] = v_cur


def _attn_group(qkv, prev, gi, batch, seq):
    dil = ATTN_GROUPS[gi][1]
    first, last = gi == 0, gi == N_GROUPS - 1
    t = batch * seq
    nb = seq // dil // ATTN_BLOCK
    units = ATTN_UNITS_PER_STEP if dil != 4 else 4
    n_pairs = ATTN_GW // LANES
    if dil == 1:
        qkv_v = qkv.reshape(batch, seq, QKV_W)
        grid = (batch, nb // units, 1)
        blk_shape = (None, units * ATTN_BLOCK, ATTN_GW)
        idx = lambda b, n, r, col: (b, n, col)
        o_spec = pl.BlockSpec((n_pairs, units * ATTN_BLOCK, LANES), lambda b, n, r: (0, b * (nb // units) + n, 0))
    else:
        chunk = ROW_TILE // dil
        qkv_v = qkv.reshape(batch, seq // ROW_TILE, dil, chunk, QKV_W)
        grid = (batch, nb, dil // units)
        if chunk >= ATTN_BLOCK:
            sub = chunk // ATTN_BLOCK
            blk_shape = (None, None, units, ATTN_BLOCK, ATTN_GW)
            idx = lambda b, n, r, col: (b, n // sub, r, n % sub, col)
        else:
            blk_shape = (None, ATTN_BLOCK // chunk, units, chunk, ATTN_GW)
            idx = lambda b, n, r, col: (b, n, r, 0, col)
        o_spec = pl.BlockSpec((n_pairs, ATTN_BLOCK * dil, LANES), lambda b, n, r: (0, b * nb + n, 0))
    in_specs = [pl.BlockSpec(blk_shape, functools.partial(lambda b, n, r, col: idx(b, n, r, col), col=gi * 3 + which))
                for which in range(3)]
    args = [qkv_v, qkv_v, qkv_v]
    if not first:
        in_specs += [o_spec, o_spec]
        args += list(prev)
    o_shape = jax.ShapeDtypeStruct((n_pairs, t, LANES), F32)
    outs = pl.pallas_call(
        functools.partial(_attn_kernel, first=first, last=last, dil=dil, units=units),
        grid=grid,
        in_specs=in_specs,
        out_specs=[o_spec] if last else [o_spec, o_spec],
        out_shape=[o_shape] if last else [o_shape, o_shape],
        scratch_shapes=[pltpu.VMEM((dil, 2 * ATTN_BLOCK, ATTN_GW), BF16),
                        pltpu.VMEM((dil, 2 * ATTN_BLOCK, ATTN_GW), BF16)],
        compiler_params=_cparams(3),
        name=f"dilated_attn_g{gi}",
    )(*args)
    return outs[0] if last else tuple(outs)


def _even_out_kernel(attn_ref, u_ref, halo_ref, pw_ref, ps_ref, w_ref, x_ref, gate_ref, o_ref, ue, a_scr, w_bf,
                     *, seq):
    tm = u_ref.shape[0]
    i = pl.program_id(0)

    @pl.when(i == 0)
    def _():
        w_bf[...] = w_ref[...].astype(BF16)

    row0 = (i * tm) % seq
    halo_ok = jnp.where(row0 > 0, 1.0, 0.0)
    ue[0:POOL_HALO, :] = halo_ref[...] * halo_ok
    ue[POOL_HALO:POOL_HALO + tm, :] = u_ref[...]
    for hp in range(ATTN_GW // LANES):
        a_scr[:, hp * LANES:(hp + 1) * LANES] = attn_ref[hp].astype(BF16)
    pos = row0 + lax.broadcasted_iota(jnp.int32, (tm, POOL_GW), 0)
    for gi, win in enumerate(POOL_WINDOWS):
        cols = slice(gi * POOL_GW, (gi + 1) * POOL_GW)
        tok = ue[POOL_HALO:POOL_HALO + tm, cols]
        acc = tok
        for back in range(1, win):
            acc = acc + ue[POOL_HALO - back:POOL_HALO - back + tm, cols]
        cnt = jnp.minimum(pos + 1, win).astype(F32)
        diff = acc / cnt - tok
        y = jnp.dot(diff.astype(BF16), pw_ref[gi].astype(BF16), preferred_element_type=F32) * ps_ref[:, cols]
        a_scr[:, ATTN_GW + gi * POOL_GW:ATTN_GW + (gi + 1) * POOL_GW] = y.astype(BF16)
    y = jnp.dot(a_scr[...], w_bf[...], preferred_element_type=F32)
    o_ref[...] = x_ref[...] + gate_ref[...] * y


def _even_out_proj(attn, u, pool_w, pool_scale, w_out, x, modr, gate_base, seq):
    t, d = x.shape
    tm = 512
    tpb = seq // tm
    k = w_out.shape[0]
    halo_blocks = tm // POOL_HALO
    n_pairs = attn.shape[0]
    return pl.pallas_call(
        functools.partial(_even_out_kernel, seq=seq),
        grid=(t // tm,),
        in_specs=[
            pl.BlockSpec((n_pairs, tm, LANES), lambda i: (0, i, 0)),
            pl.BlockSpec((tm, POOL_W), lambda i: (i, 0)),
            pl.BlockSpec((POOL_HALO, POOL_W), lambda i: (jnp.maximum(i * halo_blocks - 1, 0), 0)),
            pl.BlockSpec(pool_w.shape, lambda i: (0, 0, 0)),
            pl.BlockSpec((1, POOL_W), lambda i: (0, 0)),
            pl.BlockSpec((k, d), lambda i: (0, 0), pipeline_mode=pl.Buffered(1)),
            pl.BlockSpec((tm, d), lambda i: (i, 0)),
            _mod_spec(d, gate_base, tpb),
        ],
        out_specs=pl.BlockSpec((tm, d), lambda i: (i, 0)),
        out_shape=jax.ShapeDtypeStruct((t, d), F32),
        scratch_shapes=[pltpu.VMEM((tm + POOL_HALO, POOL_W), F32), pltpu.VMEM((tm, k), BF16),
                        pltpu.VMEM((k, d), BF16)],
        compiler_params=_cparams(1),
        name="even_out_proj",
    )(attn, u, u, pool_w, pool_scale, w_out, x, modr)


def _ffn_kernel(xa_ref, xb_ref, g_ref, sh_ref, sc_ref, gate_ref, w1_ref, w2_ref, fin_ref, o_ref, h_scr, *,
                final_norm):
    j = pl.program_id(1)
    half = xa_ref.shape[1]

    def hidden_tile():
        a = jnp.dot(h_scr[...], w1_ref[...].astype(BF16), preferred_element_type=F32)
        a = jnp.maximum(a, 0.0)
        return jnp.dot((a * a).astype(BF16), w2_ref[...].astype(BF16), preferred_element_type=F32)

    @pl.when(j == 0)
    def _():
        _norm_mod_to_scratch_unrolled((xa_ref, xb_ref), g_ref, sh_ref, sc_ref, h_scr)
        y = gate_ref[...] * hidden_tile()
        o_ref[:, 0:half] = xa_ref[...] + y[:, 0:half]
        o_ref[:, half:] = xb_ref[...] + y[:, half:]

    @pl.when(j > 0)
    def _():
        o_ref[...] += gate_ref[...] * hidden_tile()

    if final_norm:
        @pl.when(j == pl.num_programs(1) - 1)
        def _():
            rows = 256
            fin = fin_ref[...]

            def body(i, carry):
                r0 = pl.multiple_of(i * rows, rows)
                y = o_ref[pl.ds(r0, rows), :]
                ms = jnp.mean(y * y, axis=-1, keepdims=True)
                o_ref[pl.ds(r0, rows), :] = y * lax.rsqrt(ms + NORM_EPS) * fin
                return carry

            lax.fori_loop(0, o_ref.shape[0] // rows, body, 0)


def _ffn(x, g, modr, mod_base, w1_all, w2_all, layer, fin, seq, final_norm):
    t, d = x.shape
    hdim = w1_all.shape[2]
    tm, th = ROW_TILE, 512
    tpb = seq // tm
    batch = t // seq
    return pl.pallas_call(
        functools.partial(_ffn_kernel, final_norm=final_norm),
        grid=(t // tm, hdim // th),
        in_specs=[
            _x_tile_spec(tm, d // 2, t // tm, advance_at=2, col_block=0),
            _x_tile_spec(tm, d // 2, t // tm, advance_at=3, col_block=1),
            pl.BlockSpec((1, d), lambda i, j: (0, 0)),
            _mod_spec(d, mod_base + 3 * batch, tpb),
            _mod_spec(d, mod_base + 4 * batch, tpb),
            _mod_spec(d, mod_base + 5 * batch, tpb),
            pl.BlockSpec((None, d, th), lambda i, j: (layer, 0, j)),
            pl.BlockSpec((None, th, d), lambda i, j: (layer, j, 0)),
            pl.BlockSpec((1, d), lambda i, j: (0, 0)),
        ],
        out_specs=pl.BlockSpec((tm, d), lambda i, j: (i, 0)),
        out_shape=jax.ShapeDtypeStruct((t, d), F32),
        scratch_shapes=[pltpu.VMEM((tm, d), BF16)],
        compiler_params=_cparams(2, BIG_TILE_VMEM_LIMIT),
        name="ffn_final" if final_norm else "ffn",
    )(x, x, g, modr, modr, modr, w1_all, w2_all, fin)


def _ssm_in_kernel(x_ref, g_ref, sh_ref, sc_ref, wt_ref, wdt_ref, zxbc_ref, dt_ref, h_scr):
    contract_last = (((1,), (1,)), ((), ()))

    def project():
        zxbc_ref[...] = lax.dot_general(h_scr[...], wt_ref[...].astype(BF16), contract_last,
                                        preferred_element_type=F32).astype(BF16)

    @pl.when(pl.program_id(1) == 0)
    def _():
        _norm_mod_to_scratch_unrolled(x_ref, g_ref, sh_ref, sc_ref, h_scr)
        project()
        dt = lax.dot_general(h_scr[...], wdt_ref[...].astype(BF16), contract_last, preferred_element_type=F32)
        dt_ref[...] = jnp.concatenate([dt, jnp.zeros((dt.shape[0], LANES - dt.shape[1]), F32)], axis=1)

    @pl.when(pl.program_id(1) > 0)
    def _():
        project()


def _ssm_in_proj(x, g, modr, mod_base, w_in_t, n_main, seq):
    t, d = x.shape
    tm, tn = ROW_TILE, 1024
    tpb = seq // tm
    batch = t // seq
    n_dt = w_in_t.shape[0] - n_main
    return pl.pallas_call(
        _ssm_in_kernel,
        grid=(t // tm, n_main // tn),
        in_specs=[
            _x_tile_spec(tm, d, t // tm),
            pl.BlockSpec((1, d), lambda i, j: (0, 0)),
            _mod_spec(d, mod_base + 0 * batch, tpb),
            _mod_spec(d, mod_base + 1 * batch, tpb),
            pl.BlockSpec((tn, d), lambda i, j: (j, 0)),
            pl.BlockSpec((n_dt, d), lambda i, j: (n_main // n_dt, 0)),
        ],
        out_specs=[
            pl.BlockSpec((tm, tn), lambda i, j: (i, j)),
            pl.BlockSpec((tm, LANES), lambda i, j: (i, 0)),
        ],
        out_shape=[
            jax.ShapeDtypeStruct((t, n_main), BF16),
            jax.ShapeDtypeStruct((t, LANES), F32),
        ],
        scratch_shapes=[pltpu.VMEM((tm, d), BF16)],
        compiler_params=_cparams(2),
        name="ssm_in_proj",
    )(x, g, modr, modr, w_in_t, w_in_t)


def _split3_bf16(v):
    hi = v.astype(BF16)
    r1 = v - hi.astype(F32)
    mid = r1.astype(BF16)
    lo = (r1 - mid.astype(F32)).astype(BF16)
    return hi, mid, lo


def _ssd_kernel(z_ref, xr_ref, bcr_ref, dt_ref, cw_ref, cb_ref, dtb_ref, alog_ref, dskip_ref, ng_ref, e_ref, shift_ref,
                y_ref, state, tail, xs_f, xs_b, bc, ex, y_scr):
    q = SSM_CHUNK
    d_inner = xr_ref.shape[1]
    gw = d_inner // SSM_GROUPS
    gs = SSM_STATE
    bc_split = SSM_GROUPS * gs

    @pl.when(pl.program_id(1) == 0)
    def _():
        state[...] = jnp.zeros_like(state)
        tail[...] = jnp.zeros_like(tail)

    for sub in range(xs_f.shape[0]):
        _ssd_chunk(sub, z_ref, xr_ref, bcr_ref, dt_ref, cw_ref, cb_ref, dtb_ref, alog_ref, dskip_ref, ng_ref, e_ref,
                   shift_ref, y_ref, state, tail, xs_f.at[sub], xs_b.at[sub], bc.at[sub], ex.at[sub], y_scr.at[sub])
    last = xs_f.shape[0] * q
    halo = tail.shape[0]
    tail[:, 0:d_inner] = xr_ref[last - halo:last, :]
    tail[:, d_inner:] = bcr_ref[last - halo:last, :]


def _ssd_chunk(sub, z_ref, xr_ref, bcr_ref, dt_ref, cw_ref, cb_ref, dtb_ref, alog_ref, dskip_ref, ng_ref, e_ref,
               shift_ref, y_ref, state, tail, xs_f, xs_b, bc, ex, y_scr):
    q = SSM_CHUNK
    d_inner = xr_ref.shape[1]
    gw = d_inner // SSM_GROUPS
    gs = SSM_STATE
    bc_split = SSM_GROUPS * gs
    rows = slice(sub * q, (sub + 1) * q)

    halo = tail.shape[0]
    prev_rows = slice(sub * q - halo, sub * q)
    cw_blk = 256
    for c0 in range(0, d_inner + 2 * bc_split, cw_blk):
        cols = slice(c0, c0 + cw_blk)
        src, scols = (xr_ref, cols) if c0 < d_inner else (bcr_ref, slice(c0 - d_inner, c0 - d_inner + cw_blk))
        cur = src[rows, scols]
        before = tail[:, cols] if sub == 0 else src[prev_rows, scols]
        ext = jnp.concatenate([before, cur], axis=0)
        sh = jnp.dot(shift_ref[...], ext, preferred_element_type=F32)
        acc = cb_ref[:, cols] + cw_ref[SSM_CONV - 1:SSM_CONV, cols] * cur.astype(F32)
        for k in range(SSM_CONV - 1):
            acc = acc + cw_ref[k:k + 1, cols] * sh[k * q:(k + 1) * q]
        act = acc * jax.nn.sigmoid(acc)
        if c0 < d_inner:
            xs_f[:, cols] = act
            xs_b[:, cols] = act.astype(BF16)
        else:
            bc[:, c0 - d_inner:c0 - d_inner + cw_blk] = act.astype(BF16)

    v = dt_ref[rows, :] + dtb_ref[...]
    dt = jnp.maximum(v, 0.0) + jnp.log(1.0 + jnp.exp(-jnp.abs(v)))
    a_neg = -jnp.exp(alog_ref[...])
    d_a = dt * a_neg
    row = lax.broadcasted_iota(jnp.int32, (q, q), 0)
    col = lax.broadcasted_iota(jnp.int32, (q, q), 1)
    causal = row >= col
    tril = jnp.where(causal, 1.0, 0.0).astype(BF16)
    a_cum = None
    for part in _split3_bf16(d_a):
        term = jnp.dot(tril, part, preferred_element_type=F32)
        a_cum = term if a_cum is None else a_cum + term
    a2 = a_cum * LOG2E
    b2_t = (a2 - jnp.log2(dt)).T
    a_end = a_cum[q - 1:q, :]
    ea = jnp.exp(a_cum)
    wgt = dt * jnp.exp(a_end - a_cum)
    ea_hi = ea.astype(BF16)
    ea_lo = (ea - ea_hi.astype(F32)).astype(BF16)
    top = jnp.concatenate([ea_hi, ea_lo], axis=1)
    bot = jnp.concatenate([wgt.astype(BF16), jnp.zeros((q, LANES), BF16)], axis=1)
    ex[...] = jnp.dot(jnp.concatenate([top, bot], axis=0), e_ref[...], preferred_element_type=F32)

    lane = lax.broadcasted_iota(jnp.int32, (q, LANES), 1)
    lo_half = lane < SSM_HEAD_DIM
    contract_last = (((1,), (1,)), ((), ()))
    contract_first = (((0,), (0,)), ((), ()))
    heads_per_group = gw // SSM_HEAD_DIM

    def group_body(g):
        gcols = slice(g * gw, (g + 1) * gw)
        b_g = bc[:, g * gs:(g + 1) * gs]
        c_g = bc[:, bc_split + g * gs:bc_split + (g + 1) * gs]
        cb = lax.dot_general(c_g, b_g, contract_last, preferred_element_type=F32)
        s_in = state[g]
        y_off = jnp.dot(c_g, s_in.astype(BF16), preferred_element_type=F32)
        for pr in range(heads_per_group // 2):
            pcols = slice(g * gw + pr * LANES, g * gw + (pr + 1) * LANES)
            x_pair = xs_b[:, pcols]
            h0 = g * heads_per_group + 2 * pr
            halves = []
            for h in (h0, h0 + 1):
                diff = a2[:, h:h + 1] - b2_t[h:h + 1, :]
                mat = (cb * jnp.exp2(jnp.where(causal, diff, NEG_BIG))).astype(BF16)
                halves.append(jnp.dot(mat, x_pair, preferred_element_type=F32))
            y_diag = jnp.where(lo_half, halves[0], halves[1])
            y_scr[:, pcols] = (y_diag + y_off[:, pr * LANES:(pr + 1) * LANES] * ex[0:q, pcols]
                               + dskip_ref[:, pcols] * xs_f[:, pcols])
        xw = (xs_f[:, gcols] * ex[q:2 * q, gcols]).astype(BF16)
        upd = lax.dot_general(b_g, xw, contract_first, preferred_element_type=F32)
        state[g] = s_in * ex[q - 1:q, gcols] + upd

        zg = z_ref[rows, gcols].astype(F32)
        yz = y_scr[:, gcols] * (zg * jax.nn.sigmoid(zg))
        ms = jnp.mean(yz * yz, axis=-1, keepdims=True)
        y_ref[rows, gcols] = (yz * lax.rsqrt(ms + NORM_EPS) * ng_ref[:, gcols]).astype(y_ref.dtype)

    for g in range(SSM_GROUPS):
        group_body(g)


def _ssd(zxbc, dt_raw, conv_w, conv_b, dt_bias_pad, a_log_pad, d_exp, norm_g, expand, batch, seq):
    t = zxbc.shape[0]
    d_inner = d_exp.shape[1]
    n_bc = zxbc.shape[1] - 2 * d_inner
    n_conv = d_inner + n_bc
    n_sub = SSD_CHUNKS_PER_STEP
    q = SSM_CHUNK
    rows = n_sub * q
    nc = seq // rows
    gw = d_inner // SSM_GROUPS
    row_map = lambda b, c: (b * nc + c, 0)
    const = lambda b, c: (0, 0)
    halo = CONV_TAIL_ROWS
    src = jnp.arange(q)[None, :, None] + (halo - (SSM_CONV - 1)) + jnp.arange(SSM_CONV - 1)[:, None, None]
    shift = (jnp.arange(halo + q)[None, None, :] == src).astype(BF16).reshape((SSM_CONV - 1) * q, halo + q)
    return pl.pallas_call(
        _ssd_kernel,
        grid=(batch, nc),
        in_specs=[
            pl.BlockSpec((rows, d_inner), row_map),
            pl.BlockSpec((rows, d_inner), lambda b, c: (b * nc + c, 1)),
            pl.BlockSpec((rows, n_bc), lambda b, c: (b * nc + c, 2 * d_inner // n_bc)),
            pl.BlockSpec((rows, LANES), row_map),
            pl.BlockSpec((SSM_CONV, n_conv), const),
            pl.BlockSpec((1, n_conv), const),
            pl.BlockSpec((1, LANES), const),
            pl.BlockSpec((1, LANES), const),
            pl.BlockSpec((1, d_inner), const),
            pl.BlockSpec((1, d_inner), const),
            pl.BlockSpec((2 * LANES, d_inner), const),
            pl.BlockSpec(shift.shape, const),
        ],
        out_specs=pl.BlockSpec((rows, d_inner), row_map),
        out_shape=jax.ShapeDtypeStruct((t, d_inner), BF16),
        scratch_shapes=[
            pltpu.VMEM((SSM_GROUPS, SSM_STATE, gw), F32),
            pltpu.VMEM((halo, n_conv), BF16),
            pltpu.VMEM((n_sub, q, d_inner), F32),
            pltpu.VMEM((n_sub, q, d_inner), BF16),
            pltpu.VMEM((n_sub, q, n_bc), BF16),
            pltpu.VMEM((n_sub, 2 * q, d_inner), F32),
            pltpu.VMEM((n_sub, q, d_inner), F32),
        ],
        compiler_params=_cparams(2),
        name="ssd_scan",
    )(zxbc, zxbc, zxbc, dt_raw, conv_w, conv_b, dt_bias_pad, a_log_pad, d_exp, norm_g, expand, shift)


def _mm_res_kernel(a_ref, w_ref, x_ref, gate_ref, o_ref):
    y = jnp.dot(a_ref[...], w_ref[...].astype(BF16), preferred_element_type=F32)
    o_ref[...] = x_ref[...] + gate_ref[...] * y


def _matmul_residual(a, w, x, modr, gate_base, seq):
    t, k = a.shape
    d = x.shape[1]
    tm, tn = ROW_TILE, 512
    tpb = seq // tm
    gate_spec = pl.BlockSpec((None, 1, tn), lambda i, j: (gate_base + i // tpb, 0, j))
    return pl.pallas_call(
        _mm_res_kernel,
        grid=(t // tm, d // tn),
        in_specs=[
            pl.BlockSpec((tm, k), lambda i, j: (i, 0)),
            pl.BlockSpec((k, tn), lambda i, j: (0, j)),
            pl.BlockSpec((tm, tn), lambda i, j: (i, j)),
            gate_spec,
        ],
        out_specs=pl.BlockSpec((tm, tn), lambda i, j: (i, j)),
        out_shape=jax.ShapeDtypeStruct((t, d), F32),
        compiler_params=_cparams(2),
        name="ssm_out_proj",
    )(a, w, x, modr)


def kernel(x, c, ada_w, ada_b, norm_mix, norm_ffn, ffn_w1, ffn_w2, even_w_in, pool_w, pool_scale, even_w_out,
           ssm_w_in, ssm_conv_w, ssm_conv_b, ssm_dt_bias, ssm_a_log, ssm_d, ssm_norm, ssm_w_out, final_norm):
    batch, seq, d = x.shape
    depth = ada_w.shape[0]
    t = batch * seq
    xf = x.reshape(t, d)

    c_pad = jnp.pad(c, ((0, 8 - batch), (0, 0)))
    mod = _ada_mod(c_pad, ada_w, ada_b)[:, :batch]
    modr = mod.reshape(depth, batch, 6, d).transpose(0, 2, 1, 3).reshape(depth * 6 * batch, 1, d)

    d_inner = ssm_w_out.shape[1]
    n_heads = ssm_dt_bias.shape[1]
    head_of_channel = jnp.arange(d_inner) // SSM_HEAD_DIM
    expand1 = (jnp.arange(LANES)[:, None] == head_of_channel[None, :]).astype(BF16)
    expand = jnp.concatenate([expand1, expand1], axis=0)

    for i in range(depth):
        base = i * 6 * batch
        j = i // 2
        g_mix = norm_mix[i].reshape(1, d)
        g_ffn = norm_ffn[i].reshape(1, d)
        if i % 2 == 0:
            qkv, u = _even_in_proj(xf, g_mix, modr, base, even_w_in[j], seq)
            merged = None
            for gi in range(N_GROUPS):
                merged = _attn_group(qkv, merged, gi, batch, seq)
            xf = _even_out_proj(merged, u, pool_w[j], pool_scale[j].reshape(1, POOL_W), even_w_out[j], xf, modr,
                                base + 2 * batch, seq)
        else:
            w_in_t = jnp.swapaxes(ssm_w_in[j], 0, 1)
            n_main = w_in_t.shape[0] - n_heads
            pad_h = ((0, 0), (0, LANES - n_heads))
            zxbc, dt_raw = _ssm_in_proj(xf, g_mix, modr, base, w_in_t, n_main, seq)
            y = _ssd(zxbc, dt_raw, ssm_conv_w[j], ssm_conv_b[j].reshape(1, -1),
                     jnp.pad(ssm_dt_bias[j].reshape(1, -1), pad_h),
                     jnp.pad(ssm_a_log[j].reshape(1, -1), pad_h),
                     jnp.repeat(ssm_d[j], SSM_HEAD_DIM).reshape(1, d_inner), ssm_norm[j].reshape(1, d_inner),
                     expand, batch, seq)
            xf = _matmul_residual(y, ssm_w_out[j], xf, modr, base + 2 * batch, seq)
        xf = _ffn(xf, g_ffn, modr, base, ffn_w1, ffn_w2, i, final_norm.reshape(1, d), seq,
                  final_norm=(i == depth - 1))
    return xf.reshape(batch, seq, d)
```

```python
import functools

import jax
import jax.numpy as jnp
from jax import lax
from jax.experimental import pallas as pl
from jax.experimental.pallas import tpu as pltpu

F32 = jnp.float32
BF16 = jnp.bfloat16

NORM_EPS = 1e-6
NEG_BIG = -1e30
LOG2E = 1.4426950408889634

ATTN_GROUPS = ((128, 1), (512, 4), (2048, 16))
ATTN_HEADS = 8
ATTN_HEAD_DIM = 64
ATTN_BLOCK = 128
ATTN_UNITS_PER_STEP = 8
ATTN_GW = ATTN_HEADS * ATTN_HEAD_DIM
N_GROUPS = len(ATTN_GROUPS)
QKV_W = 3 * N_GROUPS * ATTN_GW
POOL_WINDOWS = (2, 4, 8, 16)
POOL_GW = 128
POOL_W = len(POOL_WINDOWS) * POOL_GW
POOL_HALO = 16

SSM_HEAD_DIM = 64
SSM_GROUPS = 8
SSM_STATE = 128
SSM_CONV = 4
SSM_CHUNK = 128
SSD_CHUNKS_PER_STEP = 2
CONV_TAIL_ROWS = 16

VMEM_LIMIT = 56 * 1024 * 1024
BIG_TILE_VMEM_LIMIT = 60 * 1024 * 1024
LANES = 128
ROW_TILE = 1024


def _cparams(n_axes, vmem_limit=VMEM_LIMIT):
    return pltpu.CompilerParams(dimension_semantics=("arbitrary",) * n_axes, vmem_limit_bytes=vmem_limit)


def _ada_kernel(c_ref, w_ref, b_ref, o_ref):
    c = c_ref[...]
    cond = (c * jax.nn.sigmoid(c)).astype(BF16)
    w = w_ref[...].astype(BF16)
    o_ref[...] = jnp.dot(cond, w, preferred_element_type=F32) + b_ref[...]


def _ada_mod(c_pad, ada_w, ada_b):
    depth, d, n = ada_w.shape
    rows = c_pad.shape[0]
    tn = 1024
    return pl.pallas_call(
        _ada_kernel,
        grid=(depth, n // tn),
        in_specs=[
            pl.BlockSpec((rows, d), lambda l, j: (0, 0)),
            pl.BlockSpec((None, d, tn), lambda l, j: (l, 0, j)),
            pl.BlockSpec((None, 1, tn), lambda l, j: (l, 0, j)),
        ],
        out_specs=pl.BlockSpec((None, rows, tn), lambda l, j: (l, 0, j)),
        out_shape=jax.ShapeDtypeStruct((depth, rows, n), F32),
        compiler_params=_cparams(2),
        name="ada_mod",
    )(c_pad, ada_w, ada_b.reshape(depth, 1, n))


def _norm_mod_rows(x_ref, g_ref, sh_ref, sc_ref, store, rows_per_step=256):
    tm = x_ref.shape[0]
    gain = g_ref[...] * (1.0 + sc_ref[...])
    shift = sh_ref[...]

    def body(i, carry):
        r0 = pl.multiple_of(i * rows_per_step, rows_per_step)
        x = x_ref[pl.ds(r0, rows_per_step), :]
        ms = jnp.mean(x * x, axis=-1, keepdims=True)
        store(r0, x * lax.rsqrt(ms + NORM_EPS) * gain + shift, rows_per_step)
        return carry

    lax.fori_loop(0, tm // rows_per_step, body, 0)


def _norm_mod_to_scratch_unrolled(x_ref, g_ref, sh_ref, sc_ref, h_scr, rows_per_step=64):
    gain = g_ref[...] * (1.0 + sc_ref[...])
    shift = sh_ref[...]
    for r0 in range(0, x_ref.shape[0], rows_per_step):
        x = x_ref[r0:r0 + rows_per_step, :]
        ms = jnp.mean(x * x, axis=-1, keepdims=True)
        h_scr[r0:r0 + rows_per_step, :] = (x * lax.rsqrt(ms + NORM_EPS) * gain + shift).astype(BF16)


def _x_tile_spec(tm, d, n_row_tiles, advance_at=1):
    return pl.BlockSpec((tm, d), lambda i, j: (jnp.minimum(i + jnp.minimum(j // advance_at, 1), n_row_tiles - 1), 0))


def _mod_spec(d, idx_base, tiles_per_batch):
    return pl.BlockSpec((None, 1, d), lambda i, *_: (idx_base + i // tiles_per_batch, 0, 0))


def _even_in_kernel(x_ref, g_ref, sh_ref, sc_ref, w_ref, wu_ref, qkv_ref, u_ref, hf, h3):
    j = pl.program_id(1)
    tm = x_ref.shape[0]
    n_slabs = hf.shape[0]

    @pl.when(j == 0)
    def _():
        def store(r0, h, n):
            h3[0, pl.ds(r0, n), :] = h.astype(BF16)
            for c in range(n_slabs):
                hf[c, pl.ds(r0, n), :] = h[:, c * LANES:(c + 1) * LANES]

        _norm_mod_rows(x_ref, g_ref, sh_ref, sc_ref, store)
        u_ref[...] = jnp.dot(h3[0], wu_ref[...].astype(BF16), preferred_element_type=F32)
        qkv_ref[...] = jnp.dot(h3[0], w_ref[...].astype(BF16), preferred_element_type=F32).astype(BF16)
        for gi, (_, dil) in enumerate(ATTN_GROUPS):
            if dil == 1:
                continue
            rows = tm // dil
            for c in range(n_slabs):
                for r in range(dil):
                    piece = hf[c, pl.ds(r, rows, stride=dil), :]
                    h3[gi, r * rows:(r + 1) * rows, c * LANES:(c + 1) * LANES] = piece.astype(BF16)

    @pl.when(j > 0)
    def _():
        gi = j // N_GROUPS
        qkv_ref[...] = jnp.dot(h3[gi], w_ref[...].astype(BF16), preferred_element_type=F32).astype(BF16)


def _even_in_proj(x, g, modr, mod_base, w_in, seq):
    t, d = x.shape
    tm, tn = ROW_TILE, ATTN_GW
    tpb = seq // tm
    batch = t // seq
    n_u = w_in.shape[1] - QKV_W
    return pl.pallas_call(
        _even_in_kernel,
        grid=(t // tm, QKV_W // tn),
        in_specs=[
            _x_tile_spec(tm, d, t // tm),
            pl.BlockSpec((1, d), lambda i, j: (0, 0)),
            _mod_spec(d, mod_base + 0 * batch, tpb),
            _mod_spec(d, mod_base + 1 * batch, tpb),
            pl.BlockSpec((d, tn), lambda i, j: (0, (j % N_GROUPS) * N_GROUPS + j // N_GROUPS)),
            pl.BlockSpec((d, n_u), lambda i, j: (0, QKV_W // n_u), pipeline_mode=pl.Buffered(1)),
        ],
        out_specs=[
            pl.BlockSpec((tm, tn), lambda i, j: (i, j)),
            pl.BlockSpec((tm, n_u), lambda i, j: (i, 0)),
        ],
        out_shape=[
            jax.ShapeDtypeStruct((t, QKV_W), BF16),
            jax.ShapeDtypeStruct((t, n_u), F32),
        ],
        scratch_shapes=[pltpu.VMEM((d // LANES, tm, LANES), F32), pltpu.VMEM((N_GROUPS, tm, d), BF16)],
        compiler_params=_cparams(2, BIG_TILE_VMEM_LIMIT),
        name="even_in_proj",
    )(x, g, modr, modr, w_in, w_in)


def _unit_rows(ref, u, sl, dil):
    if dil == 1:
        return ref[u * ATTN_BLOCK:(u + 1) * ATTN_BLOCK, sl]
    if len(ref.shape) == 3:
        return ref[u, :, sl]
    return jnp.concatenate([ref[p, u, :, sl] for p in range(ref.shape[0])], axis=0)


def _attn_kernel(*refs, first, last, dil, units):
    if first:
        q_ref, k_ref, v_ref, o_ref, l_ref, kk, vv = refs
        op_ref = lp_ref = None
    elif last:
        q_ref, k_ref, v_ref, op_ref, lp_ref, o_ref, kk, vv = refs
        l_ref = None
    else:
        q_ref, k_ref, v_ref, op_ref, lp_ref, o_ref, l_ref, kk, vv = refs
    step_n = pl.program_id(1)
    step_r = pl.program_id(2)
    blk = ATTN_BLOCK
    full = slice(None)

    @pl.when(step_n == 0)
    def _():
        n_zero = units if dil > 1 else 1
        kk[pl.ds(step_r * n_zero, n_zero), 0:blk, :] = jnp.zeros((n_zero, blk, ATTN_GW), BF16)
        vv[pl.ds(step_r * n_zero, n_zero), 0:blk, :] = jnp.zeros((n_zero, blk, ATTN_GW), BF16)

    qi = lax.broadcasted_iota(jnp.int32, (2 * blk, 2 * blk), 0) & (blk - 1)
    kj = lax.broadcasted_iota(jnp.int32, (2 * blk, 2 * blk), 1)
    cur_bias = jnp.where(kj - blk <= qi, 0.0, NEG_BIG)
    prev_ok = kj >= qi
    lane = lax.broadcasted_iota(jnp.int32, (blk, LANES), 1)
    lo_half = lane < ATTN_HEAD_DIM
    contract_last = (((1,), (1,)), ((), ()))

    for u in range(units):
        if dil > 1:
            r = step_r * units + u
            has_prev = step_n > 0
            rows_out = pl.ds(r, blk, stride=dil)
        else:
            r = 0
            has_prev = (step_n > 0) if u == 0 else True
            rows_out = slice(u * blk, (u + 1) * blk)
        prev_pen = 0.0 if has_prev is True else jnp.where(has_prev, 0.0, NEG_BIG)
        bias = jnp.where(kj < blk, jnp.where(prev_ok, prev_pen, NEG_BIG), cur_bias)
        k_cur = _unit_rows(k_ref, u, full, dil)
        v_cur = _unit_rows(v_ref, u, full, dil)
        kk[r, blk:2 * blk, :] = k_cur
        vv[r, blk:2 * blk, :] = v_cur
        for hp in range(ATTN_GW // LANES):
            sl = slice(hp * LANES, (hp + 1) * LANES)
            qp = _unit_rows(q_ref, u, sl, dil) * jnp.asarray(ATTN_HEAD_DIM ** -0.5, BF16)
            zero = jnp.zeros_like(qp)
            qm = jnp.concatenate([jnp.where(lo_half, qp, zero), jnp.where(lo_half, zero, qp)], axis=0)
            s = lax.dot_general(qm, kk[r, :, sl], contract_last, preferred_element_type=F32) + bias
            m = jnp.max(s, axis=-1, keepdims=True)
            p = jnp.exp(s - m)
            den = jnp.sum(p, axis=-1, keepdims=True)
            acc = jnp.dot(p.astype(BF16), vv[r, :, sl], preferred_element_type=F32)
            o2 = acc / den
            lse2 = jnp.broadcast_to(m + jnp.log(den), (2 * blk, LANES))
            o_new = jnp.where(lo_half, o2[0:blk], o2[blk:2 * blk])
            lse_new = jnp.where(lo_half, lse2[0:blk], lse2[blk:2 * blk])
            if not first:
                lse_old = lp_ref[hp, rows_out, :]
                o_old = op_ref[hp, rows_out, :]
                mx = jnp.maximum(lse_old, lse_new)
                w_old = jnp.exp(lse_old - mx)
                w_new = jnp.exp(lse_new - mx)
                tot = w_old + w_new
                o_new = (o_old * w_old + o_new * w_new) / tot
                lse_new = mx + jnp.log(tot)
            o_ref[hp, rows_out, :] = o_new
            if not last:
                l_ref[hp, rows_out, :] = lse_new
        kk[r, 0:blk, :] = k_cur
        vv[r, 0:blk, :] = v_cur


def _attn_group(qkv, prev, gi, batch, seq):
    dil = ATTN_GROUPS[gi][1]
    first, last = gi == 0, gi == N_GROUPS - 1
    t = batch * seq
    nb = seq // dil // ATTN_BLOCK
    units = ATTN_UNITS_PER_STEP if dil != 4 else 4
    n_pairs = ATTN_GW // LANES
    if dil == 1:
        qkv_v = qkv.reshape(batch, seq, QKV_W)
        grid = (batch, nb // units, 1)
        blk_shape = (None, units * ATTN_BLOCK, ATTN_GW)
        idx = lambda b, n, r, col: (b, n, col)
        o_spec = pl.BlockSpec((n_pairs, units * ATTN_BLOCK, LANES), lambda b, n, r: (0, b * (nb // units) + n, 0))
    else:
        chunk = ROW_TILE // dil
        qkv_v = qkv.reshape(batch, seq // ROW_TILE, dil, chunk, QKV_W)
        grid = (batch, nb, dil // units)
        if chunk >= ATTN_BLOCK:
            sub = chunk // ATTN_BLOCK
            blk_shape = (None, None, units, ATTN_BLOCK, ATTN_GW)
            idx = lambda b, n, r, col: (b, n // sub, r, n % sub, col)
        else:
            blk_shape = (None, ATTN_BLOCK // chunk, units, chunk, ATTN_GW)
            idx = lambda b, n, r, col: (b, n, r, 0, col)
        o_spec = pl.BlockSpec((n_pairs, ATTN_BLOCK * dil, LANES), lambda b, n, r: (0, b * nb + n, 0))
    in_specs = [pl.BlockSpec(blk_shape, functools.partial(lambda b, n, r, col: idx(b, n, r, col), col=gi * 3 + which))
                for which in range(3)]
    args = [qkv_v, qkv_v, qkv_v]
    if not first:
        in_specs += [o_spec, o_spec]
        args += list(prev)
    o_shape = jax.ShapeDtypeStruct((n_pairs, t, LANES), F32)
    outs = pl.pallas_call(
        functools.partial(_attn_kernel, first=first, last=last, dil=dil, units=units),
        grid=grid,
        in_specs=in_specs,
        out_specs=[o_spec] if last else [o_spec, o_spec],
        out_shape=[o_shape] if last else [o_shape, o_shape],
        scratch_shapes=[pltpu.VMEM((dil, 2 * ATTN_BLOCK, ATTN_GW), BF16),
                        pltpu.VMEM((dil, 2 * ATTN_BLOCK, ATTN_GW), BF16)],
        compiler_params=_cparams(3),
        name=f"dilated_attn_g{gi}",
    )(*args)
    return outs[0] if last else tuple(outs)


def _even_out_kernel(attn_ref, u_ref, halo_ref, pw_ref, ps_ref, w_ref, x_ref, gate_ref, o_ref, ue, a_scr, w_bf,
                     *, seq):
    tm = u_ref.shape[0]
    i = pl.program_id(0)

    @pl.when(i == 0)
    def _():
        w_bf[...] = w_ref[...].astype(BF16)

    row0 = (i * tm) % seq
    halo_ok = jnp.where(row0 > 0, 1.0, 0.0)
    ue[0:POOL_HALO, :] = halo_ref[...] * halo_ok
    ue[POOL_HALO:POOL_HALO + tm, :] = u_ref[...]
    for hp in range(ATTN_GW // LANES):
        a_scr[:, hp * LANES:(hp + 1) * LANES] = attn_ref[hp].astype(BF16)
    pos = row0 + lax.broadcasted_iota(jnp.int32, (tm, POOL_GW), 0)
    for gi, win in enumerate(POOL_WINDOWS):
        cols = slice(gi * POOL_GW, (gi + 1) * POOL_GW)
        tok = ue[POOL_HALO:POOL_HALO + tm, cols]
        acc = tok
        for back in range(1, win):
            acc = acc + ue[POOL_HALO - back:POOL_HALO - back + tm, cols]
        cnt = jnp.minimum(pos + 1, win).astype(F32)
        diff = acc / cnt - tok
        y = jnp.dot(diff.astype(BF16), pw_ref[gi].astype(BF16), preferred_element_type=F32) * ps_ref[:, cols]
        a_scr[:, ATTN_GW + gi * POOL_GW:ATTN_GW + (gi + 1) * POOL_GW] = y.astype(BF16)
    y = jnp.dot(a_scr[...], w_bf[...], preferred_element_type=F32)
    o_ref[...] = x_ref[...] + gate_ref[...] * y


def _even_out_proj(attn, u, pool_w, pool_scale, w_out, x, modr, gate_base, seq):
    t, d = x.shape
    tm = 512
    tpb = seq // tm
    k = w_out.shape[0]
    halo_blocks = tm // POOL_HALO
    n_pairs = attn.shape[0]
    return pl.pallas_call(
        functools.partial(_even_out_kernel, seq=seq),
        grid=(t // tm,),
        in_specs=[
            pl.BlockSpec((n_pairs, tm, LANES), lambda i: (0, i, 0)),
            pl.BlockSpec((tm, POOL_W), lambda i: (i, 0)),
            pl.BlockSpec((POOL_HALO, POOL_W), lambda i: (jnp.maximum(i * halo_blocks - 1, 0), 0)),
            pl.BlockSpec(pool_w.shape, lambda i: (0, 0, 0)),
            pl.BlockSpec((1, POOL_W), lambda i: (0, 0)),
            pl.BlockSpec((k, d), lambda i: (0, 0), pipeline_mode=pl.Buffered(1)),
            pl.BlockSpec((tm, d), lambda i: (i, 0)),
            _mod_spec(d, gate_base, tpb),
        ],
        out_specs=pl.BlockSpec((tm, d), lambda i: (i, 0)),
        out_shape=jax.ShapeDtypeStruct((t, d), F32),
        scratch_shapes=[pltpu.VMEM((tm + POOL_HALO, POOL_W), F32), pltpu.VMEM((tm, k), BF16),
                        pltpu.VMEM((k, d), BF16)],
        compiler_params=_cparams(1),
        name="even_out_proj",
    )(attn, u, u, pool_w, pool_scale, w_out, x, modr)


def _ffn_kernel(x_ref, g_ref, sh_ref, sc_ref, gate_ref, w1_ref, w2_ref, fin_ref, o_ref, h_scr, *, final_norm):
    j = pl.program_id(1)

    def hidden_tile():
        a = jnp.dot(h_scr[...], w1_ref[...].astype(BF16), preferred_element_type=F32)
        a = jnp.maximum(a, 0.0)
        return jnp.dot((a * a).astype(BF16), w2_ref[...].astype(BF16), preferred_element_type=F32)

    @pl.when(j == 0)
    def _():
        _norm_mod_to_scratch_unrolled(x_ref, g_ref, sh_ref, sc_ref, h_scr)
        o_ref[...] = x_ref[...] + gate_ref[...] * hidden_tile()

    @pl.when(j > 0)
    def _():
        o_ref[...] += gate_ref[...] * hidden_tile()

    if final_norm:
        @pl.when(j == pl.num_programs(1) - 1)
        def _():
            rows = 256
            fin = fin_ref[...]

            def body(i, carry):
                r0 = pl.multiple_of(i * rows, rows)
                y = o_ref[pl.ds(r0, rows), :]
                ms = jnp.mean(y * y, axis=-1, keepdims=True)
                o_ref[pl.ds(r0, rows), :] = y * lax.rsqrt(ms + NORM_EPS) * fin
                return carry

            lax.fori_loop(0, o_ref.shape[0] // rows, body, 0)


def _ffn(x, g, modr, mod_base, w1_all, w2_all, layer, fin, seq, final_norm):
    t, d = x.shape
    hdim = w1_all.shape[2]
    tm, th = ROW_TILE, 512
    tpb = seq // tm
    batch = t // seq
    return pl.pallas_call(
        functools.partial(_ffn_kernel, final_norm=final_norm),
        grid=(t // tm, hdim // th),
        in_specs=[
            _x_tile_spec(tm, d, t // tm, advance_at=2),
            pl.BlockSpec((1, d), lambda i, j: (0, 0)),
            _mod_spec(d, mod_base + 3 * batch, tpb),
            _mod_spec(d, mod_base + 4 * batch, tpb),
            _mod_spec(d, mod_base + 5 * batch, tpb),
            pl.BlockSpec((None, d, th), lambda i, j: (layer, 0, j)),
            pl.BlockSpec((None, th, d), lambda i, j: (layer, j, 0)),
            pl.BlockSpec((1, d), lambda i, j: (0, 0)),
        ],
        out_specs=pl.BlockSpec((tm, d), lambda i, j: (i, 0)),
        out_shape=jax.ShapeDtypeStruct((t, d), F32),
        scratch_shapes=[pltpu.VMEM((tm, d), BF16)],
        compiler_params=_cparams(2, BIG_TILE_VMEM_LIMIT),
        name="ffn_final" if final_norm else "ffn",
    )(x, g, modr, modr, modr, w1_all, w2_all, fin)


def _ssm_in_kernel(x_ref, g_ref, sh_ref, sc_ref, wt_ref, wdt_ref, zxbc_ref, dt_ref, h_scr):
    contract_last = (((1,), (1,)), ((), ()))

    def project():
        zxbc_ref[...] = lax.dot_general(h_scr[...], wt_ref[...].astype(BF16), contract_last,
                                        preferred_element_type=F32).astype(BF16)

    @pl.when(pl.program_id(1) == 0)
    def _():
        _norm_mod_to_scratch_unrolled(x_ref, g_ref, sh_ref, sc_ref, h_scr)
        project()
        dt = lax.dot_general(h_scr[...], wdt_ref[...].astype(BF16), contract_last, preferred_element_type=F32)
        dt_ref[...] = jnp.concatenate([dt, jnp.zeros((dt.shape[0], LANES - dt.shape[1]), F32)], axis=1)

    @pl.when(pl.program_id(1) > 0)
    def _():
        project()


def _ssm_in_proj(x, g, modr, mod_base, w_in_t, n_main, seq):
    t, d = x.shape
    tm, tn = ROW_TILE, 1024
    tpb = seq // tm
    batch = t // seq
    n_dt = w_in_t.shape[0] - n_main
    return pl.pallas_call(
        _ssm_in_kernel,
        grid=(t // tm, n_main // tn),
        in_specs=[
            _x_tile_spec(tm, d, t // tm),
            pl.BlockSpec((1, d), lambda i, j: (0, 0)),
            _mod_spec(d, mod_base + 0 * batch, tpb),
            _mod_spec(d, mod_base + 1 * batch, tpb),
            pl.BlockSpec((tn, d), lambda i, j: (j, 0)),
            pl.BlockSpec((n_dt, d), lambda i, j: (n_main // n_dt, 0)),
        ],
        out_specs=[
            pl.BlockSpec((tm, tn), lambda i, j: (i, j)),
            pl.BlockSpec((tm, LANES), lambda i, j: (i, 0)),
        ],
        out_shape=[
            jax.ShapeDtypeStruct((t, n_main), BF16),
            jax.ShapeDtypeStruct((t, LANES), F32),
        ],
        scratch_shapes=[pltpu.VMEM((tm, d), BF16)],
        compiler_params=_cparams(2),
        name="ssm_in_proj",
    )(x, g, modr, modr, w_in_t, w_in_t)


def _split3_bf16(v):
    hi = v.astype(BF16)
    r1 = v - hi.astype(F32)
    mid = r1.astype(BF16)
    lo = (r1 - mid.astype(F32)).astype(BF16)
    return hi, mid, lo


def _ssd_kernel(z_ref, xr_ref, bcr_ref, dt_ref, cw_ref, cb_ref, dtb_ref, alog_ref, dskip_ref, ng_ref, e_ref, shift_ref,
                y_ref, state, tail, xs_f, xs_b, bc, ex, y_scr):
    q = SSM_CHUNK
    d_inner = xr_ref.shape[1]
    gw = d_inner // SSM_GROUPS
    gs = SSM_STATE
    bc_split = SSM_GROUPS * gs

    @pl.when(pl.program_id(1) == 0)
    def _():
        state[...] = jnp.zeros_like(state)
        tail[...] = jnp.zeros_like(tail)

    for sub in range(xs_f.shape[0]):
        _ssd_chunk(sub, z_ref, xr_ref, bcr_ref, dt_ref, cw_ref, cb_ref, dtb_ref, alog_ref, dskip_ref, ng_ref, e_ref,
                   shift_ref, y_ref, state, tail, xs_f.at[sub], xs_b.at[sub], bc.at[sub], ex.at[sub], y_scr.at[sub])
    last = xs_f.shape[0] * q
    halo = tail.shape[0]
    tail[:, 0:d_inner] = xr_ref[last - halo:last, :]
    tail[:, d_inner:] = bcr_ref[last - halo:last, :]


def _ssd_chunk(sub, z_ref, xr_ref, bcr_ref, dt_ref, cw_ref, cb_ref, dtb_ref, alog_ref, dskip_ref, ng_ref, e_ref,
               shift_ref, y_ref, state, tail, xs_f, xs_b, bc, ex, y_scr):
    q = SSM_CHUNK
    d_inner = xr_ref.shape[1]
    gw = d_inner // SSM_GROUPS
    gs = SSM_STATE
    bc_split = SSM_GROUPS * gs
    rows = slice(sub * q, (sub + 1) * q)

    halo = tail.shape[0]
    prev_rows = slice(sub * q - halo, sub * q)
    cw_blk = 256
    for c0 in range(0, d_inner + 2 * bc_split, cw_blk):
        cols = slice(c0, c0 + cw_blk)
        src, scols = (xr_ref, cols) if c0 < d_inner else (bcr_ref, slice(c0 - d_inner, c0 - d_inner + cw_blk))
        cur = src[rows, scols]
        before = tail[:, cols] if sub == 0 else src[prev_rows, scols]
        ext = jnp.concatenate([before, cur], axis=0)
        sh = jnp.dot(shift_ref[...], ext, preferred_element_type=F32)
        acc = cb_ref[:, cols] + cw_ref[SSM_CONV - 1:SSM_CONV, cols] * cur.astype(F32)
        for k in range(SSM_CONV - 1):
            acc = acc + cw_ref[k:k + 1, cols] * sh[k * q:(k + 1) * q]
        act = acc * jax.nn.sigmoid(acc)
        if c0 < d_inner:
            xs_f[:, cols] = act
            xs_b[:, cols] = act.astype(BF16)
        else:
            bc[:, c0 - d_inner:c0 - d_inner + cw_blk] = act.astype(BF16)

    v = dt_ref[rows, :] + dtb_ref[...]
    dt = jnp.maximum(v, 0.0) + jnp.log(1.0 + jnp.exp(-jnp.abs(v)))
    a_neg = -jnp.exp(alog_ref[...])
    d_a = dt * a_neg
    row = lax.broadcasted_iota(jnp.int32, (q, q), 0)
    col = lax.broadcasted_iota(jnp.int32, (q, q), 1)
    causal = row >= col
    tril = jnp.where(causal, 1.0, 0.0).astype(BF16)
    a_cum = None
    for part in _split3_bf16(d_a):
        term = jnp.dot(tril, part, preferred_element_type=F32)
        a_cum = term if a_cum is None else a_cum + term
    a2 = a_cum * LOG2E
    b2_t = (a2 - jnp.log2(dt)).T
    a_end = a_cum[q - 1:q, :]
    ea = jnp.exp(a_cum)
    wgt = dt * jnp.exp(a_end - a_cum)
    ea_hi = ea.astype(BF16)
    ea_lo = (ea - ea_hi.astype(F32)).astype(BF16)
    top = jnp.concatenate([ea_hi, ea_lo], axis=1)
    bot = jnp.concatenate([wgt.astype(BF16), jnp.zeros((q, LANES), BF16)], axis=1)
    ex[...] = jnp.dot(jnp.concatenate([top, bot], axis=0), e_ref[...], preferred_element_type=F32)

    lane = lax.broadcasted_iota(jnp.int32, (q, LANES), 1)
    lo_half = lane < SSM_HEAD_DIM
    contract_last = (((1,), (1,)), ((), ()))
    contract_first = (((0,), (0,)), ((), ()))
    heads_per_group = gw // SSM_HEAD_DIM

    def group_body(g):
        gcols = slice(g * gw, (g + 1) * gw)
        b_g = bc[:, g * gs:(g + 1) * gs]
        c_g = bc[:, bc_split + g * gs:bc_split + (g + 1) * gs]
        cb = lax.dot_general(c_g, b_g, contract_last, preferred_element_type=F32)
        s_in = state[g]
        y_off = jnp.dot(c_g, s_in.astype(BF16), preferred_element_type=F32)
        for pr in range(heads_per_group // 2):
            pcols = slice(g * gw + pr * LANES, g * gw + (pr + 1) * LANES)
            x_pair = xs_b[:, pcols]
            h0 = g * heads_per_group + 2 * pr
            halves = []
            for h in (h0, h0 + 1):
                diff = a2[:, h:h + 1] - b2_t[h:h + 1, :]
                mat = (cb * jnp.exp2(jnp.where(causal, diff, NEG_BIG))).astype(BF16)
                halves.append(jnp.dot(mat, x_pair, preferred_element_type=F32))
            y_diag = jnp.where(lo_half, halves[0], halves[1])
            y_scr[:, pcols] = (y_diag + y_off[:, pr * LANES:(pr + 1) * LANES] * ex[0:q, pcols]
                               + dskip_ref[:, pcols] * xs_f[:, pcols])
        xw = (xs_f[:, gcols] * ex[q:2 * q, gcols]).astype(BF16)
        upd = lax.dot_general(b_g, xw, contract_first, preferred_element_type=F32)
        state[g] = s_in * ex[q - 1:q, gcols] + upd

        zg = z_ref[rows, gcols].astype(F32)
        yz = y_scr[:, gcols] * (zg * jax.nn.sigmoid(zg))
        ms = jnp.mean(yz * yz, axis=-1, keepdims=True)
        y_ref[rows, gcols] = (yz * lax.rsqrt(ms + NORM_EPS) * ng_ref[:, gcols]).astype(y_ref.dtype)

    for g in range(SSM_GROUPS):
        group_body(g)


def _ssd(zxbc, dt_raw, conv_w, conv_b, dt_bias_pad, a_log_pad, d_exp, norm_g, expand, batch, seq):
    t = zxbc.shape[0]
    d_inner = d_exp.shape[1]
    n_bc = zxbc.shape[1] - 2 * d_inner
    n_conv = d_inner + n_bc
    n_sub = SSD_CHUNKS_PER_STEP
    q = SSM_CHUNK
    rows = n_sub * q
    nc = seq // rows
    gw = d_inner // SSM_GROUPS
    row_map = lambda b, c: (b * nc + c, 0)
    const = lambda b, c: (0, 0)
    halo = CONV_TAIL_ROWS
    src = jnp.arange(q)[None, :, None] + (halo - (SSM_CONV - 1)) + jnp.arange(SSM_CONV - 1)[:, None, None]
    shift = (jnp.arange(halo + q)[None, None, :] == src).astype(BF16).reshape((SSM_CONV - 1) * q, halo + q)
    return pl.pallas_call(
        _ssd_kernel,
        grid=(batch, nc),
        in_specs=[
            pl.BlockSpec((rows, d_inner), row_map),
            pl.BlockSpec((rows, d_inner), lambda b, c: (b * nc + c, 1)),
            pl.BlockSpec((rows, n_bc), lambda b, c: (b * nc + c, 2 * d_inner // n_bc)),
            pl.BlockSpec((rows, LANES), row_map),
            pl.BlockSpec((SSM_CONV, n_conv), const),
            pl.BlockSpec((1, n_conv), const),
            pl.BlockSpec((1, LANES), const),
            pl.BlockSpec((1, LANES), const),
            pl.BlockSpec((1, d_inner), const),
            pl.BlockSpec((1, d_inner), const),
            pl.BlockSpec((2 * LANES, d_inner), const),
            pl.BlockSpec(shift.shape, const),
        ],
        out_specs=pl.BlockSpec((rows, d_inner), row_map),
        out_shape=jax.ShapeDtypeStruct((t, d_inner), BF16),
        scratch_shapes=[
            pltpu.VMEM((SSM_GROUPS, SSM_STATE, gw), F32),
            pltpu.VMEM((halo, n_conv), BF16),
            pltpu.VMEM((n_sub, q, d_inner), F32),
            pltpu.VMEM((n_sub, q, d_inner), BF16),
            pltpu.VMEM((n_sub, q, n_bc), BF16),
            pltpu.VMEM((n_sub, 2 * q, d_inner), F32),
            pltpu.VMEM((n_sub, q, d_inner), F32),
        ],
        compiler_params=_cparams(2),
        name="ssd_scan",
    )(zxbc, zxbc, zxbc, dt_raw, conv_w, conv_b, dt_bias_pad, a_log_pad, d_exp, norm_g, expand, shift)


def _mm_res_kernel(a_ref, w_ref, x_ref, gate_ref, o_ref, w_bf):
    @pl.when(pl.program_id(1) == 0)
    def _():
        w_bf[...] = w_ref[...].astype(BF16)

    y = jnp.dot(a_ref[...], w_bf[...], preferred_element_type=F32)
    o_ref[...] = x_ref[...] + gate_ref[...] * y


def _matmul_residual(a, w, x, modr, gate_base, seq):
    t, k = a.shape
    d = x.shape[1]
    tm, tn = ROW_TILE, 512
    tpb = seq // tm
    gate_spec = pl.BlockSpec((None, 1, tn), lambda j, i: (gate_base + i // tpb, 0, j))
    return pl.pallas_call(
        _mm_res_kernel,
        grid=(d // tn, t // tm),
        in_specs=[
            pl.BlockSpec((tm, k), lambda j, i: (i, 0)),
            pl.BlockSpec((k, tn), lambda j, i: (0, j)),
            pl.BlockSpec((tm, tn), lambda j, i: (i, j)),
            gate_spec,
        ],
        out_specs=pl.BlockSpec((tm, tn), lambda j, i: (i, j)),
        out_shape=jax.ShapeDtypeStruct((t, d), F32),
        scratch_shapes=[pltpu.VMEM((k, tn), BF16)],
        compiler_params=_cparams(2),
        name="ssm_out_proj",
    )(a, w, x, modr)


def kernel(x, c, ada_w, ada_b, norm_mix, norm_ffn, ffn_w1, ffn_w2, even_w_in, pool_w, pool_scale, even_w_out,
           ssm_w_in, ssm_conv_w, ssm_conv_b, ssm_dt_bias, ssm_a_log, ssm_d, ssm_norm, ssm_w_out, final_norm):
    batch, seq, d = x.shape
    depth = ada_w.shape[0]
    t = batch * seq
    xf = x.reshape(t, d)

    c_pad = jnp.pad(c, ((0, 8 - batch), (0, 0)))
    mod = _ada_mod(c_pad, ada_w, ada_b)[:, :batch]
    modr = mod.reshape(depth, batch, 6, d).transpose(0, 2, 1, 3).reshape(depth * 6 * batch, 1, d)

    d_inner = ssm_w_out.shape[1]
    n_heads = ssm_dt_bias.shape[1]
    head_of_channel = jnp.arange(d_inner) // SSM_HEAD_DIM
    expand1 = (jnp.arange(LANES)[:, None] == head_of_channel[None, :]).astype(BF16)
    expand = jnp.concatenate([expand1, expand1], axis=0)

    for i in range(depth):
        base = i * 6 * batch
        j = i // 2
        g_mix = norm_mix[i].reshape(1, d)
        g_ffn = norm_ffn[i].reshape(1, d)
        if i % 2 == 0:
            qkv, u = _even_in_proj(xf, g_mix, modr, base, even_w_in[j], seq)
            merged = None
            for gi in range(N_GROUPS):
                merged = _attn_group(qkv, merged, gi, batch, seq)
            xf = _even_out_proj(merged, u, pool_w[j], pool_scale[j].reshape(1, POOL_W), even_w_out[j], xf, modr,
                                base + 2 * batch, seq)
        else:
            w_in_t = jnp.swapaxes(ssm_w_in[j], 0, 1)
            n_main = w_in_t.shape[0] - n_heads
            pad_h = ((0, 0), (0, LANES - n_heads))
            zxbc, dt_raw = _ssm_in_proj(xf, g_mix, modr, base, w_in_t, n_main, seq)
            y = _ssd(zxbc, dt_raw, ssm_conv_w[j], ssm_conv_b[j].reshape(1, -1),
                     jnp.pad(ssm_dt_bias[j].reshape(1, -1), pad_h),
                     jnp.pad(ssm_a_log[j].reshape(1, -1), pad_h),
                     jnp.repeat(ssm_d[j], SSM_HEAD_DIM).reshape(1, d_inner), ssm_norm[j].reshape(1, d_inner),
                     expand, batch, seq)
            xf = _matmul_residual(y, ssm_w_out[j], xf, modr, base + 2 * batch, seq)
        xf = _ffn(xf, g_ffn, modr, base, ffn_w1, ffn_w2, i, final_norm.reshape(1, d), seq,
                  final_norm=(i == depth - 1))
    return xf.reshape(batch, seq, d)
```
